```python
import math
import jax
import jax.numpy as jnp
from jax import lax
import numpy as np


D_MODEL = 1024
BATCH = 2
SEQ = 8192
DEPTH = 2

CTX_LEN = 256
GRID_W = 64
EPS = 1e-6
ROPE_BASE = 10000.0
CHUNK = 128
Q_BLOCK = 128
D_MIX = 2 * D_MODEL

ATT_QK_DIM = 64
ATT_V_DIM = 2 * ATT_QK_DIM
ATT_WIDTH = D_MODEL // 2
ATT_HEADS = ATT_WIDTH // ATT_V_DIM

SSD_HEAD_DIM = 64
SSD_WIDTH = D_MODEL
SSD_HEADS = SSD_WIDTH // SSD_HEAD_DIM
SSD_GROUPS = 2
SSD_HPG = SSD_HEADS // SSD_GROUPS
SSD_STATE = 128
SSD_CONV = 3
SSD_CONV_DIM = SSD_WIDTH + 2 * SSD_GROUPS * SSD_STATE

RET_WIDTH = D_MODEL // 2
RET_HEADS = 4
RET_V_DIM = RET_WIDTH // RET_HEADS
RET_QK_DIM = RET_V_DIM // 2
RET_DECAY_EXP_FWD = (5.0, 6.0, 7.0, 8.0)
RET_DECAY_EXP_BWD = (5.5, 6.5, 7.5, 8.5)

IN_SPLITS = (ATT_HEADS * 2 * ATT_QK_DIM, ATT_HEADS * 2 * ATT_QK_DIM, ATT_WIDTH, ATT_WIDTH,
             SSD_CONV_DIM, 2 * SSD_HEADS, SSD_WIDTH,
             RET_HEADS * RET_QK_DIM, RET_HEADS * RET_QK_DIM, RET_WIDTH, RET_WIDTH)
D_IN_PROJ = sum(IN_SPLITS)

kernel_name = 'hybrid_diffusion_parallel_heads'


def rms_norm(x, gain=None):
    xf = x.astype(jnp.float32)
    y = xf * lax.rsqrt(jnp.mean(xf * xf, axis=-1, keepdims=True) + EPS)
    if gain is not None:
        y = y * gain.astype(jnp.float32)
    return y.astype(x.dtype)


def split_columns(t, sizes):
    parts, off = [], 0
    for s in sizes:
        parts.append(t[..., off:off + s])
        off += s
    return parts


def axial_rope_table(n_rows, dim):
    row = jnp.repeat(jnp.arange(n_rows, dtype=jnp.float32), GRID_W)
    col = jnp.tile(jnp.arange(GRID_W, dtype=jnp.float32), n_rows)
    n_freq = dim // 4
    inv_freq = ROPE_BASE ** (-jnp.arange(n_freq, dtype=jnp.float32) / n_freq)
    ang = jnp.concatenate([row[:, None] * inv_freq, col[:, None] * inv_freq], axis=-1)
    return jnp.cos(ang), jnp.sin(ang)


def apply_rope(x, cos, sin):
    shape = (1, x.shape[1]) + (1,) * (x.ndim - 3) + (cos.shape[-1],)
    cs = cos.reshape(shape).astype(x.dtype)
    sn = sin.reshape(shape).astype(x.dtype)
    x1, x2 = jnp.split(x, 2, axis=-1)
    return jnp.concatenate([x1 * cs - x2 * sn, x1 * sn + x2 * cs], axis=-1)


def depthwise_centred_conv(u, w, b):
    pad = (w.shape[0] - 1) // 2
    y = lax.conv_general_dilated(u, w[:, None, :].astype(u.dtype), window_strides=(1,),
                                 padding=[(pad, pad)], dimension_numbers=('NWC', 'WIO', 'NWC'),
                                 feature_group_count=u.shape[-1])
    return jax.nn.silu(y + b.astype(u.dtype))


def diff_attention(q, k, v, lam):
    s = jnp.einsum('bqhmd,bkhmd->bhmqk', q, k).astype(jnp.float32) * (ATT_QK_DIM ** -0.5)
    p = jax.nn.softmax(s, axis=-1)
    a = p[:, :, 0] - lam * p[:, :, 1]
    return jnp.einsum('bhqk,bkhe->bqhe', a.astype(v.dtype), v)


def diff_attention_blocked(q, k, v, lam):
    b, L = q.shape[:2]
    qb = jnp.moveaxis(q.reshape((b, L // Q_BLOCK, Q_BLOCK) + q.shape[2:]), 1, 0)
    o = lax.map(lambda qq: diff_attention(qq, k, v, lam), qb)
    return jnp.moveaxis(o, 0, 1).reshape(b, L, ATT_HEADS, ATT_V_DIM)


def chunked_scan(q, k, v, log_a, s0, want_y):
    b, L, g, n = q.shape
    hg, p = v.shape[-2:]
    nc = L // CHUNK
    qc = q.reshape(b, nc, CHUNK, g, n)
    kc = k.reshape(b, nc, CHUNK, g, n)
    vc = v.reshape(b, nc, CHUNK, g, hg, p)
    cum = jnp.cumsum(log_a.astype(jnp.float32).reshape(b, nc, CHUNK, g, hg), axis=2)
    total = cum[:, :, -1]
    to_end = jnp.exp(total[:, :, None] - cum)
    chunk_states = jnp.einsum('bcjgn,bcjgh,bcjghp->bcghnp', kc, to_end, vc).astype(jnp.float32)

    def step(s, inp):
        st, tot = inp
        return jnp.exp(tot)[..., None, None] * s + st, s

    s_final, s_in = lax.scan(step, s0, (jnp.moveaxis(chunk_states, 1, 0), jnp.moveaxis(total, 1, 0)))
    if not want_y:
        return None, s_final
    s_in = jnp.moveaxis(s_in, 0, 1)
    lower = jnp.tril(jnp.ones((CHUNK, CHUNK), dtype=bool))[:, :, None, None]
    seg = cum[:, :, :, None] - cum[:, :, None]
    decay = jnp.exp(jnp.where(lower, seg, -jnp.inf))
    scores = jnp.einsum('bcign,bcjgn->bcijg', qc, kc)
    y = (jnp.einsum('bcijg,bcijgh,bcjghp->bcighp', scores, decay, vc)
         + jnp.einsum('bcign,bcigh,bcghnp->bcighp', qc, jnp.exp(cum), s_in))
    return y.reshape(b, L, g, hg, p), s_final


def bidirectional_scan(q, k, v_f, v_b, la_f, la_b, s0_f, s0_b, want_y):
    y_f, s_f = chunked_scan(q, k, v_f, la_f, s0_f, want_y)
    fl = lambda a: jnp.flip(a, axis=1)
    y_b, s_b = chunked_scan(fl(q), fl(k), fl(v_b), fl(la_b), s0_b, want_y)
    y = y_f + fl(y_b) if want_y else None
    return y, s_f, s_b


def retention_log_decay(exps):
    return jnp.log1p(-jnp.exp2(-jnp.asarray(exps, dtype=jnp.float32)))


def hybrid_layer(x, ctx, c, c_ctx, w_ada, b_ada, w_in, w_out, attn_q_norm, attn_k_norm,
                 lambda_q1, lambda_k1, lambda_q2, lambda_k2, attn_subln,
                 ssd_conv_w, ssd_conv_b, ssd_dt_bias, ssd_a_log, ssd_d, ssd_norm, ret_norm,
                 lam_init, cos, sin, need_ctx):
    bsz = x.shape[0]
    shift, scale, gate = jnp.split(jax.nn.silu(c) @ w_ada + b_ada, 3, axis=-1)
    shift_c, scale_c, gate_c = jnp.split(jax.nn.silu(c_ctx) @ w_ada + b_ada, 3, axis=-1)
    h = rms_norm(x) * (1 + scale[:, None]) + shift[:, None]
    hc = rms_norm(ctx) * (1 + scale_c) + shift_c
    aq, ak, av, ag, xbc, dtr, z, rq, rk, rv, rg = split_columns(h @ w_in, IN_SPLITS)
    aq_c, ak_c, av_c, ag_c, xbc_c, dtr_c, z_c, rq_c, rk_c, rv_c, rg_c = split_columns(hc @ w_in, IN_SPLITS)

    lam = (jnp.exp(jnp.sum(lambda_q1 * lambda_k1)) - jnp.exp(jnp.sum(lambda_q2 * lambda_k2))
           + lam_init).astype(jnp.float32)

    def attn_heads(t, gain):
        return rms_norm(t.reshape(bsz, t.shape[1], ATT_HEADS, 2, ATT_QK_DIM), gain)

    def attn_out(o, g):
        o = rms_norm(o, attn_subln) * (1.0 - lam_init)
        return o.reshape(bsz, o.shape[1], ATT_WIDTH) * jax.nn.silu(g)

    k_c = attn_heads(ak_c, attn_k_norm)
    v_c = av_c.reshape(bsz, -1, ATT_HEADS, ATT_V_DIM)
    q_l = apply_rope(attn_heads(aq, attn_q_norm), cos, sin)
    k_l = apply_rope(attn_heads(ak, attn_k_norm), cos, sin)
    v_l = av.reshape(bsz, -1, ATT_HEADS, ATT_V_DIM)
    attn_lat = attn_out(diff_attention_blocked(q_l, jnp.concatenate([k_l, k_c], axis=1),
                                               jnp.concatenate([v_l, v_c], axis=1), lam), ag)

    def ssd_prep(xbc_t, dtr_t):
        u = depthwise_centred_conv(xbc_t, ssd_conv_w, ssd_conv_b)
        L = u.shape[1]
        xs, bm, cm = split_columns(u, (SSD_WIDTH, SSD_GROUPS * SSD_STATE, SSD_GROUPS * SSD_STATE))
        xs = xs.reshape(bsz, L, SSD_GROUPS, SSD_HPG, SSD_HEAD_DIM)
        bm = bm.reshape(bsz, L, SSD_GROUPS, SSD_STATE)
        cm = cm.reshape(bsz, L, SSD_GROUPS, SSD_STATE)
        dt = jax.nn.softplus(dtr_t.astype(jnp.float32).reshape(bsz, L, 2, SSD_HEADS) + ssd_dt_bias)
        la = (dt * -jnp.exp(ssd_a_log)).reshape(bsz, L, 2, SSD_GROUPS, SSD_HPG)
        dt = dt.reshape(bsz, L, 2, SSD_GROUPS, SSD_HPG)
        return (xs, bm, cm, xs * dt[:, :, 0, ..., None], xs * dt[:, :, 1, ..., None],
                la[:, :, 0], la[:, :, 1])

    def ssd_out(y, xs, z_t):
        L = y.shape[1]
        y = y + xs * ssd_d.reshape(SSD_GROUPS, SSD_HPG, 1)
        y = y.reshape(bsz, L, SSD_WIDTH) * jax.nn.silu(z_t)
        y = rms_norm(y.reshape(bsz, L, SSD_GROUPS, SSD_WIDTH // SSD_GROUPS)).reshape(bsz, L, SSD_WIDTH)
        return y * ssd_norm

    zero_s = jnp.zeros((bsz, SSD_GROUPS, SSD_HPG, SSD_STATE, SSD_HEAD_DIM), jnp.float32)
    xs_c, bm_c, cm_c, vf_c, vb_c, laf_c, lab_c = ssd_prep(xbc_c, dtr_c)
    ys_c, ssd_sf, ssd_sb = bidirectional_scan(cm_c, bm_c, vf_c, vb_c, laf_c, lab_c, zero_s, zero_s, need_ctx)
    xs_l, bm_l, cm_l, vf_l, vb_l, laf_l, lab_l = ssd_prep(xbc, dtr)
    ys_l, _, _ = bidirectional_scan(cm_l, bm_l, vf_l, vb_l, laf_l, lab_l, ssd_sf, ssd_sb, True)
    ssd_lat = ssd_out(ys_l, xs_l, z)

    ret_la_f = retention_log_decay(RET_DECAY_EXP_FWD)[:, None]
    ret_la_b = retention_log_decay(RET_DECAY_EXP_BWD)[:, None]

    def ret_prep(q_t, k_t, v_t, rope):
        L = q_t.shape[1]
        q = q_t.reshape(bsz, L, RET_HEADS, RET_QK_DIM)
        k = k_t.reshape(bsz, L, RET_HEADS, RET_QK_DIM) * (RET_QK_DIM ** -0.5)
        if rope:
            q, k = apply_rope(q, cos, sin), apply_rope(k, cos, sin)
        v = v_t.reshape(bsz, L, RET_HEADS, 1, RET_V_DIM)
        shp = (bsz, L, RET_HEADS, 1)
        return q, k, v, jnp.broadcast_to(ret_la_f, shp), jnp.broadcast_to(ret_la_b, shp)

    def ret_out(y, g):
        L = y.shape[1]
        y = rms_norm(y.reshape(bsz, L, RET_HEADS, RET_V_DIM), ret_norm).reshape(bsz, L, RET_WIDTH)
        return y * jax.nn.silu(g)

    zero_r = jnp.zeros((bsz, RET_HEADS, 1, RET_QK_DIM, RET_V_DIM), jnp.float32)
    q_rc, k_rc, v_rc, lf_rc, lb_rc = ret_prep(rq_c, rk_c, rv_c, False)
    yr_c, ret_sf, ret_sb = bidirectional_scan(q_rc, k_rc, v_rc, v_rc, lf_rc, lb_rc, zero_r, zero_r, need_ctx)
    q_rl, k_rl, v_rl, lf_rl, lb_rl = ret_prep(rq, rk, rv, True)
    yr_l, _, _ = bidirectional_scan(q_rl, k_rl, v_rl, v_rl, lf_rl, lb_rl, ret_sf, ret_sb, True)
    ret_lat = ret_out(yr_l, rg)

    mix = jnp.concatenate([attn_lat, ssd_lat, ret_lat], axis=-1).astype(x.dtype)
    x_new = x + gate[:, None] * (mix @ w_out)
    if not need_ctx:
        return x_new, None
    attn_ctx = attn_out(diff_attention(attn_heads(aq_c, attn_q_norm), k_c, v_c, lam), ag_c)
    mix_c = jnp.concatenate([attn_ctx, ssd_out(ys_c, xs_c, z_c), ret_out(yr_c, rg_c)], axis=-1).astype(ctx.dtype)
    ctx_new = ctx + gate_c * (mix_c @ w_out)
    return x_new, ctx_new


def setup_inputs(seed: int = 0) -> dict:
    key = jax.random.key(seed)
    ks = jax.random.split(key, 24)
    f32 = jnp.float32

    def nrm(k, shape, s):
        return jax.random.normal(k, shape, f32) * s

    log_lo, log_hi = math.log(1e-3), math.log(1e-1)
    dt = jnp.exp(jax.random.uniform(ks[15], (DEPTH, 2, SSD_HEADS), f32) * (log_hi - log_lo) + log_lo)
    return {
        'x': nrm(ks[0], (BATCH, SEQ, D_MODEL), 1.0),
        'c': nrm(ks[1], (BATCH, D_MODEL), 1.0),
        'ctx': nrm(ks[2], (BATCH, CTX_LEN, D_MODEL), 1.0),
        'c_ctx': nrm(ks[3], (D_MODEL,), 1.0),
        'w_ada': nrm(ks[4], (DEPTH, D_MODEL, 3 * D_MODEL), 0.5 * D_MODEL ** -0.5),
        'b_ada': nrm(ks[5], (DEPTH, 3 * D_MODEL), 0.02),
        'w_in': nrm(ks[6], (DEPTH, D_MODEL, D_IN_PROJ), D_MODEL ** -0.5),
        'w_out': nrm(ks[7], (DEPTH, D_MIX, D_MODEL), D_MIX ** -0.5),
        'attn_q_norm': 1.0 + nrm(ks[8], (DEPTH, ATT_QK_DIM), 0.02),
        'attn_k_norm': 1.0 + nrm(ks[9], (DEPTH, ATT_QK_DIM), 0.02),
        'lambda_q1': nrm(ks[10], (DEPTH, ATT_QK_DIM), 0.1),
        'lambda_k1': nrm(ks[11], (DEPTH, ATT_QK_DIM), 0.1),
        'lambda_q2': nrm(ks[12], (DEPTH, ATT_QK_DIM), 0.1),
        'lambda_k2': nrm(ks[13], (DEPTH, ATT_QK_DIM), 0.1),
        'attn_subln': 1.0 + nrm(ks[14], (DEPTH, ATT_V_DIM), 0.02),
        'ssd_conv_w': nrm(ks[16], (DEPTH, SSD_CONV, SSD_CONV_DIM), SSD_CONV ** -0.5),
        'ssd_conv_b': nrm(ks[17], (DEPTH, SSD_CONV_DIM), 0.02),
        'ssd_dt_bias': dt + jnp.log(-jnp.expm1(-dt)),
        'ssd_a_log': jnp.log(jax.random.uniform(ks[18], (DEPTH, 2, SSD_HEADS), f32, 1.0, 16.0)),
        'ssd_d': 1.0 + nrm(ks[19], (DEPTH, SSD_HEADS), 0.1),
        'ssd_norm': 1.0 + nrm(ks[20], (DEPTH, SSD_WIDTH), 0.02),
        'ret_norm': 1.0 + nrm(ks[21], (DEPTH, RET_V_DIM), 0.02),
    }


def reference(x, c, ctx, c_ctx, w_ada, b_ada, w_in, w_out, attn_q_norm, attn_k_norm,
              lambda_q1, lambda_k1, lambda_q2, lambda_k2, attn_subln,
              ssd_conv_w, ssd_conv_b, ssd_dt_bias, ssd_a_log, ssd_d, ssd_norm, ret_norm):
    n_rows = x.shape[1] // GRID_W
    cos, sin = axial_rope_table(n_rows, ATT_QK_DIM)
    for layer in range(DEPTH):
        lam_init = 0.8 - 0.6 * math.exp(-0.3 * layer)
        x, ctx = hybrid_layer(x, ctx, c, c_ctx, w_ada[layer], b_ada[layer], w_in[layer], w_out[layer],
                              attn_q_norm[layer], attn_k_norm[layer], lambda_q1[layer], lambda_k1[layer],
                              lambda_q2[layer], lambda_k2[layer], attn_subln[layer],
                              ssd_conv_w[layer], ssd_conv_b[layer], ssd_dt_bias[layer], ssd_a_log[layer],
                              ssd_d[layer], ssd_norm[layer], ret_norm[layer],
                              lam_init, cos, sin, layer < DEPTH - 1)
    return x
```

```python
import functools
import math

import numpy as np
import jax
import jax.numpy as jnp
from jax import lax
from jax.experimental import pallas as pl
from jax.experimental.pallas import tpu as pltpu

F32 = jnp.float32
BF16 = jnp.bfloat16

D = 1024
N_CTX = 256
GRID_W = 64
EPS = 1e-6
ROPE_BASE = 10000.0
CHUNK = 128
QK = 64
ATT_H = 4
SSD_H = 16
SSD_G = 2
SSD_HPG = 8
SSD_P = 64
SSD_N = 128
RET_H = 4
RET_EXP_F = (5.0, 6.0, 7.0, 8.0)
RET_EXP_B = (5.5, 6.5, 7.5, 8.5)
N_MAIN = 6144
LOG2E = math.log2(math.e)
VMEM_LIMIT = 56 * 1024 * 1024


def _silu(v):
    return v * (1.0 / (1.0 + jnp.exp(-v)))


def _split3(v):
    hi = v.astype(BF16)
    r1 = v - hi.astype(F32)
    mid = r1.astype(BF16)
    lo = (r1 - mid.astype(F32)).astype(BF16)
    return hi, mid, lo


def _dot(a, b):
    return jnp.dot(a, b, preferred_element_type=F32)


def _dot_nt(a, b):
    return lax.dot_general(a, b, (((1,), (1,)), ((), ())), preferred_element_type=F32)


def _tri_dot(tri, v):
    hi, mid, lo = _split3(v)
    return _dot(tri, hi) + _dot(tri, mid) + _dot(tri, lo)


def _ada_kernel(c_ref, w_ref, b_ref, o_ref):
    o_ref[...] = _dot(_silu(c_ref[...]), w_ref[...]) + b_ref[...]


def _ada(cvec, w, b):
    n = w.shape[1]
    tn = 768
    return pl.pallas_call(
        _ada_kernel,
        grid=(n // tn,),
        in_specs=[pl.BlockSpec((8, D), lambda j: (0, 0)),
                  pl.BlockSpec((D, tn), lambda j: (0, j)),
                  pl.BlockSpec((1, tn), lambda j: (0, j))],
        out_specs=pl.BlockSpec((8, tn), lambda j: (0, j)),
        out_shape=jax.ShapeDtypeStruct((8, n), F32),
        name="ada",
    )(cvec, w, b)


def _row_mod(ada_ref, b, rows, lo, hi, ctx_row):
    vb = ada_ref[pl.ds(b, 1), lo:hi]
    vc = ada_ref[ctx_row:ctx_row + 1, lo:hi]
    return jnp.where(rows < N_CTX, vc, vb)


def _inproj_kernel(x_ref, ada_ref, w_ref, wdt_ref, main_ref, dt_ref, *, tm, tn, ctx_row):
    b = pl.program_id(0)
    i = pl.program_id(1)
    x = x_ref[...]
    xn = x * lax.rsqrt(jnp.mean(x * x, axis=-1, keepdims=True) + EPS)
    rows = i * tm + lax.broadcasted_iota(jnp.int32, (tm, 1), 0)
    shift = _row_mod(ada_ref, b, rows, 0, D, ctx_row)
    scale = _row_mod(ada_ref, b, rows, D, 2 * D, ctx_row)
    h = (xn * (1.0 + scale) + shift).astype(BF16)
    for c in range(N_MAIN // tn):
        main_ref[:, c * tn:(c + 1) * tn] = _dot(h, w_ref[:, c * tn:(c + 1) * tn]).astype(BF16)
    dt_ref[...] = _dot(h, wdt_ref[...])


def _inproj(xcat, ada, w_main, w_dt, ctx_row):
    bsz, t, _ = xcat.shape
    tm, tn = 384, 768
    return pl.pallas_call(
        functools.partial(_inproj_kernel, tm=tm, tn=tn, ctx_row=ctx_row),
        grid=(bsz, t // tm),
        in_specs=[pl.BlockSpec((None, tm, D), lambda b, i: (b, i, 0)),
                  pl.BlockSpec((8, 3 * D), lambda b, i: (0, 0)),
                  pl.BlockSpec((D, N_MAIN), lambda b, i: (0, 0)),
                  pl.BlockSpec((D, 128), lambda b, i: (0, 0))],
        out_specs=[pl.BlockSpec((None, tm, N_MAIN), lambda b, i: (b, i, 0)),
                   pl.BlockSpec((None, tm, 128), lambda b, i: (b, i, 0))],
        out_shape=[jax.ShapeDtypeStruct((bsz, t, N_MAIN), BF16),
                   jax.ShapeDtypeStruct((bsz, t, 128), F32)],
        compiler_params=pltpu.CompilerParams(
            dimension_semantics=("arbitrary", "arbitrary"), vmem_limit_bytes=VMEM_LIMIT),
        name="inproj",
    )(xcat, ada, w_main, w_dt)


def _rope128(x, cos, sin, lo_half):
    partner = jnp.where(lo_half, pltpu.roll(x, 96, 1), pltpu.roll(x, 32, 1))
    return x * cos + partner * sin


def _attnprep_kernel(q_ref, k_ref, v_ref, cos_ref, sin_ref, gq_ref, gk_ref, gmat_ref,
                     qo_ref, ko_ref, vt_ref, *, tm):
    cos = cos_ref[...]
    sin = sin_ref[...]
    lane = lax.broadcasted_iota(jnp.int32, (tm, 128), 1)
    lo_half = (lane % QK) < (QK // 2)
    gmat = gmat_ref[...]

    def norm_rope(src_ref, gain_ref, out_ref, out_scale):
        x = src_ref[...].astype(F32)
        ss = _dot((x * x).astype(BF16), gmat)
        y = x * lax.rsqrt(ss * (1.0 / QK) + EPS) * gain_ref[...]
        for g in range(4):
            yg = _rope128(y[:, g * 128:(g + 1) * 128], cos, sin, lo_half)
            out_ref[:, g * 128:(g + 1) * 128] = (yg * out_scale).astype(BF16)

    norm_rope(q_ref, gq_ref, qo_ref, (QK ** -0.5) * LOG2E)
    norm_rope(k_ref, gk_ref, ko_ref, 1.0)
    for h in range(ATT_H):
        vt_ref[h] = v_ref[:, h * 128:(h + 1) * 128].astype(F32).T.astype(BF16)


def _attnprep(main, cos_t, sin_t, gq, gk, gmat, tm):
    bsz, t, _ = main.shape
    nb = t // tm
    return pl.pallas_call(
        functools.partial(_attnprep_kernel, tm=tm),
        grid=(bsz, nb),
        in_specs=[pl.BlockSpec((None, tm, 512), lambda b, i: (b, i, 0)),
                  pl.BlockSpec((None, tm, 512), lambda b, i: (b, i, 1)),
                  pl.BlockSpec((None, tm, 512), lambda b, i: (b, i, 2)),
                  pl.BlockSpec((tm, 128), lambda b, i: (i, 0)),
                  pl.BlockSpec((tm, 128), lambda b, i: (i, 0)),
                  pl.BlockSpec((1, 512), lambda b, i: (0, 0)),
                  pl.BlockSpec((1, 512), lambda b, i: (0, 0)),
                  pl.BlockSpec((512, 512), lambda b, i: (0, 0))],
        out_specs=[pl.BlockSpec((None, tm, 512), lambda b, i: (b, i, 0)),
                   pl.BlockSpec((None, tm, 512), lambda b, i: (b, i, 0)),
                   pl.BlockSpec((None, ATT_H, None, 128, tm), lambda b, i: (b, 0, i, 0, 0))],
        out_shape=[jax.ShapeDtypeStruct((bsz, t, 512), BF16),
                   jax.ShapeDtypeStruct((bsz, t, 512), BF16),
                   jax.ShapeDtypeStruct((bsz, ATT_H, nb, 128, tm), BF16)],
        compiler_params=pltpu.CompilerParams(
            dimension_semantics=("arbitrary", "arbitrary"), vmem_limit_bytes=VMEM_LIMIT),
        name="attnprep",
    )(main, main, main, cos_t, sin_t, gq, gk, gmat)


def _attn_kernel(q_ref, k_ref, vt_ref, gate_ref, lam_ref, subln_ref, o_ref,
                 q2_ref, m_ref, l_ref, acc_ref, *, tq, tk, nkv, lam_init):
    qi = pl.program_id(2)
    q = q_ref[...]
    lane = lax.broadcasted_iota(jnp.int32, (tq, 128), 1)
    zero = jnp.zeros_like(q)
    q2_ref[0:tq, :] = jnp.where(lane < QK, q, zero)
    q2_ref[tq:2 * tq, :] = jnp.where(lane >= QK, q, zero)
    m_ref[...] = jnp.full((1, 2 * tq), -jnp.inf, F32)
    l_ref[...] = jnp.zeros((1, 2 * tq), F32)
    acc_ref[...] = jnp.zeros((128, 2 * tq), F32)

    def step(j, ctx_only):
        k = k_ref[pl.ds(j * tk, tk), :]
        st = _dot_nt(k, q2_ref[...])
        if ctx_only:
            krow = j * tk + lax.broadcasted_iota(jnp.int32, (tk, 1), 0)
            st = jnp.where(krow < N_CTX, st, -jnp.inf)
        m_old = m_ref[...]
        m_new = jnp.maximum(m_old, jnp.max(st, axis=0, keepdims=True))
        alpha = jnp.exp2(m_old - m_new)
        p = jnp.exp2(st - m_new)
        l_ref[...] = alpha * l_ref[...] + jnp.sum(p, axis=0, keepdims=True)
        acc_ref[...] = alpha * acc_ref[...] + _dot(vt_ref[j], p.astype(BF16))
        m_ref[...] = m_new

    @pl.when(qi == 0)
    def _():
        step(0, True)

    @pl.when(qi != 0)
    def _():
        def body(j, carry):
            step(j, False)
            return carry
        lax.fori_loop(0, nkv, body, 0)

    lam_v = lam_ref[...]
    lam = (jnp.exp(jnp.sum(lam_v[0:1] * lam_v[1:2], axis=-1, keepdims=True))
           - jnp.exp(jnp.sum(lam_v[2:3] * lam_v[3:4], axis=-1, keepdims=True)) + lam_init)
    inv = 1.0 / l_ref[...]
    acc = acc_ref[...]
    o = acc[:, 0:tq] * inv[:, 0:tq] - lam * (acc[:, tq:2 * tq] * inv[:, tq:2 * tq])
    o = o * lax.rsqrt(jnp.mean(o * o, axis=0, keepdims=True) + EPS)
    o = o * subln_ref[...] * (1.0 - lam_init)
    o_ref[...] = (o.T * _silu(gate_ref[...].astype(F32))).astype(BF16)


def _attn(qp, kp, vt, main, lam_vec, subln, lam_init, tq, tk):
    bsz, t, _ = qp.shape
    nkv = t // tk
    assert tq == N_CTX and tk >= N_CTX
    return pl.pallas_call(
        functools.partial(_attn_kernel, tq=tq, tk=tk, nkv=nkv, lam_init=lam_init),
        grid=(bsz, ATT_H, t // tq),
        in_specs=[pl.BlockSpec((None, tq, 128), lambda b, h, i: (b, i, h)),
                  pl.BlockSpec((None, t, 128), lambda b, h, i: (b, 0, h)),
                  pl.BlockSpec((None, None, nkv, 128, tk), lambda b, h, i: (b, h, 0, 0, 0)),
                  pl.BlockSpec((None, tq, 128), lambda b, h, i: (b, i, 12 + h)),
                  pl.BlockSpec((4, QK), lambda b, h, i: (0, 0)),
                  pl.BlockSpec((128, 1), lambda b, h, i: (0, 0))],
        out_specs=pl.BlockSpec((None, tq, 128), lambda b, h, i: (b, i, h)),
        out_shape=jax.ShapeDtypeStruct((bsz, t, 512), BF16),
        scratch_shapes=[pltpu.VMEM((2 * tq, 128), BF16),
                        pltpu.VMEM((1, 2 * tq), F32),
                        pltpu.VMEM((1, 2 * tq), F32),
                        pltpu.VMEM((128, 2 * tq), F32)],
        compiler_params=pltpu.CompilerParams(
            dimension_semantics=("arbitrary", "arbitrary", "arbitrary"),
            vmem_limit_bytes=VMEM_LIMIT),
        name="attn",
    )(qp, kp, vt, main, lam_vec, subln)


def _conv_kernel(x_ref, prev_ref, next_ref, w_ref, b_ref, o_ref, *, tm, t):
    i = pl.program_id(1)
    x = x_ref[...].astype(F32)
    loc = lax.broadcasted_iota(jnp.int32, (tm, 1), 0)
    row = i * tm + loc
    xm1 = pltpu.roll(x, 1, 0)
    xm1 = jnp.where(loc == 0, prev_ref[7:8, :].astype(F32), xm1)
    xm1 = jnp.where((row == 0) | (row == N_CTX), 0.0, xm1)
    xp1 = pltpu.roll(x, tm - 1, 0)
    xp1 = jnp.where(loc == tm - 1, next_ref[0:1, :].astype(F32), xp1)
    xp1 = jnp.where((row == t - 1) | (row == N_CTX - 1), 0.0, xp1)
    w = w_ref[...]
    y = w[0:1] * xm1 + w[1:2] * x + w[2:3] * xp1 + b_ref[...]
    o_ref[...] = _silu(y).astype(BF16)


def _conv(main, w, b, tm):
    bsz, t, _ = main.shape
    r8 = tm // 8
    last8 = t // 8 - 1
    return pl.pallas_call(
        functools.partial(_conv_kernel, tm=tm, t=t),
        grid=(bsz, t // tm, 3),
        in_specs=[pl.BlockSpec((None, tm, 512), lambda b, i, j: (b, i, 6 + j)),
                  pl.BlockSpec((None, 8, 512), lambda b, i, j: (b, jnp.maximum(i * r8 - 1, 0), 6 + j)),
                  pl.BlockSpec((None, 8, 512), lambda b, i, j: (b, jnp.minimum((i + 1) * r8, last8), 6 + j)),
                  pl.BlockSpec((3, 512), lambda b, i, j: (0, j)),
                  pl.BlockSpec((1, 512), lambda b, i, j: (0, j))],
        out_specs=pl.BlockSpec((None, tm, 512), lambda b, i, j: (b, i, j)),
        out_shape=jax.ShapeDtypeStruct((bsz, t, 1536), BF16),
        compiler_params=pltpu.CompilerParams(
            dimension_semantics=("arbitrary", "arbitrary", "arbitrary"),
            vmem_limit_bytes=VMEM_LIMIT),
        name="conv",
    )(main, main, main, w, b)


def _chunk_index(s, nchunks, backward):
    if not backward:
        return s
    nctx = N_CTX // CHUNK
    return jnp.where(s < nctx, nctx - 1 - s, nchunks - 1 + nctx - s)


def _tri_mask(backward):
    r = lax.broadcasted_iota(jnp.int32, (CHUNK, CHUNK), 0)
    c = lax.broadcasted_iota(jnp.int32, (CHUNK, CHUNK), 1)
    return (c >= r) if backward else (c <= r)


def _ssd_kernel(*refs, backward):
    if backward:
        (u_ref, dtr_ref, bias_ref, alog_ref, z_ref, yf_ref, dskip_ref, gain_ref,
         o_ref, s_ref) = refs
    else:
        u_ref, dtr_ref, bias_ref, alog_ref, o_ref, s_ref = refs
    step = pl.program_id(1)

    @pl.when(step == 0)
    def _():
        s_ref[...] = jnp.zeros_like(s_ref)

    mask = _tri_mask(backward)
    tri = jnp.where(mask, 1.0, 0.0).astype(BF16)
    pre = dtr_ref[...] + bias_ref[...]
    dt = jnp.maximum(pre, 0.0) + jnp.log1p(jnp.exp(-jnp.abs(pre)))
    la = dt * (-jnp.exp(alog_ref[...]))
    cum = _tri_dot(tri, la)
    cum_t = cum.T
    edge = 0 if backward else CHUNK - 1
    tot = cum[edge:edge + 1, :]
    ecum = jnp.exp(cum)
    first = lax.broadcasted_iota(jnp.int32, (CHUNK, 128), 1) < SSD_P
    first_row = lax.broadcasted_iota(jnp.int32, (1, 128), 1) < SSD_P
    col0 = SSD_H if backward else 0

    def pair_cols(v, c):
        rows = v.shape[0]
        a = jnp.broadcast_to(v[:, c:c + 1], (rows, 128))
        b = jnp.broadcast_to(v[:, c + 1:c + 2], (rows, 128))
        return jnp.where(first if rows == CHUNK else first_row, a, b)

    for g in range(SSD_G):
        k = u_ref[:, 1024 + g * SSD_N:1024 + (g + 1) * SSD_N]
        q = u_ref[:, 1280 + g * SSD_N:1280 + (g + 1) * SSD_N]
        scores = _dot_nt(q, k)
        k_t = k.astype(F32).T.astype(BF16)
        y_inter = _dot(q, s_ref[g].astype(BF16))
        for pp in range(SSD_HPG // 2):
            h0 = g * SSD_HPG + 2 * pp
            c0 = col0 + h0
            off = h0 * SSD_P
            xs = u_ref[:, off:off + 128].astype(F32)
            v = xs * pair_cols(dt, c0)
            vb = v.astype(BF16)
            zero = jnp.zeros_like(vb)
            y = pair_cols(ecum, c0) * y_inter[:, pp * 128:(pp + 1) * 128]
            for hh in range(2):
                c = c0 + hh
                seg = jnp.broadcast_to(cum[:, c:c + 1], (CHUNK, CHUNK)) - cum_t[c:c + 1, :]
                dec = jnp.exp(jnp.where(mask, seg, -jnp.inf))
                vh = jnp.where(first if hh == 0 else jnp.logical_not(first), vb, zero)
                y = y + _dot((scores * dec).astype(BF16), vh)
            w = jnp.exp(pair_cols(tot, c0) - pair_cols(cum, c0))
            s_old = s_ref[g, :, pp * 128:(pp + 1) * 128]
            s_ref[g, :, pp * 128:(pp + 1) * 128] = (
                jnp.exp(pair_cols(tot, c0)) * s_old + _dot(k_t, (v * w).astype(BF16)))
            if backward:
                y = y + yf_ref[:, off:off + 128] + xs * dskip_ref[:, off:off + 128]
                y = y * _silu(z_ref[:, off:off + 128].astype(F32))
            o_ref[:, off:off + 128] = y
        if backward:
            lo, hi = g * 512, (g + 1) * 512
            yg = o_ref[:, lo:hi]
            yg = yg * lax.rsqrt(jnp.mean(yg * yg, axis=-1, keepdims=True) + EPS)
            o_ref[:, lo:hi] = yg * gain_ref[:, lo:hi]


def _ssd(u, dtr, bias, alog, backward, z_main=None, yf=None, dskip=None, gain=None):
    bsz, t, _ = u.shape
    nchunks = t // CHUNK
    idx = functools.partial(_chunk_index, nchunks=nchunks, backward=backward)
    row = lambda b, s: (b, idx(s), 0)
    const = lambda b, s: (0, 0)
    in_specs = [pl.BlockSpec((None, CHUNK, 1536), row),
                pl.BlockSpec((None, CHUNK, 128), row),
                pl.BlockSpec((1, 128), const),
                pl.BlockSpec((1, 128), const)]
    args = [u, dtr, bias, alog]
    if backward:
        in_specs += [pl.BlockSpec((None, CHUNK, 1024), lambda b, s: (b, idx(s), 2)),
                     pl.BlockSpec((None, CHUNK, 1024), row),
                     pl.BlockSpec((1, 1024), const),
                     pl.BlockSpec((1, 1024), const)]
        args += [z_main, yf, dskip, gain]
    return pl.pallas_call(
        functools.partial(_ssd_kernel, backward=backward),
        grid=(bsz, nchunks),
        in_specs=in_specs,
        out_specs=pl.BlockSpec((None, CHUNK, 1024), row),
        out_shape=jax.ShapeDtypeStruct((bsz, t, 1024), F32),
        scratch_shapes=[pltpu.VMEM((SSD_G, SSD_N, 512), F32)],
        compiler_params=pltpu.CompilerParams(
            dimension_semantics=("arbitrary", "arbitrary"), vmem_limit_bytes=VMEM_LIMIT),
        name="ssd_bwd" if backward else "ssd_fwd",
    )(*args)


def _ret_tables(backward):
    exps = RET_EXP_B if backward else RET_EXP_F
    lg = np.log1p(-np.exp2(-np.asarray(exps, np.float64)))
    i = np.arange(CHUNK, dtype=np.float64)
    if backward:
        cum = (CHUNK - i)[None, :] * lg[:, None]
        tot = cum[:, 0]
        msk = i[None, :] >= i[:, None]
    else:
        cum = (i + 1.0)[None, :] * lg[:, None]
        tot = cum[:, -1]
        msk = i[None, :] <= i[:, None]
    dec = np.where(msk[None], np.exp(cum[:, :, None] - cum[:, None, :]), 0.0)
    inter = np.broadcast_to(np.exp(cum)[:, :, None], (RET_H, CHUNK, 128))
    toend = np.broadcast_to(np.exp(tot[:, None] - cum)[:, :, None], (RET_H, CHUNK, 128))
    tables = np.stack([dec, inter, toend], axis=1).astype(np.float32)
    return tables, [float(np.exp(v)) for v in tot]


def _ret_kernel(*refs, backward, etot):
    if backward:
        qk_ref, v_ref, cos_ref, sin_ref, tab_ref, g_ref, yf_ref, gain_ref, o_ref, s_ref = refs
    else:
        qk_ref, v_ref, cos_ref, sin_ref, tab_ref, o_ref, s_ref = refs
    step = pl.program_id(1)

    @pl.when(step == 0)
    def _():
        s_ref[...] = jnp.zeros_like(s_ref)

    cos = cos_ref[...]
    sin = sin_ref[...]
    lane = lax.broadcasted_iota(jnp.int32, (CHUNK, 128), 1)
    lo_half = (lane % QK) < (QK // 2)
    first = lane < QK
    srow = lax.broadcasted_iota(jnp.int32, (128, 128), 0) < QK
    for pp in range(RET_H // 2):
        q = _rope128(qk_ref[:, pp * 128:(pp + 1) * 128].astype(F32), cos, sin, lo_half)
        k = _rope128(qk_ref[:, 256 + pp * 128:256 + (pp + 1) * 128].astype(F32) * (QK ** -0.5),
                     cos, sin, lo_half)
        qb = q.astype(BF16)
        kb = k.astype(BF16)
        k_t = kb.astype(F32).T.astype(BF16)
        zero = jnp.zeros_like(qb)
        s_pair = s_ref[pp]
        s_bf = s_pair.astype(BF16)
        s_new = jnp.where(srow, etot[2 * pp], etot[2 * pp + 1]) * s_pair
        for hh in range(2):
            h = 2 * pp + hh
            sel = first if hh == 0 else jnp.logical_not(first)
            qh = jnp.where(sel, qb, zero)
            scores = _dot_nt(qh, kb)
            vh = v_ref[:, h * 128:(h + 1) * 128]
            y = _dot((scores * tab_ref[h, 0]).astype(BF16), vh) + tab_ref[h, 1] * _dot(qh, s_bf)
            upd = _dot(k_t, (vh.astype(F32) * tab_ref[h, 2]).astype(BF16))
            s_new = s_new + jnp.where(srow if hh == 0 else jnp.logical_not(srow), upd, 0.0)
            if backward:
                y = y + yf_ref[:, h * 128:(h + 1) * 128]
                y = y * lax.rsqrt(jnp.mean(y * y, axis=-1, keepdims=True) + EPS) * gain_ref[...]
                y = y * _silu(g_ref[:, h * 128:(h + 1) * 128].astype(F32))
            o_ref[:, h * 128:(h + 1) * 128] = y.astype(o_ref.dtype)
        s_ref[pp] = s_new


def _ret(main, cos_t, sin_t, backward, yf=None, gain=None):
    bsz, t, _ = main.shape
    nchunks = t // CHUNK
    tables, etot = _ret_tables(backward)
    idx = functools.partial(_chunk_index, nchunks=nchunks, backward=backward)
    row = lambda b, s: (b, idx(s), 0)
    in_specs = [pl.BlockSpec((None, CHUNK, 512), lambda b, s: (b, idx(s), 9)),
                pl.BlockSpec((None, CHUNK, 512), lambda b, s: (b, idx(s), 10)),
                pl.BlockSpec((CHUNK, 128), lambda b, s: (idx(s), 0)),
                pl.BlockSpec((CHUNK, 128), lambda b, s: (idx(s), 0)),
                pl.BlockSpec((RET_H, 3, CHUNK, 128), lambda b, s: (0, 0, 0, 0))]
    args = [main, main, cos_t, sin_t, jnp.asarray(tables)]
    if backward:
        in_specs += [pl.BlockSpec((None, CHUNK, 512), lambda b, s: (b, idx(s), 11)),
                     pl.BlockSpec((None, CHUNK, 512), row),
                     pl.BlockSpec((1, 128), lambda b, s: (0, 0))]
        args += [main, yf, gain]
    return pl.pallas_call(
        functools.partial(_ret_kernel, backward=backward, etot=etot),
        grid=(bsz, nchunks),
        in_specs=in_specs,
        out_specs=pl.BlockSpec((None, CHUNK, 512), row),
        out_shape=jax.ShapeDtypeStruct((bsz, t, 512), BF16 if backward else F32),
        scratch_shapes=[pltpu.VMEM((RET_H // 2, 128, 128), F32)],
        compiler_params=pltpu.CompilerParams(
            dimension_semantics=("arbitrary", "arbitrary"), vmem_limit_bytes=VMEM_LIMIT),
        name="ret_bwd" if backward else "ret_fwd",
    )(*args)


def _outproj_kernel(x_ref, a_ref, s_ref, r_ref, ada_ref, w_ref, o_ref, *, tm, blk0, ctx_row):
    b = pl.program_id(0)
    i = pl.program_id(1) + blk0
    acc = _dot(a_ref[...], w_ref[0:512, :])
    acc += _dot(s_ref[...].astype(BF16), w_ref[512:1536, :])
    acc += _dot(r_ref[...], w_ref[1536:2048, :])
    rows = i * tm + lax.broadcasted_iota(jnp.int32, (tm, 1), 0)
    gate = _row_mod(ada_ref, b, rows, 2 * D, 3 * D, ctx_row)
    o_ref[...] = x_ref[...] + gate * acc


def _outproj(xcat, attn_o, ssd_o, ret_o, ada, w_out, ctx_row, latent_only):
    bsz, t, _ = xcat.shape
    tm = N_CTX
    blk0 = 1 if latent_only else 0
    nb = t // tm - blk0
    row = lambda b, i: (b, i + blk0, 0)
    return pl.pallas_call(
        functools.partial(_outproj_kernel, tm=tm, blk0=blk0, ctx_row=ctx_row),
        grid=(bsz, nb),
        in_specs=[pl.BlockSpec((None, tm, D), row),
                  pl.BlockSpec((None, tm, 512), row),
                  pl.BlockSpec((None, tm, 1024), row),
                  pl.BlockSpec((None, tm, 512), row),
                  pl.BlockSpec((8, 3 * D), lambda b, i: (0, 0)),
                  pl.BlockSpec((2 * D, D), lambda b, i: (0, 0))],
        out_specs=pl.BlockSpec((None, tm, D), lambda b, i: (b, i, 0)),
        out_shape=jax.ShapeDtypeStruct((bsz, nb * tm, D), F32),
        compiler_params=pltpu.CompilerParams(
            dimension_semantics=("arbitrary", "arbitrary"), vmem_limit_bytes=VMEM_LIMIT),
        name="outproj",
    )(xcat, attn_o, ssd_o, ret_o, ada, w_out)


def _rope_tables(seq):
    n_rows = seq // GRID_W
    row = np.repeat(np.arange(n_rows, dtype=np.float32), GRID_W)
    col = np.tile(np.arange(GRID_W, dtype=np.float32), n_rows)
    n_freq = QK // 4
    inv_freq = (np.float32(ROPE_BASE) ** (-np.arange(n_freq, dtype=np.float32) / n_freq)).astype(np.float32)
    ang = np.concatenate([row[:, None] * inv_freq, col[:, None] * inv_freq], axis=-1).astype(np.float32)
    cos = np.concatenate([np.ones((N_CTX, QK // 2)), np.cos(ang.astype(np.float64))], axis=0)
    sin = np.concatenate([np.zeros((N_CTX, QK // 2)), np.sin(ang.astype(np.float64))], axis=0)
    cos_t = np.tile(cos, (1, 4)).astype(np.float32)
    sin_t = np.tile(np.concatenate([-sin, sin], axis=1), (1, 2)).astype(np.float32)
    return jnp.asarray(cos_t), jnp.asarray(sin_t)


_MAIN_COLS = ((0, 2048), (3616, 4640), (2048, 3584), (4640, 6176))
_DT_COLS = (3584, 3616)


def _pad_lanes(v, n=128):
    v = v.reshape(1, -1)
    return jnp.pad(v, ((0, 0), (0, n - v.shape[1])))


def kernel(x, c, ctx, c_ctx, w_ada, b_ada, w_in, w_out, attn_q_norm, attn_k_norm,
           lambda_q1, lambda_k1, lambda_q2, lambda_k2, attn_subln,
           ssd_conv_w, ssd_conv_b, ssd_dt_bias, ssd_a_log, ssd_d, ssd_norm, ret_norm):
    bsz, seq, _ = x.shape
    depth = w_ada.shape[0]
    assert bsz + 1 <= 8 and ctx.shape[1] == N_CTX
    ctx_row = bsz
    xcat = jnp.concatenate([ctx, x], axis=1)
    cvec = jnp.zeros((8, D), F32).at[0:bsz].set(c).at[ctx_row].set(c_ctx)
    cos_t, sin_t = _rope_tables(seq)
    gsel = np.arange(512) // QK
    gmat = jnp.asarray((gsel[:, None] == gsel[None, :]).astype(np.float32), BF16)

    for layer in range(depth):
        last = layer == depth - 1
        lam_init = 0.8 - 0.6 * math.exp(-0.3 * layer)
        w_l = w_in[layer]
        w_main = jnp.concatenate([w_l[:, a:b] for a, b in _MAIN_COLS], axis=1).astype(BF16)
        w_dt = jnp.pad(w_l[:, _DT_COLS[0]:_DT_COLS[1]], ((0, 0), (0, 96))).astype(BF16)

        ada = _ada(cvec, w_ada[layer], b_ada[layer].reshape(1, -1))
        main, dtr = _inproj(xcat, ada, w_main, w_dt, ctx_row)

        qp, kp, vt = _attnprep(main, cos_t, sin_t,
                               jnp.tile(attn_q_norm[layer], 8).reshape(1, 512),
                               jnp.tile(attn_k_norm[layer], 8).reshape(1, 512), gmat, tm=768)
        lam_vec = jnp.stack([lambda_q1[layer], lambda_k1[layer], lambda_q2[layer], lambda_k2[layer]])
        attn_o = _attn(qp, kp, vt, main, lam_vec, attn_subln[layer].reshape(128, 1), lam_init,
                       tq=N_CTX, tk=768)

        u = _conv(main, ssd_conv_w[layer], ssd_conv_b[layer].reshape(1, -1), tm=768)
        bias = _pad_lanes(ssd_dt_bias[layer])
        alog = _pad_lanes(ssd_a_log[layer])
        ssd_f = _ssd(u, dtr, bias, alog, backward=False)
        ssd_o = _ssd(u, dtr, bias, alog, backward=True, z_main=main, yf=ssd_f,
                     dskip=jnp.repeat(ssd_d[layer], SSD_P).reshape(1, 1024),
                     gain=ssd_norm[layer].reshape(1, 1024))

        ret_f = _ret(main, cos_t, sin_t, backward=False)
        ret_o = _ret(main, cos_t, sin_t, backward=True, yf=ret_f,
                     gain=ret_norm[layer].reshape(1, 128))

        xcat = _outproj(xcat, attn_o, ssd_o, ret_o, ada, w_out[layer].astype(BF16), ctx_row,
                        latent_only=last)
    return xcat
```

```python
import functools
import math

import numpy as np
import jax
import jax.numpy as jnp
from jax import lax
from jax.experimental import pallas as pl
from jax.experimental.pallas import tpu as pltpu

F32 = jnp.float32
BF16 = jnp.bfloat16

D = 1024
N_CTX = 256
GRID_W = 64
EPS = 1e-6
ROPE_BASE = 10000.0
CHUNK = 128
QK = 64
ATT_H = 4
SSD_H = 16
SSD_G = 2
SSD_HPG = 8
SSD_P = 64
SSD_N = 128
RET_H = 4
RET_EXP_F = (5.0, 6.0, 7.0, 8.0)
RET_EXP_B = (5.5, 6.5, 7.5, 8.5)
N_MAIN = 6144
LOG2E = math.log2(math.e)
VMEM_LIMIT = 56 * 1024 * 1024


def _silu(v):
    return v * (1.0 / (1.0 + jnp.exp(-v)))


def _split3(v):
    hi = v.astype(BF16)
    r1 = v - hi.astype(F32)
    mid = r1.astype(BF16)
    lo = (r1 - mid.astype(F32)).astype(BF16)
    return hi, mid, lo


def _dot(a, b):
    return jnp.dot(a, b, preferred_element_type=F32)


def _dot_nt(a, b):
    return lax.dot_general(a, b, (((1,), (1,)), ((), ())), preferred_element_type=F32)


def _tri_dot(tri, v):
    hi, mid, lo = _split3(v)
    return _dot(tri, hi) + _dot(tri, mid) + _dot(tri, lo)


def _ada_kernel(c_ref, w_ref, b_ref, o_ref):
    o_ref[...] = _dot(_silu(c_ref[...]), w_ref[...]) + b_ref[...]


def _ada(cvec, w, b):
    n = w.shape[1]
    tn = 768
    return pl.pallas_call(
        _ada_kernel,
        grid=(n // tn,),
        in_specs=[pl.BlockSpec((8, D), lambda j: (0, 0)),
                  pl.BlockSpec((D, tn), lambda j: (0, j)),
                  pl.BlockSpec((1, tn), lambda j: (0, j))],
        out_specs=pl.BlockSpec((8, tn), lambda j: (0, j)),
        out_shape=jax.ShapeDtypeStruct((8, n), F32),
        name="ada",
    )(cvec, w, b)


def _row_mod(ada_ref, b, rows, lo, hi, ctx_row):
    vb = ada_ref[pl.ds(b, 1), lo:hi]
    vc = ada_ref[ctx_row:ctx_row + 1, lo:hi]
    return jnp.where(rows < N_CTX, vc, vb)


def _inproj_kernel(x_ref, ada_ref, w_ref, wdt_ref, main_ref, dt_ref, *, tm, tn, ctx_row):
    b = pl.program_id(0)
    i = pl.program_id(1)
    x = x_ref[...]
    xn = x * lax.rsqrt(jnp.mean(x * x, axis=-1, keepdims=True) + EPS)
    rows = i * tm + lax.broadcasted_iota(jnp.int32, (tm, 1), 0)
    shift = _row_mod(ada_ref, b, rows, 0, D, ctx_row)
    scale = _row_mod(ada_ref, b, rows, D, 2 * D, ctx_row)
    h = (xn * (1.0 + scale) + shift).astype(BF16)
    for c in range(N_MAIN // tn):
        main_ref[:, c * tn:(c + 1) * tn] = _dot(h, w_ref[:, c * tn:(c + 1) * tn]).astype(BF16)
    dt_ref[...] = _dot(h, wdt_ref[...])


def _inproj(xcat, ada, w_main, w_dt, ctx_row):
    bsz, t, _ = xcat.shape
    tm, tn = 384, 768
    return pl.pallas_call(
        functools.partial(_inproj_kernel, tm=tm, tn=tn, ctx_row=ctx_row),
        grid=(bsz, t // tm),
        in_specs=[pl.BlockSpec((None, tm, D), lambda b, i: (b, i, 0)),
                  pl.BlockSpec((8, 3 * D), lambda b, i: (0, 0)),
                  pl.BlockSpec((D, N_MAIN), lambda b, i: (0, 0)),
                  pl.BlockSpec((D, 128), lambda b, i: (0, 0))],
        out_specs=[pl.BlockSpec((None, tm, N_MAIN), lambda b, i: (b, i, 0)),
                   pl.BlockSpec((None, tm, 128), lambda b, i: (b, i, 0))],
        out_shape=[jax.ShapeDtypeStruct((bsz, t, N_MAIN), BF16),
                   jax.ShapeDtypeStruct((bsz, t, 128), F32)],
        compiler_params=pltpu.CompilerParams(
            dimension_semantics=("arbitrary", "arbitrary"), vmem_limit_bytes=VMEM_LIMIT),
        name="inproj",
    )(xcat, ada, w_main, w_dt)


def _rope128(x, cos, sin, lo_half):
    partner = jnp.where(lo_half, pltpu.roll(x, 96, 1), pltpu.roll(x, 32, 1))
    return x * cos + partner * sin


def _attnprep_kernel(q_ref, k_ref, v_ref, cos_ref, sin_ref, gq_ref, gk_ref, gmat_ref,
                     qo_ref, ko_ref, vt_ref, *, tm):
    cos = cos_ref[...]
    sin = sin_ref[...]
    lane = lax.broadcasted_iota(jnp.int32, (tm, 128), 1)
    lo_half = (lane % QK) < (QK // 2)
    gmat = gmat_ref[...]

    def norm_rope(src_ref, gain_ref, out_ref, out_scale):
        x = src_ref[...].astype(F32)
        ss = _dot((x * x).astype(BF16), gmat)
        y = x * lax.rsqrt(ss * (1.0 / QK) + EPS) * gain_ref[...]
        for g in range(4):
            yg = _rope128(y[:, g * 128:(g + 1) * 128], cos, sin, lo_half)
            out_ref[:, g * 128:(g + 1) * 128] = (yg * out_scale).astype(BF16)

    norm_rope(q_ref, gq_ref, qo_ref, (QK ** -0.5) * LOG2E)
    norm_rope(k_ref, gk_ref, ko_ref, 1.0)
    for h in range(ATT_H):
        vt_ref[h] = v_ref[:, h * 128:(h + 1) * 128].astype(F32).T.astype(BF16)


def _attnprep(main, cos_t, sin_t, gq, gk, gmat, tm):
    bsz, t, _ = main.shape
    nb = t // tm
    return pl.pallas_call(
        functools.partial(_attnprep_kernel, tm=tm),
        grid=(bsz, nb),
        in_specs=[pl.BlockSpec((None, tm, 512), lambda b, i: (b, i, 0)),
                  pl.BlockSpec((None, tm, 512), lambda b, i: (b, i, 1)),
                  pl.BlockSpec((None, tm, 512), lambda b, i: (b, i, 2)),
                  pl.BlockSpec((tm, 128), lambda b, i: (i, 0)),
                  pl.BlockSpec((tm, 128), lambda b, i: (i, 0)),
                  pl.BlockSpec((1, 512), lambda b, i: (0, 0)),
                  pl.BlockSpec((1, 512), lambda b, i: (0, 0)),
                  pl.BlockSpec((512, 512), lambda b, i: (0, 0))],
        out_specs=[pl.BlockSpec((None, tm, 512), lambda b, i: (b, i, 0)),
                   pl.BlockSpec((None, tm, 512), lambda b, i: (b, i, 0)),
                   pl.BlockSpec((None, ATT_H, None, 128, tm), lambda b, i: (b, 0, i, 0, 0))],
        out_shape=[jax.ShapeDtypeStruct((bsz, t, 512), BF16),
                   jax.ShapeDtypeStruct((bsz, t, 512), BF16),
                   jax.ShapeDtypeStruct((bsz, ATT_H, nb, 128, tm), BF16)],
        compiler_params=pltpu.CompilerParams(
            dimension_semantics=("arbitrary", "arbitrary"), vmem_limit_bytes=VMEM_LIMIT),
        name="attnprep",
    )(main, main, main, cos_t, sin_t, gq, gk, gmat)


def _attn_kernel(q_ref, k_ref, vt_ref, gate_ref, lam_ref, subln_ref, o_ref,
                 q2_ref, m_ref, l_ref, acc_ref, sa_ref, sb_ref, *, tq, tk, nkv, lam_init):
    qi = pl.program_id(2)
    q = q_ref[...]
    lane = lax.broadcasted_iota(jnp.int32, (tq, 128), 1)
    zero = jnp.zeros_like(q)
    q2_ref[0:tq, :] = jnp.where(lane < QK, q, zero)
    q2_ref[tq:2 * tq, :] = jnp.where(lane >= QK, q, zero)
    m_ref[...] = jnp.full((1, 2 * tq), -jnp.inf, F32)
    l_ref[...] = jnp.zeros((1, 2 * tq), F32)
    acc_ref[...] = jnp.zeros((128, 2 * tq), F32)

    def scores(j, dst_ref):
        dst_ref[...] = _dot_nt(k_ref[pl.ds(j * tk, tk), :], q2_ref[...])

    def consume(j, src_ref, ctx_only):
        st = src_ref[...]
        if ctx_only:
            krow = j * tk + lax.broadcasted_iota(jnp.int32, (tk, 1), 0)
            st = jnp.where(krow < N_CTX, st, -jnp.inf)
        m_old = m_ref[...]
        m_new = jnp.maximum(m_old, jnp.max(st, axis=0, keepdims=True))
        alpha = jnp.exp2(m_old - m_new)
        p = jnp.exp2(st - m_new)
        l_ref[...] = alpha * l_ref[...] + jnp.sum(p, axis=0, keepdims=True)
        acc_ref[...] = alpha * acc_ref[...] + _dot(vt_ref[j], p.astype(BF16))
        m_ref[...] = m_new

    @pl.when(qi == 0)
    def _():
        scores(0, sa_ref)
        consume(0, sa_ref, True)

    @pl.when(qi != 0)
    def _():
        bufs = (sa_ref, sb_ref)
        scores(0, sa_ref)
        for j in range(nkv):
            if j + 1 < nkv:
                scores(j + 1, bufs[(j + 1) % 2])
            consume(j, bufs[j % 2], False)

    lam_v = lam_ref[...]
    lam = (jnp.exp(jnp.sum(lam_v[0:1] * lam_v[1:2], axis=-1, keepdims=True))
           - jnp.exp(jnp.sum(lam_v[2:3] * lam_v[3:4], axis=-1, keepdims=True)) + lam_init)
    inv = 1.0 / l_ref[...]
    acc = acc_ref[...]
    o = acc[:, 0:tq] * inv[:, 0:tq] - lam * (acc[:, tq:2 * tq] * inv[:, tq:2 * tq])
    o = o * lax.rsqrt(jnp.mean(o * o, axis=0, keepdims=True) + EPS)
    o = o * subln_ref[...] * (1.0 - lam_init)
    o_ref[...] = (o.T * _silu(gate_ref[...].astype(F32))).astype(BF16)


def _attn(qp, kp, vt, main, lam_vec, subln, lam_init, tq, tk):
    bsz, t, _ = qp.shape
    nkv = t // tk
    assert tq == N_CTX and tk >= N_CTX
    return pl.pallas_call(
        functools.partial(_attn_kernel, tq=tq, tk=tk, nkv=nkv, lam_init=lam_init),
        grid=(bsz, ATT_H, t // tq),
        in_specs=[pl.BlockSpec((None, tq, 128), lambda b, h, i: (b, i, h)),
                  pl.BlockSpec((None, t, 128), lambda b, h, i: (b, 0, h)),
                  pl.BlockSpec((None, None, nkv, 128, tk), lambda b, h, i: (b, h, 0, 0, 0)),
                  pl.BlockSpec((None, tq, 128), lambda b, h, i: (b, i, 12 + h)),
                  pl.BlockSpec((4, QK), lambda b, h, i: (0, 0)),
                  pl.BlockSpec((128, 1), lambda b, h, i: (0, 0))],
        out_specs=pl.BlockSpec((None, tq, 128), lambda b, h, i: (b, i, h)),
        out_shape=jax.ShapeDtypeStruct((bsz, t, 512), BF16),
        scratch_shapes=[pltpu.VMEM((2 * tq, 128), BF16),
                        pltpu.VMEM((1, 2 * tq), F32),
                        pltpu.VMEM((1, 2 * tq), F32),
                        pltpu.VMEM((128, 2 * tq), F32),
                        pltpu.VMEM((tk, 2 * tq), F32),
                        pltpu.VMEM((tk, 2 * tq), F32)],
        compiler_params=pltpu.CompilerParams(
            dimension_semantics=("arbitrary", "arbitrary", "arbitrary"),
            vmem_limit_bytes=VMEM_LIMIT),
        name="attn",
    )(qp, kp, vt, main, lam_vec, subln)


def _conv_kernel(x_ref, prev_ref, next_ref, w_ref, b_ref, o_ref, *, tm, t):
    i = pl.program_id(1)
    x = x_ref[...].astype(F32)
    loc = lax.broadcasted_iota(jnp.int32, (tm, 1), 0)
    row = i * tm + loc
    xm1 = pltpu.roll(x, 1, 0)
    xm1 = jnp.where(loc == 0, prev_ref[7:8, :].astype(F32), xm1)
    xm1 = jnp.where((row == 0) | (row == N_CTX), 0.0, xm1)
    xp1 = pltpu.roll(x, tm - 1, 0)
    xp1 = jnp.where(loc == tm - 1, next_ref[0:1, :].astype(F32), xp1)
    xp1 = jnp.where((row == t - 1) | (row == N_CTX - 1), 0.0, xp1)
    w = w_ref[...]
    y = w[0:1] * xm1 + w[1:2] * x + w[2:3] * xp1 + b_ref[...]
    o_ref[...] = _silu(y).astype(BF16)


def _conv(main, w, b, tm):
    bsz, t, _ = main.shape
    r8 = tm // 8
    last8 = t // 8 - 1
    return pl.pallas_call(
        functools.partial(_conv_kernel, tm=tm, t=t),
        grid=(bsz, t // tm, 3),
        in_specs=[pl.BlockSpec((None, tm, 512), lambda b, i, j: (b, i, 6 + j)),
                  pl.BlockSpec((None, 8, 512), lambda b, i, j: (b, jnp.maximum(i * r8 - 1, 0), 6 + j)),
                  pl.BlockSpec((None, 8, 512), lambda b, i, j: (b, jnp.minimum((i + 1) * r8, last8), 6 + j)),
                  pl.BlockSpec((3, 512), lambda b, i, j: (0, j)),
                  pl.BlockSpec((1, 512), lambda b, i, j: (0, j))],
        out_specs=pl.BlockSpec((None, tm, 512), lambda b, i, j: (b, i, j)),
        out_shape=jax.ShapeDtypeStruct((bsz, t, 1536), BF16),
        compiler_params=pltpu.CompilerParams(
            dimension_semantics=("arbitrary", "arbitrary", "arbitrary"),
            vmem_limit_bytes=VMEM_LIMIT),
        name="conv",
    )(main, main, main, w, b)


def _chunk_index(s, nchunks, backward):
    if not backward:
        return s
    nctx = N_CTX // CHUNK
    return jnp.where(s < nctx, nctx - 1 - s, nchunks - 1 + nctx - s)


def _tri_mask(backward):
    r = lax.broadcasted_iota(jnp.int32, (CHUNK, CHUNK), 0)
    c = lax.broadcasted_iota(jnp.int32, (CHUNK, CHUNK), 1)
    return (c >= r) if backward else (c <= r)


def _ssd_kernel(*refs, backward):
    if backward:
        (u_ref, dtr_ref, bias_ref, alog_ref, z_ref, yf_ref, dskip_ref, gain_ref,
         o_ref, s_ref) = refs
    else:
        u_ref, dtr_ref, bias_ref, alog_ref, o_ref, s_ref = refs
    step = pl.program_id(1)

    @pl.when(step == 0)
    def _():
        s_ref[...] = jnp.zeros_like(s_ref)

    mask = _tri_mask(backward)
    tri = jnp.where(mask, 1.0, 0.0).astype(BF16)
    pre = dtr_ref[...] + bias_ref[...]
    dt = jnp.maximum(pre, 0.0) + jnp.log1p(jnp.exp(-jnp.abs(pre)))
    la = dt * (-jnp.exp(alog_ref[...]))
    cum = _tri_dot(tri, la)
    cum_t = cum.T
    edge = 0 if backward else CHUNK - 1
    tot = cum[edge:edge + 1, :]
    ecum = jnp.exp(cum)
    first = lax.broadcasted_iota(jnp.int32, (CHUNK, 128), 1) < SSD_P
    first_row = lax.broadcasted_iota(jnp.int32, (1, 128), 1) < SSD_P
    col0 = SSD_H if backward else 0

    def pair_cols(v, c):
        rows = v.shape[0]
        a = jnp.broadcast_to(v[:, c:c + 1], (rows, 128))
        b = jnp.broadcast_to(v[:, c + 1:c + 2], (rows, 128))
        return jnp.where(first if rows == CHUNK else first_row, a, b)

    for g in range(SSD_G):
        k = u_ref[:, 1024 + g * SSD_N:1024 + (g + 1) * SSD_N]
        q = u_ref[:, 1280 + g * SSD_N:1280 + (g + 1) * SSD_N]
        scores = _dot_nt(q, k)
        k_t = k.astype(F32).T.astype(BF16)
        y_inter = _dot(q, s_ref[g].astype(BF16))
        for pp in range(SSD_HPG // 2):
            h0 = g * SSD_HPG + 2 * pp
            c0 = col0 + h0
            off = h0 * SSD_P
            xs = u_ref[:, off:off + 128].astype(F32)
            v = xs * pair_cols(dt, c0)
            vb = v.astype(BF16)
            zero = jnp.zeros_like(vb)
            y = pair_cols(ecum, c0) * y_inter[:, pp * 128:(pp + 1) * 128]
            for hh in range(2):
                c = c0 + hh
                seg = jnp.broadcast_to(cum[:, c:c + 1], (CHUNK, CHUNK)) - cum_t[c:c + 1, :]
                dec = jnp.exp(jnp.where(mask, seg, -jnp.inf))
                vh = jnp.where(first if hh == 0 else jnp.logical_not(first), vb, zero)
                y = y + _dot((scores * dec).astype(BF16), vh)
            w = jnp.exp(pair_cols(tot, c0) - pair_cols(cum, c0))
            s_old = s_ref[g, :, pp * 128:(pp + 1) * 128]
            s_ref[g, :, pp * 128:(pp + 1) * 128] = (
                jnp.exp(pair_cols(tot, c0)) * s_old + _dot(k_t, (v * w).astype(BF16)))
            if backward:
                y = y + yf_ref[:, off:off + 128] + xs * dskip_ref[:, off:off + 128]
                y = y * _silu(z_ref[:, off:off + 128].astype(F32))
            o_ref[:, off:off + 128] = y
        if backward:
            lo, hi = g * 512, (g + 1) * 512
            yg = o_ref[:, lo:hi]
            yg = yg * lax.rsqrt(jnp.mean(yg * yg, axis=-1, keepdims=True) + EPS)
            o_ref[:, lo:hi] = yg * gain_ref[:, lo:hi]


def _ssd(u, dtr, bias, alog, backward, z_main=None, yf=None, dskip=None, gain=None):
    bsz, t, _ = u.shape
    nchunks = t // CHUNK
    idx = functools.partial(_chunk_index, nchunks=nchunks, backward=backward)
    row = lambda b, s: (b, idx(s), 0)
    const = lambda b, s: (0, 0)
    in_specs = [pl.BlockSpec((None, CHUNK, 1536), row),
                pl.BlockSpec((None, CHUNK, 128), row),
                pl.BlockSpec((1, 128), const),
                pl.BlockSpec((1, 128), const)]
    args = [u, dtr, bias, alog]
    if backward:
        in_specs += [pl.BlockSpec((None, CHUNK, 1024), lambda b, s: (b, idx(s), 2)),
                     pl.BlockSpec((None, CHUNK, 1024), row),
                     pl.BlockSpec((1, 1024), const),
                     pl.BlockSpec((1, 1024), const)]
        args += [z_main, yf, dskip, gain]
    return pl.pallas_call(
        functools.partial(_ssd_kernel, backward=backward),
        grid=(bsz, nchunks),
        in_specs=in_specs,
        out_specs=pl.BlockSpec((None, CHUNK, 1024), row),
        out_shape=jax.ShapeDtypeStruct((bsz, t, 1024), F32),
        scratch_shapes=[pltpu.VMEM((SSD_G, SSD_N, 512), F32)],
        compiler_params=pltpu.CompilerParams(
            dimension_semantics=("arbitrary", "arbitrary"), vmem_limit_bytes=VMEM_LIMIT),
        name="ssd_bwd" if backward else "ssd_fwd",
    )(*args)


def _ret_tables(backward):
    exps = RET_EXP_B if backward else RET_EXP_F
    lg = np.log1p(-np.exp2(-np.asarray(exps, np.float64)))
    i = np.arange(CHUNK, dtype=np.float64)
    if backward:
        cum = (CHUNK - i)[None, :] * lg[:, None]
        tot = cum[:, 0]
        msk = i[None, :] >= i[:, None]
    else:
        cum = (i + 1.0)[None, :] * lg[:, None]
        tot = cum[:, -1]
        msk = i[None, :] <= i[:, None]
    dec = np.where(msk[None], np.exp(cum[:, :, None] - cum[:, None, :]), 0.0)
    inter = np.broadcast_to(np.exp(cum)[:, :, None], (RET_H, CHUNK, 128))
    toend = np.broadcast_to(np.exp(tot[:, None] - cum)[:, :, None], (RET_H, CHUNK, 128))
    tables = np.stack([dec, inter, toend], axis=1).astype(np.float32)
    return tables, [float(np.exp(v)) for v in tot]


def _ret_kernel(*refs, backward, etot):
    if backward:
        qk_ref, v_ref, cos_ref, sin_ref, tab_ref, g_ref, yf_ref, gain_ref, o_ref, s_ref = refs
    else:
        qk_ref, v_ref, cos_ref, sin_ref, tab_ref, o_ref, s_ref = refs
    step = pl.program_id(1)

    @pl.when(step == 0)
    def _():
        s_ref[...] = jnp.zeros_like(s_ref)

    cos = cos_ref[...]
    sin = sin_ref[...]
    lane = lax.broadcasted_iota(jnp.int32, (CHUNK, 128), 1)
    lo_half = (lane % QK) < (QK // 2)
    first = lane < QK
    srow = lax.broadcasted_iota(jnp.int32, (128, 128), 0) < QK
    for pp in range(RET_H // 2):
        q = _rope128(qk_ref[:, pp * 128:(pp + 1) * 128].astype(F32), cos, sin, lo_half)
        k = _rope128(qk_ref[:, 256 + pp * 128:256 + (pp + 1) * 128].astype(F32) * (QK ** -0.5),
                     cos, sin, lo_half)
        qb = q.astype(BF16)
        kb = k.astype(BF16)
        k_t = kb.astype(F32).T.astype(BF16)
        zero = jnp.zeros_like(qb)
        s_pair = s_ref[pp]
        s_bf = s_pair.astype(BF16)
        s_new = jnp.where(srow, etot[2 * pp], etot[2 * pp + 1]) * s_pair
        for hh in range(2):
            h = 2 * pp + hh
            sel = first if hh == 0 else jnp.logical_not(first)
            qh = jnp.where(sel, qb, zero)
            scores = _dot_nt(qh, kb)
            vh = v_ref[:, h * 128:(h + 1) * 128]
            y = _dot((scores * tab_ref[h, 0]).astype(BF16), vh) + tab_ref[h, 1] * _dot(qh, s_bf)
            upd = _dot(k_t, (vh.astype(F32) * tab_ref[h, 2]).astype(BF16))
            s_new = s_new + jnp.where(srow if hh == 0 else jnp.logical_not(srow), upd, 0.0)
            if backward:
                y = y + yf_ref[:, h * 128:(h + 1) * 128]
                y = y * lax.rsqrt(jnp.mean(y * y, axis=-1, keepdims=True) + EPS) * gain_ref[...]
                y = y * _silu(g_ref[:, h * 128:(h + 1) * 128].astype(F32))
            o_ref[:, h * 128:(h + 1) * 128] = y.astype(o_ref.dtype)
        s_ref[pp] = s_new


def _ret(main, cos_t, sin_t, backward, yf=None, gain=None):
    bsz, t, _ = main.shape
    nchunks = t // CHUNK
    tables, etot = _ret_tables(backward)
    idx = functools.partial(_chunk_index, nchunks=nchunks, backward=backward)
    row = lambda b, s: (b, idx(s), 0)
    in_specs = [pl.BlockSpec((None, CHUNK, 512), lambda b, s: (b, idx(s), 9)),
                pl.BlockSpec((None, CHUNK, 512), lambda b, s: (b, idx(s), 10)),
                pl.BlockSpec((CHUNK, 128), lambda b, s: (idx(s), 0)),
                pl.BlockSpec((CHUNK, 128), lambda b, s: (idx(s), 0)),
                pl.BlockSpec((RET_H, 3, CHUNK, 128), lambda b, s: (0, 0, 0, 0))]
    args = [main, main, cos_t, sin_t, jnp.asarray(tables)]
    if backward:
        in_specs += [pl.BlockSpec((None, CHUNK, 512), lambda b, s: (b, idx(s), 11)),
                     pl.BlockSpec((None, CHUNK, 512), row),
                     pl.BlockSpec((1, 128), lambda b, s: (0, 0))]
        args += [main, yf, gain]
    return pl.pallas_call(
        functools.partial(_ret_kernel, backward=backward, etot=etot),
        grid=(bsz, nchunks),
        in_specs=in_specs,
        out_specs=pl.BlockSpec((None, CHUNK, 512), row),
        out_shape=jax.ShapeDtypeStruct((bsz, t, 512), BF16 if backward else F32),
        scratch_shapes=[pltpu.VMEM((RET_H // 2, 128, 128), F32)],
        compiler_params=pltpu.CompilerParams(
            dimension_semantics=("arbitrary", "arbitrary"), vmem_limit_bytes=VMEM_LIMIT),
        name="ret_bwd" if backward else "ret_fwd",
    )(*args)


def _outproj_kernel(x_ref, a_ref, s_ref, r_ref, ada_ref, w_ref, o_ref, *, tm, blk0, ctx_row):
    b = pl.program_id(0)
    i = pl.program_id(1) + blk0
    acc = _dot(a_ref[...], w_ref[0:512, :])
    acc += _dot(s_ref[...].astype(BF16), w_ref[512:1536, :])
    acc += _dot(r_ref[...], w_ref[1536:2048, :])
    rows = i * tm + lax.broadcasted_iota(jnp.int32, (tm, 1), 0)
    gate = _row_mod(ada_ref, b, rows, 2 * D, 3 * D, ctx_row)
    o_ref[...] = x_ref[...] + gate * acc


def _outproj(xcat, attn_o, ssd_o, ret_o, ada, w_out, ctx_row, latent_only):
    bsz, t, _ = xcat.shape
    tm = N_CTX
    blk0 = 1 if latent_only else 0
    nb = t // tm - blk0
    row = lambda b, i: (b, i + blk0, 0)
    return pl.pallas_call(
        functools.partial(_outproj_kernel, tm=tm, blk0=blk0, ctx_row=ctx_row),
        grid=(bsz, nb),
        in_specs=[pl.BlockSpec((None, tm, D), row),
                  pl.BlockSpec((None, tm, 512), row),
                  pl.BlockSpec((None, tm, 1024), row),
                  pl.BlockSpec((None, tm, 512), row),
                  pl.BlockSpec((8, 3 * D), lambda b, i: (0, 0)),
                  pl.BlockSpec((2 * D, D), lambda b, i: (0, 0))],
        out_specs=pl.BlockSpec((None, tm, D), lambda b, i: (b, i, 0)),
        out_shape=jax.ShapeDtypeStruct((bsz, nb * tm, D), F32),
        compiler_params=pltpu.CompilerParams(
            dimension_semantics=("arbitrary", "arbitrary"), vmem_limit_bytes=VMEM_LIMIT),
        name="outproj",
    )(xcat, attn_o, ssd_o, ret_o, ada, w_out)


def _rope_tables(seq):
    n_rows = seq // GRID_W
    row = np.repeat(np.arange(n_rows, dtype=np.float32), GRID_W)
    col = np.tile(np.arange(GRID_W, dtype=np.float32), n_rows)
    n_freq = QK // 4
    inv_freq = (np.float32(ROPE_BASE) ** (-np.arange(n_freq, dtype=np.float32) / n_freq)).astype(np.float32)
    ang = np.concatenate([row[:, None] * inv_freq, col[:, None] * inv_freq], axis=-1).astype(np.float32)
    cos = np.concatenate([np.ones((N_CTX, QK // 2)), np.cos(ang.astype(np.float64))], axis=0)
    sin = np.concatenate([np.zeros((N_CTX, QK // 2)), np.sin(ang.astype(np.float64))], axis=0)
    cos_t = np.tile(cos, (1, 4)).astype(np.float32)
    sin_t = np.tile(np.concatenate([-sin, sin], axis=1), (1, 2)).astype(np.float32)
    return jnp.asarray(cos_t), jnp.asarray(sin_t)


_MAIN_COLS = ((0, 2048), (3616, 4640), (2048, 3584), (4640, 6176))
_DT_COLS = (3584, 3616)


def _pad_lanes(v, n=128):
    v = v.reshape(1, -1)
    return jnp.pad(v, ((0, 0), (0, n - v.shape[1])))


def kernel(x, c, ctx, c_ctx, w_ada, b_ada, w_in, w_out, attn_q_norm, attn_k_norm,
           lambda_q1, lambda_k1, lambda_q2, lambda_k2, attn_subln,
           ssd_conv_w, ssd_conv_b, ssd_dt_bias, ssd_a_log, ssd_d, ssd_norm, ret_norm):
    bsz, seq, _ = x.shape
    depth = w_ada.shape[0]
    assert bsz + 1 <= 8 and ctx.shape[1] == N_CTX
    ctx_row = bsz
    xcat = jnp.concatenate([ctx, x], axis=1)
    cvec = jnp.zeros((8, D), F32).at[0:bsz].set(c).at[ctx_row].set(c_ctx)
    cos_t, sin_t = _rope_tables(seq)
    gsel = np.arange(512) // QK
    gmat = jnp.asarray((gsel[:, None] == gsel[None, :]).astype(np.float32), BF16)

    for layer in range(depth):
        last = layer == depth - 1
        lam_init = 0.8 - 0.6 * math.exp(-0.3 * layer)
        w_l = w_in[layer]
        w_main = jnp.concatenate([w_l[:, a:b] for a, b in _MAIN_COLS], axis=1).astype(BF16)
        w_dt = jnp.pad(w_l[:, _DT_COLS[0]:_DT_COLS[1]], ((0, 0), (0, 96))).astype(BF16)

        ada = _ada(cvec, w_ada[layer], b_ada[layer].reshape(1, -1))
        main, dtr = _inproj(xcat, ada, w_main, w_dt, ctx_row)

        qp, kp, vt = _attnprep(main, cos_t, sin_t,
                               jnp.tile(attn_q_norm[layer], 8).reshape(1, 512),
                               jnp.tile(attn_k_norm[layer], 8).reshape(1, 512), gmat, tm=768)
        lam_vec = jnp.stack([lambda_q1[layer], lambda_k1[layer], lambda_q2[layer], lambda_k2[layer]])
        attn_o = _attn(qp, kp, vt, main, lam_vec, attn_subln[layer].reshape(128, 1), lam_init,
                       tq=N_CTX, tk=768)

        u = _conv(main, ssd_conv_w[layer], ssd_conv_b[layer].reshape(1, -1), tm=768)
        bias = _pad_lanes(ssd_dt_bias[layer])
        alog = _pad_lanes(ssd_a_log[layer])
        ssd_f = _ssd(u, dtr, bias, alog, backward=False)
        ssd_o = _ssd(u, dtr, bias, alog, backward=True, z_main=main, yf=ssd_f,
                     dskip=jnp.repeat(ssd_d[layer], SSD_P).reshape(1, 1024),
                     gain=ssd_norm[layer].reshape(1, 1024))

        ret_f = _ret(main, cos_t, sin_t, backward=False)
        ret_o = _ret(main, cos_t, sin_t, backward=True, yf=ret_f,
                     gain=ret_norm[layer].reshape(1, 128))

        xcat = _outproj(xcat, attn_o, ssd_o, ret_o, ada, w_out[layer].astype(BF16), ctx_row,
                        latent_only=last)
    return xcat
```

```python
import functools
import math

import numpy as np
import jax
import jax.numpy as jnp
from jax import lax
from jax.experimental import pallas as pl
from jax.experimental.pallas import tpu as pltpu

F32 = jnp.float32
BF16 = jnp.bfloat16

D = 1024
N_CTX = 256
GRID_W = 64
EPS = 1e-6
ROPE_BASE = 10000.0
CHUNK = 128
QK = 64
ATT_H = 4
SSD_H = 16
SSD_G = 2
SSD_HPG = 8
SSD_P = 64
SSD_N = 128
RET_H = 4
RET_EXP_F = (5.0, 6.0, 7.0, 8.0)
RET_EXP_B = (5.5, 6.5, 7.5, 8.5)
N_MAIN = 6144
LOG2E = math.log2(math.e)
VMEM_LIMIT = 56 * 1024 * 1024
ATT_TK = 768


def _silu(v):
    return v * (1.0 / (1.0 + jnp.exp(-v)))


def _split3(v):
    hi = v.astype(BF16)
    r1 = v - hi.astype(F32)
    mid = r1.astype(BF16)
    lo = (r1 - mid.astype(F32)).astype(BF16)
    return hi, mid, lo


def _dot(a, b):
    return jnp.dot(a, b, preferred_element_type=F32)


def _dot_nt(a, b):
    return lax.dot_general(a, b, (((1,), (1,)), ((), ())), preferred_element_type=F32)


def _tri_dot(tri, v):
    hi, mid, lo = _split3(v)
    return _dot(tri, hi) + _dot(tri, mid) + _dot(tri, lo)


def _ada_kernel(c_ref, w_ref, b_ref, o_ref):
    o_ref[...] = _dot(_silu(c_ref[...]), w_ref[...]) + b_ref[...]


def _ada(cvec, w, b):
    n = w.shape[1]
    tn = 768
    return pl.pallas_call(
        _ada_kernel,
        grid=(n // tn,),
        in_specs=[pl.BlockSpec((8, D), lambda j: (0, 0)),
                  pl.BlockSpec((D, tn), lambda j: (0, j)),
                  pl.BlockSpec((1, tn), lambda j: (0, j))],
        out_specs=pl.BlockSpec((8, tn), lambda j: (0, j)),
        out_shape=jax.ShapeDtypeStruct((8, n), F32),
        name="ada",
    )(cvec, w, b)


def _row_mod(ada_ref, b, rows, lo, hi, ctx_row):
    vb = ada_ref[pl.ds(b, 1), lo:hi]
    vc = ada_ref[ctx_row:ctx_row + 1, lo:hi]
    return jnp.where(rows < N_CTX, vc, vb)


def _inproj_kernel(x_ref, ada_ref, w_ref, wdt_ref, main_ref, dt_ref, *, tm, tn, ctx_row):
    b = pl.program_id(0)
    i = pl.program_id(1)
    x = x_ref[...]
    xn = x * lax.rsqrt(jnp.mean(x * x, axis=-1, keepdims=True) + EPS)
    rows = i * tm + lax.broadcasted_iota(jnp.int32, (tm, 1), 0)
    shift = _row_mod(ada_ref, b, rows, 0, D, ctx_row)
    scale = _row_mod(ada_ref, b, rows, D, 2 * D, ctx_row)
    h = (xn * (1.0 + scale) + shift).astype(BF16)
    for c in range(N_MAIN // tn):
        main_ref[:, c * tn:(c + 1) * tn] = _dot(h, w_ref[:, c * tn:(c + 1) * tn]).astype(BF16)
    dt_ref[...] = _dot(h, wdt_ref[...])


def _inproj(xcat, ada, w_main, w_dt, ctx_row):
    bsz, t, _ = xcat.shape
    tm, tn = 384, 768
    return pl.pallas_call(
        functools.partial(_inproj_kernel, tm=tm, tn=tn, ctx_row=ctx_row),
        grid=(bsz, t // tm),
        in_specs=[pl.BlockSpec((None, tm, D), lambda b, i: (b, i, 0)),
                  pl.BlockSpec((8, 3 * D), lambda b, i: (0, 0)),
                  pl.BlockSpec((D, N_MAIN), lambda b, i: (0, 0)),
                  pl.BlockSpec((D, 128), lambda b, i: (0, 0))],
        out_specs=[pl.BlockSpec((None, tm, N_MAIN), lambda b, i: (b, i, 0)),
                   pl.BlockSpec((None, tm, 128), lambda b, i: (b, i, 0))],
        out_shape=[jax.ShapeDtypeStruct((bsz, t, N_MAIN), BF16),
                   jax.ShapeDtypeStruct((bsz, t, 128), F32)],
        compiler_params=pltpu.CompilerParams(
            dimension_semantics=("arbitrary", "arbitrary"), vmem_limit_bytes=VMEM_LIMIT),
        name="inproj",
    )(xcat, ada, w_main, w_dt)


def _rope128(x, cos, sin, lo_half):
    partner = jnp.where(lo_half, pltpu.roll(x, 96, 1), pltpu.roll(x, 32, 1))
    return x * cos + partner * sin


def _attnprep_kernel(q_ref, k_ref, v_ref, cos_ref, sin_ref, gq_ref, gk_ref, gmat_ref,
                     qo_ref, ko_ref, vt_ref, *, tm):
    cos = cos_ref[...]
    sin = sin_ref[...]
    lane = lax.broadcasted_iota(jnp.int32, (tm, 128), 1)
    lo_half = (lane % QK) < (QK // 2)
    gmat = gmat_ref[...]

    def norm_rope(src_ref, gain_ref, out_ref, out_scale):
        x = src_ref[...].astype(F32)
        ss = _dot((x * x).astype(BF16), gmat)
        y = x * lax.rsqrt(ss * (1.0 / QK) + EPS) * gain_ref[...]
        for g in range(4):
            yg = _rope128(y[:, g * 128:(g + 1) * 128], cos, sin, lo_half)
            out_ref[:, g * 128:(g + 1) * 128] = (yg * out_scale).astype(BF16)

    norm_rope(q_ref, gq_ref, qo_ref, (QK ** -0.5) * LOG2E)
    norm_rope(k_ref, gk_ref, ko_ref, 1.0)
    for h in range(ATT_H):
        vt_ref[h] = v_ref[:, h * 128:(h + 1) * 128].astype(F32).T.astype(BF16)


def _attnprep(main, cos_t, sin_t, gq, gk, gmat, tm):
    bsz, t, _ = main.shape
    nb = t // tm
    return pl.pallas_call(
        functools.partial(_attnprep_kernel, tm=tm),
        grid=(bsz, nb),
        in_specs=[pl.BlockSpec((None, tm, 512), lambda b, i: (b, i, 0)),
                  pl.BlockSpec((None, tm, 512), lambda b, i: (b, i, 1)),
                  pl.BlockSpec((None, tm, 512), lambda b, i: (b, i, 2)),
                  pl.BlockSpec((tm, 128), lambda b, i: (i, 0)),
                  pl.BlockSpec((tm, 128), lambda b, i: (i, 0)),
                  pl.BlockSpec((1, 512), lambda b, i: (0, 0)),
                  pl.BlockSpec((1, 512), lambda b, i: (0, 0)),
                  pl.BlockSpec((512, 512), lambda b, i: (0, 0))],
        out_specs=[pl.BlockSpec((None, tm, 512), lambda b, i: (b, i, 0)),
                   pl.BlockSpec((None, tm, 512), lambda b, i: (b, i, 0)),
                   pl.BlockSpec((None, ATT_H, None, 128, tm), lambda b, i: (b, 0, i, 0, 0))],
        out_shape=[jax.ShapeDtypeStruct((bsz, t, 512), BF16),
                   jax.ShapeDtypeStruct((bsz, t, 512), BF16),
                   jax.ShapeDtypeStruct((bsz, ATT_H, nb, 128, tm), BF16)],
        compiler_params=pltpu.CompilerParams(
            dimension_semantics=("arbitrary", "arbitrary"), vmem_limit_bytes=VMEM_LIMIT),
        name="attnprep",
    )(main, main, main, cos_t, sin_t, gq, gk, gmat)


def _attn_kernel(q_ref, qn_ref, k_ref, vt_ref, gate_ref, lam_ref, subln_ref, o_ref,
                 q2_ref, m_ref, l_ref, acc_ref, s_ref, mt_ref, *, tq, tk, nkv, lam_init):
    qi = pl.program_id(2)
    lane = lax.broadcasted_iota(jnp.int32, (tq, 128), 1)
    m_ref[...] = jnp.full((1, 2 * tq), -jnp.inf, F32)
    l_ref[...] = jnp.zeros((1, 2 * tq), F32)
    acc_ref[...] = jnp.zeros((128, 2 * tq), F32)

    def stack_maps(src_ref):
        q = src_ref[...]
        zero = jnp.zeros_like(q)
        q2_ref[0:tq, :] = jnp.where(lane < QK, q, zero)
        q2_ref[tq:2 * tq, :] = jnp.where(lane >= QK, q, zero)

    def hand_off():
        stack_maps(qn_ref)
        scores(0, 2, False)

    def scores(j, slot, ctx_only):
        st = _dot_nt(k_ref[pl.ds(j * tk, tk), :], q2_ref[...])
        if ctx_only:
            krow = j * tk + lax.broadcasted_iota(jnp.int32, (tk, 1), 0)
            st = jnp.where(krow < N_CTX, st, -jnp.inf)
        s_ref[slot] = st
        mt_ref[slot] = jnp.max(st, axis=0, keepdims=True)

    def consume(j, slot):
        m_old = m_ref[...]
        m_new = jnp.maximum(m_old, mt_ref[slot])
        alpha = jnp.exp2(m_old - m_new)
        p = jnp.exp2(s_ref[slot] - m_new)
        l_ref[...] = alpha * l_ref[...] + jnp.sum(p, axis=0, keepdims=True)
        acc_ref[...] = alpha * acc_ref[...] + _dot(vt_ref[j], p.astype(BF16))
        m_ref[...] = m_new

    @pl.when(qi == 0)
    def _():
        stack_maps(q_ref)
        scores(0, 0, True)
        hand_off()
        consume(0, 0)

    @pl.when(qi != 0)
    def _():
        for j in range(nkv):
            if j + 1 < nkv:
                scores(j + 1, (j + 1) % 2, False)
            else:
                hand_off()
            consume(j, 2 if j == 0 else j % 2)

    lam_v = lam_ref[...]
    lam = (jnp.exp(jnp.sum(lam_v[0:1] * lam_v[1:2], axis=-1, keepdims=True))
           - jnp.exp(jnp.sum(lam_v[2:3] * lam_v[3:4], axis=-1, keepdims=True)) + lam_init)
    inv = 1.0 / l_ref[...]
    acc = acc_ref[...]
    o = acc[:, 0:tq] * inv[:, 0:tq] - lam * (acc[:, tq:2 * tq] * inv[:, tq:2 * tq])
    o = o * lax.rsqrt(jnp.mean(o * o, axis=0, keepdims=True) + EPS)
    o = o * subln_ref[...] * (1.0 - lam_init)
    o_ref[...] = (o.T * _silu(gate_ref[...].astype(F32))).astype(BF16)


def _attn(qp, kp, vt, main, lam_vec, subln, lam_init, tq, tk):
    bsz, t, _ = qp.shape
    nkv = t // tk
    assert tq == N_CTX and tk >= N_CTX and nkv >= 2
    nq = t // tq
    return pl.pallas_call(
        functools.partial(_attn_kernel, tq=tq, tk=tk, nkv=nkv, lam_init=lam_init),
        grid=(bsz, ATT_H, nq),
        in_specs=[pl.BlockSpec((None, tq, 128), lambda b, h, i: (b, i, h)),
                  pl.BlockSpec((None, tq, 128), lambda b, h, i: (b, jnp.minimum(i + 1, nq - 1), h)),
                  pl.BlockSpec((None, t, 128), lambda b, h, i: (b, 0, h)),
                  pl.BlockSpec((None, None, nkv, 128, tk), lambda b, h, i: (b, h, 0, 0, 0)),
                  pl.BlockSpec((None, tq, 128), lambda b, h, i: (b, i, 12 + h)),
                  pl.BlockSpec((4, QK), lambda b, h, i: (0, 0)),
                  pl.BlockSpec((128, 1), lambda b, h, i: (0, 0))],
        out_specs=pl.BlockSpec((None, tq, 128), lambda b, h, i: (b, i, h)),
        out_shape=jax.ShapeDtypeStruct((bsz, t, 512), BF16),
        scratch_shapes=[pltpu.VMEM((2 * tq, 128), BF16),
                        pltpu.VMEM((1, 2 * tq), F32),
                        pltpu.VMEM((1, 2 * tq), F32),
                        pltpu.VMEM((128, 2 * tq), F32),
                        pltpu.VMEM((3, tk, 2 * tq), F32),
                        pltpu.VMEM((3, 1, 2 * tq), F32)],
        compiler_params=pltpu.CompilerParams(
            dimension_semantics=("arbitrary", "arbitrary", "arbitrary"),
            vmem_limit_bytes=VMEM_LIMIT),
        name="attn",
    )(qp, qp, kp, vt, main, lam_vec, subln)


def _conv_kernel(x_ref, prev_ref, next_ref, w_ref, b_ref, o_ref, *, tm, t):
    i = pl.program_id(1)
    x = x_ref[...].astype(F32)
    loc = lax.broadcasted_iota(jnp.int32, (tm, 1), 0)
    row = i * tm + loc
    xm1 = pltpu.roll(x, 1, 0)
    xm1 = jnp.where(loc == 0, prev_ref[7:8, :].astype(F32), xm1)
    xm1 = jnp.where((row == 0) | (row == N_CTX), 0.0, xm1)
    xp1 = pltpu.roll(x, tm - 1, 0)
    xp1 = jnp.where(loc == tm - 1, next_ref[0:1, :].astype(F32), xp1)
    xp1 = jnp.where((row == t - 1) | (row == N_CTX - 1), 0.0, xp1)
    w = w_ref[...]
    y = w[0:1] * xm1 + w[1:2] * x + w[2:3] * xp1 + b_ref[...]
    o_ref[...] = _silu(y).astype(BF16)


def _conv(main, w, b, tm):
    bsz, t, _ = main.shape
    r8 = tm // 8
    last8 = t // 8 - 1
    return pl.pallas_call(
        functools.partial(_conv_kernel, tm=tm, t=t),
        grid=(bsz, t // tm, 3),
        in_specs=[pl.BlockSpec((None, tm, 512), lambda b, i, j: (b, i, 6 + j)),
                  pl.BlockSpec((None, 8, 512), lambda b, i, j: (b, jnp.maximum(i * r8 - 1, 0), 6 + j)),
                  pl.BlockSpec((None, 8, 512), lambda b, i, j: (b, jnp.minimum((i + 1) * r8, last8), 6 + j)),
                  pl.BlockSpec((3, 512), lambda b, i, j: (0, j)),
                  pl.BlockSpec((1, 512), lambda b, i, j: (0, j))],
        out_specs=pl.BlockSpec((None, tm, 512), lambda b, i, j: (b, i, j)),
        out_shape=jax.ShapeDtypeStruct((bsz, t, 1536), BF16),
        compiler_params=pltpu.CompilerParams(
            dimension_semantics=("arbitrary", "arbitrary", "arbitrary"),
            vmem_limit_bytes=VMEM_LIMIT),
        name="conv",
    )(main, main, main, w, b)


def _chunk_index(s, nchunks, backward):
    if not backward:
        return s
    nctx = N_CTX // CHUNK
    return jnp.where(s < nctx, nctx - 1 - s, nchunks - 1 + nctx - s)


def _tri_mask(backward):
    r = lax.broadcasted_iota(jnp.int32, (CHUNK, CHUNK), 0)
    c = lax.broadcasted_iota(jnp.int32, (CHUNK, CHUNK), 1)
    return (c >= r) if backward else (c <= r)


def _ssd_chunk(u_ref, dtr_ref, bias_ref, alog_ref, o_ref, s_ref, fin, backward):
    mask = _tri_mask(backward)
    tri = jnp.where(mask, 1.0, 0.0).astype(BF16)
    pre = dtr_ref[...] + bias_ref[...]
    dt = jnp.maximum(pre, 0.0) + jnp.log1p(jnp.exp(-jnp.abs(pre)))
    la = dt * (-jnp.exp(alog_ref[...]))
    cum = _tri_dot(tri, la)
    cum_t = cum.T
    edge = 0 if backward else CHUNK - 1
    tot = cum[edge:edge + 1, :]
    first = lax.broadcasted_iota(jnp.int32, (CHUNK, 128), 1) < SSD_P
    first_row = lax.broadcasted_iota(jnp.int32, (1, 128), 1) < SSD_P
    col0 = SSD_H if backward else 0

    def pair_cols(v, c):
        rows = v.shape[0]
        a = jnp.broadcast_to(v[:, c:c + 1], (rows, 128))
        b = jnp.broadcast_to(v[:, c + 1:c + 2], (rows, 128))
        return jnp.where(first if rows == CHUNK else first_row, a, b)

    for g in range(SSD_G):
        k = u_ref[:, 1024 + g * SSD_N:1024 + (g + 1) * SSD_N]
        q = u_ref[:, 1280 + g * SSD_N:1280 + (g + 1) * SSD_N]
        scores = _dot_nt(q, k)
        k_t = k.astype(F32).T.astype(BF16)
        y_inter = _dot(q, s_ref[g].astype(BF16))
        for pp in range(SSD_HPG // 2):
            h0 = g * SSD_HPG + 2 * pp
            c0 = col0 + h0
            off = h0 * SSD_P
            xs = u_ref[:, off:off + 128].astype(F32)
            v = xs * pair_cols(dt, c0)
            vb = v.astype(BF16)
            zero = jnp.zeros_like(vb)
            cum_p = pair_cols(cum, c0)
            tot_p = pair_cols(tot, c0)
            wts, vals = [], []
            for hh in range(2):
                c = c0 + hh
                seg = jnp.broadcast_to(cum[:, c:c + 1], (CHUNK, CHUNK)) - cum_t[c:c + 1, :]
                dec = jnp.exp(jnp.where(mask, seg, -jnp.inf))
                wts.append((scores * dec).astype(BF16))
                vals.append(jnp.where(first if hh == 0 else jnp.logical_not(first), vb, zero))
            y = (jnp.exp(cum_p) * y_inter[:, pp * 128:(pp + 1) * 128]
                 + _dot(jnp.concatenate(wts, axis=1), jnp.concatenate(vals, axis=0)))
            w = jnp.exp(tot_p - cum_p)
            s_old = s_ref[g, :, pp * 128:(pp + 1) * 128]
            s_ref[g, :, pp * 128:(pp + 1) * 128] = (
                jnp.exp(tot_p) * s_old + _dot(k_t, (v * w).astype(BF16)))
            if backward:
                z_ref, yf_ref, dskip_ref, _ = fin
                y = y + yf_ref[:, off:off + 128] + xs * dskip_ref[:, off:off + 128]
                y = y * _silu(z_ref[:, off:off + 128].astype(F32))
            o_ref[:, off:off + 128] = y
        if backward:
            gain_ref = fin[3]
            lo, hi = g * 512, (g + 1) * 512
            yg = o_ref[:, lo:hi]
            yg = yg * lax.rsqrt(jnp.mean(yg * yg, axis=-1, keepdims=True) + EPS)
            o_ref[:, lo:hi] = yg * gain_ref[:, lo:hi]


def _ssd_kernel(*refs, backward, nb):
    if backward:
        (u_ref, dtr_ref, bias_ref, alog_ref, z_ref, yf_ref, dskip_ref, gain_ref,
         o_ref, s_ref) = refs
    else:
        u_ref, dtr_ref, bias_ref, alog_ref, o_ref, s_ref = refs

    @pl.when(pl.program_id(0) == 0)
    def _():
        s_ref[...] = jnp.zeros_like(s_ref)

    for b in range(nb):
        fin = (z_ref.at[b], yf_ref.at[b], dskip_ref, gain_ref) if backward else None
        _ssd_chunk(u_ref.at[b], dtr_ref.at[b], bias_ref, alog_ref, o_ref.at[b], s_ref.at[b],
                   fin, backward)


def _ssd(u, dtr, bias, alog, backward, z_main=None, yf=None, dskip=None, gain=None):
    bsz, t, _ = u.shape
    nchunks = t // CHUNK
    idx = functools.partial(_chunk_index, nchunks=nchunks, backward=backward)
    row = lambda s: (0, idx(s), 0)
    const = lambda s: (0, 0)
    in_specs = [pl.BlockSpec((bsz, CHUNK, 1536), row),
                pl.BlockSpec((bsz, CHUNK, 128), row),
                pl.BlockSpec((1, 128), const),
                pl.BlockSpec((1, 128), const)]
    args = [u, dtr, bias, alog]
    if backward:
        in_specs += [pl.BlockSpec((bsz, CHUNK, 1024), lambda s: (0, idx(s), 2)),
                     pl.BlockSpec((bsz, CHUNK, 1024), row),
                     pl.BlockSpec((1, 1024), const),
                     pl.BlockSpec((1, 1024), const)]
        args += [z_main, yf, dskip, gain]
    return pl.pallas_call(
        functools.partial(_ssd_kernel, backward=backward, nb=bsz),
        grid=(nchunks,),
        in_specs=in_specs,
        out_specs=pl.BlockSpec((bsz, CHUNK, 1024), row),
        out_shape=jax.ShapeDtypeStruct((bsz, t, 1024), F32),
        scratch_shapes=[pltpu.VMEM((bsz, SSD_G, SSD_N, 512), F32)],
        compiler_params=pltpu.CompilerParams(
            dimension_semantics=("arbitrary",), vmem_limit_bytes=VMEM_LIMIT),
        name="ssd_bwd" if backward else "ssd_fwd",
    )(*args)


def _ret_tables(backward):
    exps = RET_EXP_B if backward else RET_EXP_F
    lg = np.log1p(-np.exp2(-np.asarray(exps, np.float64)))
    i = np.arange(CHUNK, dtype=np.float64)
    if backward:
        cum = (CHUNK - i)[None, :] * lg[:, None]
        tot = cum[:, 0]
        msk = i[None, :] >= i[:, None]
    else:
        cum = (i + 1.0)[None, :] * lg[:, None]
        tot = cum[:, -1]
        msk = i[None, :] <= i[:, None]
    dec = np.where(msk[None], np.exp(cum[:, :, None] - cum[:, None, :]), 0.0)
    inter = np.broadcast_to(np.exp(cum)[:, :, None], (RET_H, CHUNK, 128))
    toend = np.broadcast_to(np.exp(tot[:, None] - cum)[:, :, None], (RET_H, CHUNK, 128))
    tables = np.stack([dec, inter, toend], axis=1).astype(np.float32)
    return tables, [float(np.exp(v)) for v in tot]


def _ret_chunk(qk_ref, v_ref, cos, sin, tab_ref, o_ref, s_ref, fin, backward, etot):
    lane = lax.broadcasted_iota(jnp.int32, (CHUNK, 128), 1)
    lo_half = (lane % QK) < (QK // 2)
    first = lane < QK
    srow = lax.broadcasted_iota(jnp.int32, (128, 256), 0) < QK
    scol = lax.broadcasted_iota(jnp.int32, (128, 256), 1) < 128
    diag = srow == scol
    for pp in range(RET_H // 2):
        h0 = 2 * pp
        q = _rope128(qk_ref[:, pp * 128:(pp + 1) * 128].astype(F32), cos, sin, lo_half)
        k = _rope128(qk_ref[:, 256 + pp * 128:256 + (pp + 1) * 128].astype(F32) * (QK ** -0.5),
                     cos, sin, lo_half)
        qb = q.astype(BF16)
        kb = k.astype(BF16)
        k_t = kb.astype(F32).T.astype(BF16)
        zero = jnp.zeros_like(kb)
        k2 = jnp.concatenate([jnp.where(first, kb, zero),
                              jnp.where(first, zero, kb)], axis=0)
        scores = _dot_nt(qb, k2)
        v2 = v_ref[:, h0 * 128:(h0 + 2) * 128]
        zv = jnp.zeros((CHUNK, 128), BF16)
        vdiag = jnp.concatenate(
            [jnp.concatenate([v2[:, 0:128], zv], axis=1),
             jnp.concatenate([zv, v2[:, 128:256]], axis=1)], axis=0)
        dec = jnp.concatenate([tab_ref[h0, 0], tab_ref[h0 + 1, 0]], axis=1)
        inter = jnp.concatenate([tab_ref[h0, 1], tab_ref[h0 + 1, 1]], axis=1)
        toend = jnp.concatenate([tab_ref[h0, 2], tab_ref[h0 + 1, 2]], axis=1)
        s_pair = s_ref[pp]
        y = _dot((scores * dec).astype(BF16), vdiag) + inter * _dot(qb, s_pair.astype(BF16))
        upd = _dot(k_t, (v2.astype(F32) * toend).astype(BF16))
        s_ref[pp] = (jnp.where(scol, etot[h0], etot[h0 + 1]) * s_pair
                     + jnp.where(diag, upd, 0.0))
        for hh in range(2):
            h = h0 + hh
            yh = y[:, hh * 128:(hh + 1) * 128]
            if backward:
                g_ref, yf_ref, gain_ref = fin
                yh = yh + yf_ref[:, h * 128:(h + 1) * 128]
                yh = yh * lax.rsqrt(jnp.mean(yh * yh, axis=-1, keepdims=True) + EPS) * gain_ref[...]
                yh = yh * _silu(g_ref[:, h * 128:(h + 1) * 128].astype(F32))
            o_ref[:, h * 128:(h + 1) * 128] = yh.astype(o_ref.dtype)


def _ret_kernel(*refs, backward, etot, nb):
    if backward:
        qk_ref, v_ref, cos_ref, sin_ref, tab_ref, g_ref, yf_ref, gain_ref, o_ref, s_ref = refs
    else:
        qk_ref, v_ref, cos_ref, sin_ref, tab_ref, o_ref, s_ref = refs

    @pl.when(pl.program_id(0) == 0)
    def _():
        s_ref[...] = jnp.zeros_like(s_ref)

    cos = cos_ref[...]
    sin = sin_ref[...]
    for b in range(nb):
        fin = (g_ref.at[b], yf_ref.at[b], gain_ref) if backward else None
        _ret_chunk(qk_ref.at[b], v_ref.at[b], cos, sin, tab_ref, o_ref.at[b], s_ref.at[b],
                   fin, backward, etot)


def _ret(main, cos_t, sin_t, backward, yf=None, gain=None):
    bsz, t, _ = main.shape
    nchunks = t // CHUNK
    tables, etot = _ret_tables(backward)
    idx = functools.partial(_chunk_index, nchunks=nchunks, backward=backward)
    row = lambda s: (0, idx(s), 0)
    in_specs = [pl.BlockSpec((bsz, CHUNK, 512), lambda s: (0, idx(s), 9)),
                pl.BlockSpec((bsz, CHUNK, 512), lambda s: (0, idx(s), 10)),
                pl.BlockSpec((CHUNK, 128), lambda s: (idx(s), 0)),
                pl.BlockSpec((CHUNK, 128), lambda s: (idx(s), 0)),
                pl.BlockSpec((RET_H, 3, CHUNK, 128), lambda s: (0, 0, 0, 0))]
    args = [main, main, cos_t, sin_t, jnp.asarray(tables)]
    if backward:
        in_specs += [pl.BlockSpec((bsz, CHUNK, 512), lambda s: (0, idx(s), 11)),
                     pl.BlockSpec((bsz, CHUNK, 512), row),
                     pl.BlockSpec((1, 128), lambda s: (0, 0))]
        args += [main, yf, gain]
    return pl.pallas_call(
        functools.partial(_ret_kernel, backward=backward, etot=etot, nb=bsz),
        grid=(nchunks,),
        in_specs=in_specs,
        out_specs=pl.BlockSpec((bsz, CHUNK, 512), row),
        out_shape=jax.ShapeDtypeStruct((bsz, t, 512), BF16 if backward else F32),
        scratch_shapes=[pltpu.VMEM((bsz, RET_H // 2, 128, 256), F32)],
        compiler_params=pltpu.CompilerParams(
            dimension_semantics=("arbitrary",), vmem_limit_bytes=VMEM_LIMIT),
        name="ret_bwd" if backward else "ret_fwd",
    )(*args)


def _outproj_kernel(x_ref, a_ref, s_ref, r_ref, ada_ref, w_ref, o_ref, *, tm, blk0, ctx_row):
    b = pl.program_id(0)
    i = pl.program_id(1) + blk0
    acc = _dot(a_ref[...], w_ref[0:512, :])
    acc += _dot(s_ref[...].astype(BF16), w_ref[512:1536, :])
    acc += _dot(r_ref[...], w_ref[1536:2048, :])
    rows = i * tm + lax.broadcasted_iota(jnp.int32, (tm, 1), 0)
    gate = _row_mod(ada_ref, b, rows, 2 * D, 3 * D, ctx_row)
    o_ref[...] = x_ref[...] + gate * acc


def _outproj(xcat, attn_o, ssd_o, ret_o, ada, w_out, ctx_row, latent_only):
    bsz, t, _ = xcat.shape
    tm = N_CTX
    blk0 = 1 if latent_only else 0
    nb = t // tm - blk0
    row = lambda b, i: (b, i + blk0, 0)
    return pl.pallas_call(
        functools.partial(_outproj_kernel, tm=tm, blk0=blk0, ctx_row=ctx_row),
        grid=(bsz, nb),
        in_specs=[pl.BlockSpec((None, tm, D), row),
                  pl.BlockSpec((None, tm, 512), row),
                  pl.BlockSpec((None, tm, 1024), row),
                  pl.BlockSpec((None, tm, 512), row),
                  pl.BlockSpec((8, 3 * D), lambda b, i: (0, 0)),
                  pl.BlockSpec((2 * D, D), lambda b, i: (0, 0))],
        out_specs=pl.BlockSpec((None, tm, D), lambda b, i: (b, i, 0)),
        out_shape=jax.ShapeDtypeStruct((bsz, nb * tm, D), F32),
        compiler_params=pltpu.CompilerParams(
            dimension_semantics=("arbitrary", "arbitrary"), vmem_limit_bytes=VMEM_LIMIT),
        name="outproj",
    )(xcat, attn_o, ssd_o, ret_o, ada, w_out)


def _rope_tables(seq):
    n_rows = seq // GRID_W
    row = np.repeat(np.arange(n_rows, dtype=np.float32), GRID_W)
    col = np.tile(np.arange(GRID_W, dtype=np.float32), n_rows)
    n_freq = QK // 4
    inv_freq = (np.float32(ROPE_BASE) ** (-np.arange(n_freq, dtype=np.float32) / n_freq)).astype(np.float32)
    ang = np.concatenate([row[:, None] * inv_freq, col[:, None] * inv_freq], axis=-1).astype(np.float32)
    cos = np.concatenate([np.ones((N_CTX, QK // 2)), np.cos(ang.astype(np.float64))], axis=0)
    sin = np.concatenate([np.zeros((N_CTX, QK // 2)), np.sin(ang.astype(np.float64))], axis=0)
    cos_t = np.tile(cos, (1, 4)).astype(np.float32)
    sin_t = np.tile(np.concatenate([-sin, sin], axis=1), (1, 2)).astype(np.float32)
    return jnp.asarray(cos_t), jnp.asarray(sin_t)


_MAIN_COLS = ((0, 2048), (3616, 4640), (2048, 3584), (4640, 6176))
_DT_COLS = (3584, 3616)


def _pad_lanes(v, n=128):
    v = v.reshape(1, -1)
    return jnp.pad(v, ((0, 0), (0, n - v.shape[1])))


def kernel(x, c, ctx, c_ctx, w_ada, b_ada, w_in, w_out, attn_q_norm, attn_k_norm,
           lambda_q1, lambda_k1, lambda_q2, lambda_k2, attn_subln,
           ssd_conv_w, ssd_conv_b, ssd_dt_bias, ssd_a_log, ssd_d, ssd_norm, ret_norm):
    bsz, seq, _ = x.shape
    depth = w_ada.shape[0]
    assert bsz + 1 <= 8 and ctx.shape[1] == N_CTX
    ctx_row = bsz
    xcat = jnp.concatenate([ctx, x], axis=1)
    cvec = jnp.zeros((8, D), F32).at[0:bsz].set(c).at[ctx_row].set(c_ctx)
    cos_t, sin_t = _rope_tables(seq)
    gsel = np.arange(512) // QK
    gmat = jnp.asarray((gsel[:, None] == gsel[None, :]).astype(np.float32), BF16)

    for layer in range(depth):
        last = layer == depth - 1
        lam_init = 0.8 - 0.6 * math.exp(-0.3 * layer)
        w_l = w_in[layer]
        w_main = jnp.concatenate([w_l[:, a:b] for a, b in _MAIN_COLS], axis=1).astype(BF16)
        w_dt = jnp.pad(w_l[:, _DT_COLS[0]:_DT_COLS[1]], ((0, 0), (0, 96))).astype(BF16)

        ada = _ada(cvec, w_ada[layer], b_ada[layer].reshape(1, -1))
        main, dtr = _inproj(xcat, ada, w_main, w_dt, ctx_row)

        qp, kp, vt = _attnprep(main, cos_t, sin_t,
                               jnp.tile(attn_q_norm[layer], 8).reshape(1, 512),
                               jnp.tile(attn_k_norm[layer], 8).reshape(1, 512), gmat, tm=ATT_TK)
        lam_vec = jnp.stack([lambda_q1[layer], lambda_k1[layer], lambda_q2[layer], lambda_k2[layer]])
        attn_o = _attn(qp, kp, vt, main, lam_vec, attn_subln[layer].reshape(128, 1), lam_init,
                       tq=N_CTX, tk=ATT_TK)

        u = _conv(main, ssd_conv_w[layer], ssd_conv_b[layer].reshape(1, -1), tm=768)
        bias = _pad_lanes(ssd_dt_bias[layer])
        alog = _pad_lanes(ssd_a_log[layer])
        ssd_f = _ssd(u, dtr, bias, alog, backward=False)
        ssd_o = _ssd(u, dtr, bias, alog, backward=True, z_main=main, yf=ssd_f,
                     dskip=jnp.repeat(ssd_d[layer], SSD_P).reshape(1, 1024),
                     gain=ssd_norm[layer].reshape(1, 1024))

        ret_f = _ret(main, cos_t, sin_t, backward=False)
        ret_o = _ret(main, cos_t, sin_t, backward=True, yf=ret_f,
                     gain=ret_norm[layer].reshape(1, 128))

        xcat = _outproj(xcat, attn_o, ssd_o, ret_o, ada, w_out[layer].astype(BF16), ctx_row,
                        latent_only=last)
    return xcat
```

```python
import functools
import math

import numpy as np
import jax
import jax.numpy as jnp
from jax import lax
from jax.experimental import pallas as pl
from jax.experimental.pallas import tpu as pltpu

F32 = jnp.float32
BF16 = jnp.bfloat16

D = 1024
N_CTX = 256
GRID_W = 64
EPS = 1e-6
ROPE_BASE = 10000.0
CHUNK = 128
QK = 64
ATT_H = 4
SSD_H = 16
SSD_G = 2
SSD_HPG = 8
SSD_P = 64
SSD_N = 128
RET_H = 4
RET_EXP_F = (5.0, 6.0, 7.0, 8.0)
RET_EXP_B = (5.5, 6.5, 7.5, 8.5)
N_MAIN = 6144
LOG2E = math.log2(math.e)
VMEM_LIMIT = 56 * 1024 * 1024
ATT_TK = 768
_DONE = object()


def _silu(v):
    return v * (0.5 + 0.5 * jnp.tanh(0.5 * v))


def _split3(v):
    hi = v.astype(BF16)
    r1 = v - hi.astype(F32)
    mid = r1.astype(BF16)
    lo = (r1 - mid.astype(F32)).astype(BF16)
    return hi, mid, lo


def _dot(a, b):
    return jnp.dot(a, b, preferred_element_type=F32)


def _dot_nt(a, b):
    return lax.dot_general(a, b, (((1,), (1,)), ((), ())), preferred_element_type=F32)


def _tri_dot(tri, v):
    hi, mid, lo = _split3(v)
    return _dot(tri, hi) + _dot(tri, mid) + _dot(tri, lo)


def _ada_kernel(c_ref, w_ref, b_ref, o_ref):
    o_ref[...] = _dot(_silu(c_ref[...]), w_ref[...]) + b_ref[...]


def _ada(cvec, w, b):
    n = w.shape[1]
    tn = 768
    return pl.pallas_call(
        _ada_kernel,
        grid=(n // tn,),
        in_specs=[pl.BlockSpec((8, D), lambda j: (0, 0)),
                  pl.BlockSpec((D, tn), lambda j: (0, j)),
                  pl.BlockSpec((1, tn), lambda j: (0, j))],
        out_specs=pl.BlockSpec((8, tn), lambda j: (0, j)),
        out_shape=jax.ShapeDtypeStruct((8, n), F32),
        name="ada",
    )(cvec, w, b)


def _row_mod(ada_ref, b, rows, lo, hi, ctx_row):
    vb = ada_ref[pl.ds(b, 1), lo:hi]
    vc = ada_ref[ctx_row:ctx_row + 1, lo:hi]
    return jnp.where(rows < N_CTX, vc, vb)


def _inproj_kernel(x_ref, ada_ref, w_ref, wdt_ref, main_ref, dt_ref, *, tm, tn, ctx_row):
    b = pl.program_id(0)
    i = pl.program_id(1)
    x = x_ref[...]
    xn = x * lax.rsqrt(jnp.mean(x * x, axis=-1, keepdims=True) + EPS)
    rows = i * tm + lax.broadcasted_iota(jnp.int32, (tm, 1), 0)
    shift = _row_mod(ada_ref, b, rows, 0, D, ctx_row)
    scale = _row_mod(ada_ref, b, rows, D, 2 * D, ctx_row)
    h = (xn * (1.0 + scale) + shift).astype(BF16)
    for c in range(N_MAIN // tn):
        main_ref[:, c * tn:(c + 1) * tn] = _dot(h, w_ref[:, c * tn:(c + 1) * tn]).astype(BF16)
    dt_ref[...] = _dot(h, wdt_ref[...])


def _inproj(xcat, ada, w_main, w_dt, ctx_row):
    bsz, t, _ = xcat.shape
    tm, tn = 384, 768
    return pl.pallas_call(
        functools.partial(_inproj_kernel, tm=tm, tn=tn, ctx_row=ctx_row),
        grid=(bsz, t // tm),
        in_specs=[pl.BlockSpec((None, tm, D), lambda b, i: (b, i, 0)),
                  pl.BlockSpec((8, 3 * D), lambda b, i: (0, 0)),
                  pl.BlockSpec((D, N_MAIN), lambda b, i: (0, 0)),
                  pl.BlockSpec((D, 128), lambda b, i: (0, 0))],
        out_specs=[pl.BlockSpec((None, tm, N_MAIN), lambda b, i: (b, i, 0)),
                   pl.BlockSpec((None, tm, 128), lambda b, i: (b, i, 0))],
        out_shape=[jax.ShapeDtypeStruct((bsz, t, N_MAIN), BF16),
                   jax.ShapeDtypeStruct((bsz, t, 128), F32)],
        compiler_params=pltpu.CompilerParams(
            dimension_semantics=("arbitrary", "arbitrary"), vmem_limit_bytes=VMEM_LIMIT),
        name="inproj",
    )(xcat, ada, w_main, w_dt)


def _rope128(x, cos, sin, lo_half):
    partner = jnp.where(lo_half, pltpu.roll(x, 96, 1), pltpu.roll(x, 32, 1))
    return x * cos + partner * sin


def _attnprep_kernel(q_ref, k_ref, v_ref, cos_ref, sin_ref, gq_ref, gk_ref, gmat_ref,
                     qo_ref, ko_ref, vt_ref, *, tm):
    cos = cos_ref[...]
    sin = sin_ref[...]
    lane = lax.broadcasted_iota(jnp.int32, (tm, 128), 1)
    lo_half = (lane % QK) < (QK // 2)
    gmat = gmat_ref[...]

    def norm_rope(src_ref, gain_ref, out_ref, out_scale):
        x = src_ref[...].astype(F32)
        ss = _dot((x * x).astype(BF16), gmat)
        y = x * lax.rsqrt(ss * (1.0 / QK) + EPS) * gain_ref[...]
        for g in range(4):
            yg = _rope128(y[:, g * 128:(g + 1) * 128], cos, sin, lo_half)
            out_ref[:, g * 128:(g + 1) * 128] = (yg * out_scale).astype(BF16)

    norm_rope(q_ref, gq_ref, qo_ref, (QK ** -0.5) * LOG2E)
    norm_rope(k_ref, gk_ref, ko_ref, 1.0)
    for h in range(ATT_H):
        vt_ref[h] = v_ref[:, h * 128:(h + 1) * 128].astype(F32).T.astype(BF16)


def _attnprep(main, cos_t, sin_t, gq, gk, gmat, tm):
    bsz, t, _ = main.shape
    nb = t // tm
    return pl.pallas_call(
        functools.partial(_attnprep_kernel, tm=tm),
        grid=(bsz, nb),
        in_specs=[pl.BlockSpec((None, tm, 512), lambda b, i: (b, i, 0)),
                  pl.BlockSpec((None, tm, 512), lambda b, i: (b, i, 1)),
                  pl.BlockSpec((None, tm, 512), lambda b, i: (b, i, 2)),
                  pl.BlockSpec((tm, 128), lambda b, i: (i, 0)),
                  pl.BlockSpec((tm, 128), lambda b, i: (i, 0)),
                  pl.BlockSpec((1, 512), lambda b, i: (0, 0)),
                  pl.BlockSpec((1, 512), lambda b, i: (0, 0)),
                  pl.BlockSpec((512, 512), lambda b, i: (0, 0))],
        out_specs=[pl.BlockSpec((None, tm, 512), lambda b, i: (b, i, 0)),
                   pl.BlockSpec((None, tm, 512), lambda b, i: (b, i, 0)),
                   pl.BlockSpec((None, ATT_H, None, 128, tm), lambda b, i: (b, 0, i, 0, 0))],
        out_shape=[jax.ShapeDtypeStruct((bsz, t, 512), BF16),
                   jax.ShapeDtypeStruct((bsz, t, 512), BF16),
                   jax.ShapeDtypeStruct((bsz, ATT_H, nb, 128, tm), BF16)],
        compiler_params=pltpu.CompilerParams(
            dimension_semantics=("arbitrary", "arbitrary"), vmem_limit_bytes=VMEM_LIMIT),
        name="attnprep",
    )(main, main, main, cos_t, sin_t, gq, gk, gmat)


def _attn_kernel(q_ref, qn_ref, k_ref, vt_ref, gate_ref, lam_ref, subln_ref, o_ref,
                 q2_ref, m_ref, l_ref, acc_ref, s_ref, mt_ref, *, tq, tk, nkv, lam_init):
    qi = pl.program_id(2)
    lane = lax.broadcasted_iota(jnp.int32, (tq, 128), 1)
    m_ref[...] = jnp.full((1, 2 * tq), -jnp.inf, F32)
    l_ref[...] = jnp.zeros((1, 2 * tq), F32)
    acc_ref[...] = jnp.zeros((128, 2 * tq), F32)

    def stack_maps(src_ref):
        q = src_ref[...]
        zero = jnp.zeros_like(q)
        q2_ref[0:tq, :] = jnp.where(lane < QK, q, zero)
        q2_ref[tq:2 * tq, :] = jnp.where(lane >= QK, q, zero)

    def hand_off():
        stack_maps(qn_ref)
        scores(0, 2, False)

    def scores(j, slot, ctx_only):
        st = _dot_nt(k_ref[pl.ds(j * tk, tk), :], q2_ref[...])
        if ctx_only:
            krow = j * tk + lax.broadcasted_iota(jnp.int32, (tk, 1), 0)
            st = jnp.where(krow < N_CTX, st, -jnp.inf)
        s_ref[slot] = st
        mt_ref[slot] = jnp.max(st, axis=0, keepdims=True)

    def consume(j, slot):
        m_old = m_ref[...]
        m_new = jnp.maximum(m_old, mt_ref[slot])
        alpha = jnp.exp2(m_old - m_new)
        p = jnp.exp2(s_ref[slot] - m_new)
        l_ref[...] = alpha * l_ref[...] + jnp.sum(p, axis=0, keepdims=True)
        acc_ref[...] = alpha * acc_ref[...] + _dot(vt_ref[j], p.astype(BF16))
        m_ref[...] = m_new

    def finalize():
        lam_v = lam_ref[...]
        lam = (jnp.exp(jnp.sum(lam_v[0:1] * lam_v[1:2], axis=-1, keepdims=True))
               - jnp.exp(jnp.sum(lam_v[2:3] * lam_v[3:4], axis=-1, keepdims=True)) + lam_init)
        inv = 1.0 / l_ref[...]
        acc = acc_ref[...]
        o = acc[:, 0:tq] * inv[:, 0:tq] - lam * (acc[:, tq:2 * tq] * inv[:, tq:2 * tq])
        o = o * lax.rsqrt(jnp.mean(o * o, axis=0, keepdims=True) + EPS)
        o = o * subln_ref[...] * (1.0 - lam_init)
        o_ref[...] = (o.T * _silu(gate_ref[...].astype(F32))).astype(BF16)

    @pl.when(qi == 0)
    def _():
        stack_maps(q_ref)
        scores(0, 0, True)
        hand_off()
        consume(0, 0)
        finalize()

    @pl.when(qi != 0)
    def _():
        for j in range(nkv):
            if j + 1 < nkv:
                scores(j + 1, (j + 1) % 2, False)
            else:
                hand_off()
            consume(j, 2 if j == 0 else j % 2)
        finalize()


def _attn(qp, kp, vt, main, lam_vec, subln, lam_init, tq, tk):
    bsz, t, _ = qp.shape
    nkv = t // tk
    assert tq == N_CTX and tk >= N_CTX and nkv >= 2
    nq = t // tq
    return pl.pallas_call(
        functools.partial(_attn_kernel, tq=tq, tk=tk, nkv=nkv, lam_init=lam_init),
        grid=(bsz, ATT_H, nq),
        in_specs=[pl.BlockSpec((None, tq, 128), lambda b, h, i: (b, i, h)),
                  pl.BlockSpec((None, tq, 128), lambda b, h, i: (b, jnp.minimum(i + 1, nq - 1), h)),
                  pl.BlockSpec((None, t, 128), lambda b, h, i: (b, 0, h)),
                  pl.BlockSpec((None, None, nkv, 128, tk), lambda b, h, i: (b, h, 0, 0, 0)),
                  pl.BlockSpec((None, tq, 128), lambda b, h, i: (b, i, 12 + h)),
                  pl.BlockSpec((4, QK), lambda b, h, i: (0, 0)),
                  pl.BlockSpec((128, 1), lambda b, h, i: (0, 0))],
        out_specs=pl.BlockSpec((None, tq, 128), lambda b, h, i: (b, i, h)),
        out_shape=jax.ShapeDtypeStruct((bsz, t, 512), BF16),
        scratch_shapes=[pltpu.VMEM((2 * tq, 128), BF16),
                        pltpu.VMEM((1, 2 * tq), F32),
                        pltpu.VMEM((1, 2 * tq), F32),
                        pltpu.VMEM((128, 2 * tq), F32),
                        pltpu.VMEM((3, tk, 2 * tq), F32),
                        pltpu.VMEM((3, 1, 2 * tq), F32)],
        compiler_params=pltpu.CompilerParams(
            dimension_semantics=("arbitrary", "arbitrary", "arbitrary"),
            vmem_limit_bytes=VMEM_LIMIT),
        name="attn",
    )(qp, qp, kp, vt, main, lam_vec, subln)


def _conv_kernel(x_ref, prev_ref, next_ref, w_ref, b_ref, o_ref, *, tm, t):
    i = pl.program_id(1)
    x = x_ref[...].astype(F32)
    loc = lax.broadcasted_iota(jnp.int32, (tm, 1), 0)
    row = i * tm + loc
    xm1 = pltpu.roll(x, 1, 0)
    xm1 = jnp.where(loc == 0, prev_ref[7:8, :].astype(F32), xm1)
    xm1 = jnp.where((row == 0) | (row == N_CTX), 0.0, xm1)
    xp1 = pltpu.roll(x, tm - 1, 0)
    xp1 = jnp.where(loc == tm - 1, next_ref[0:1, :].astype(F32), xp1)
    xp1 = jnp.where((row == t - 1) | (row == N_CTX - 1), 0.0, xp1)
    w = w_ref[...]
    y = w[0:1] * xm1 + w[1:2] * x + w[2:3] * xp1 + b_ref[...]
    o_ref[...] = _silu(y).astype(BF16)


def _conv(main, w, b, tm):
    bsz, t, _ = main.shape
    r8 = tm // 8
    last8 = t // 8 - 1
    return pl.pallas_call(
        functools.partial(_conv_kernel, tm=tm, t=t),
        grid=(bsz, t // tm),
        in_specs=[pl.BlockSpec((None, tm, 1536), lambda b, i: (b, i, 2)),
                  pl.BlockSpec((None, 8, 1536), lambda b, i: (b, jnp.maximum(i * r8 - 1, 0), 2)),
                  pl.BlockSpec((None, 8, 1536), lambda b, i: (b, jnp.minimum((i + 1) * r8, last8), 2)),
                  pl.BlockSpec((3, 1536), lambda b, i: (0, 0)),
                  pl.BlockSpec((1, 1536), lambda b, i: (0, 0))],
        out_specs=pl.BlockSpec((None, tm, 1536), lambda b, i: (b, i, 0)),
        out_shape=jax.ShapeDtypeStruct((bsz, t, 1536), BF16),
        compiler_params=pltpu.CompilerParams(
            dimension_semantics=("arbitrary", "arbitrary"),
            vmem_limit_bytes=VMEM_LIMIT),
        name="conv",
    )(main, main, main, w, b)


def _chunk_index(s, nchunks, backward):
    if not backward:
        return s
    nctx = N_CTX // CHUNK
    return jnp.where(s < nctx, nctx - 1 - s, nchunks - 1 + nctx - s)


def _tri_mask(backward):
    r = lax.broadcasted_iota(jnp.int32, (CHUNK, CHUNK), 0)
    c = lax.broadcasted_iota(jnp.int32, (CHUNK, CHUNK), 1)
    return (c >= r) if backward else (c <= r)


def _ssd_chunk(u_ref, dtr_ref, bias_ref, alog_ref, o_ref, s_ref, fin, backward):
    mask = _tri_mask(backward)
    tri = jnp.where(mask, 1.0, 0.0).astype(BF16)
    pre = dtr_ref[...] + bias_ref[...]
    dt = jnp.maximum(pre, 0.0) + jnp.log1p(jnp.exp(-jnp.abs(pre)))
    la = dt * (-LOG2E * jnp.exp(alog_ref[...]))
    cum = _tri_dot(tri, la)
    cum_t = cum.T
    dt_t = dt.T
    edge = 0 if backward else CHUNK - 1
    etot = jnp.exp2(cum[edge:edge + 1, :])
    wdt_t = jnp.exp2(cum_t[:, edge:edge + 1] - cum_t) * dt_t
    first = lax.broadcasted_iota(jnp.int32, (CHUNK, 128), 1) < SSD_P
    first_row = lax.broadcasted_iota(jnp.int32, (1, 128), 1) < SSD_P
    col0 = SSD_H if backward else 0
    yield

    for g in range(SSD_G):
        k = u_ref[:, 1024 + g * SSD_N:1024 + (g + 1) * SSD_N]
        q = u_ref[:, 1280 + g * SSD_N:1280 + (g + 1) * SSD_N]
        scores = _dot_nt(q, k)
        k_t = k.astype(F32).T
        y_inter = _dot(q, s_ref[g].astype(BF16))
        ys = []
        yield
        for pp in range(SSD_HPG // 2):
            h0 = g * SSD_HPG + 2 * pp
            c0 = col0 + h0
            off = h0 * SSD_P
            xs = u_ref[:, off:off + 128]
            zero = jnp.zeros_like(xs)
            vals = jnp.concatenate([jnp.where(first, xs, zero),
                                    jnp.where(first, zero, xs)], axis=0)
            wts, kws, ecs = [], [], []
            for hh in range(2):
                c = c0 + hh
                cum_c = jnp.broadcast_to(cum[:, c:c + 1], (CHUNK, CHUNK))
                dec = jnp.exp2(jnp.where(mask, cum_c - cum_t[c:c + 1, :], -jnp.inf))
                wts.append((scores * dec * dt_t[c:c + 1, :]).astype(BF16))
                kws.append((k_t * wdt_t[c:c + 1, :]).astype(BF16))
                ecs.append(jnp.exp2(cum_c))
            y = (jnp.where(first, ecs[0], ecs[1]) * y_inter[:, pp * 128:(pp + 1) * 128]
                 + _dot(jnp.concatenate(wts, axis=1), vals))
            etot_p = jnp.where(first_row, jnp.broadcast_to(etot[:, c0:c0 + 1], (1, 128)),
                               jnp.broadcast_to(etot[:, c0 + 1:c0 + 2], (1, 128)))
            s_old = s_ref[g, :, pp * 128:(pp + 1) * 128]
            s_ref[g, :, pp * 128:(pp + 1) * 128] = (
                etot_p * s_old + _dot(jnp.concatenate(kws, axis=1), vals))
            if backward:
                z_ref, yf_ref, dskip_ref, _ = fin
                y = y + yf_ref[:, off:off + 128] + xs.astype(F32) * dskip_ref[:, off:off + 128]
                y = y * _silu(z_ref[:, off:off + 128].astype(F32))
                ys.append(y)
            else:
                o_ref[:, off:off + 128] = y
            yield
        if backward:
            gain_ref = fin[3]
            lo, hi = g * 512, (g + 1) * 512
            yg = jnp.concatenate(ys, axis=1)
            yg = yg * lax.rsqrt(jnp.mean(yg * yg, axis=-1, keepdims=True) + EPS)
            o_ref[:, lo:hi] = (yg * gain_ref[:, lo:hi]).astype(o_ref.dtype)


def _ret_tables(backward):
    exps = RET_EXP_B if backward else RET_EXP_F
    lg = np.log1p(-np.exp2(-np.asarray(exps, np.float64)))
    i = np.arange(CHUNK, dtype=np.float64)
    if backward:
        cum = (CHUNK - i)[None, :] * lg[:, None]
        tot = cum[:, 0]
        msk = i[None, :] >= i[:, None]
    else:
        cum = (i + 1.0)[None, :] * lg[:, None]
        tot = cum[:, -1]
        msk = i[None, :] <= i[:, None]
    dec = np.where(msk[None], np.exp(cum[:, :, None] - cum[:, None, :]), 0.0)
    inter = np.broadcast_to(np.exp(cum)[:, :, None], (RET_H, CHUNK, 128))
    toend = np.broadcast_to(np.exp(tot[:, None] - cum)[:, :, None], (RET_H, CHUNK, 128))
    tables = np.stack([dec, inter, toend], axis=1).astype(np.float32)
    return tables, [float(np.exp(v)) for v in tot]


def _ret_chunk(qk_ref, v_ref, cos, sin, tab_ref, o_ref, s_ref, fin, backward, etot):
    lane = lax.broadcasted_iota(jnp.int32, (CHUNK, 128), 1)
    lo_half = (lane % QK) < (QK // 2)
    first = lane < QK
    srow = lax.broadcasted_iota(jnp.int32, (128, 256), 0) < QK
    scol = lax.broadcasted_iota(jnp.int32, (128, 256), 1) < 128
    diag = srow == scol
    for pp in range(RET_H // 2):
        h0 = 2 * pp
        q = _rope128(qk_ref[:, pp * 128:(pp + 1) * 128].astype(F32), cos, sin, lo_half)
        k = _rope128(qk_ref[:, 256 + pp * 128:256 + (pp + 1) * 128].astype(F32) * (QK ** -0.5),
                     cos, sin, lo_half)
        qb = q.astype(BF16)
        kb = k.astype(BF16)
        k_t = kb.astype(F32).T.astype(BF16)
        zero = jnp.zeros_like(kb)
        k2 = jnp.concatenate([jnp.where(first, kb, zero),
                              jnp.where(first, zero, kb)], axis=0)
        scores = _dot_nt(qb, k2)
        v2 = v_ref[:, h0 * 128:(h0 + 2) * 128]
        zv = jnp.zeros((CHUNK, 128), BF16)
        vdiag = jnp.concatenate(
            [jnp.concatenate([v2[:, 0:128], zv], axis=1),
             jnp.concatenate([zv, v2[:, 128:256]], axis=1)], axis=0)
        dec = jnp.concatenate([tab_ref[h0, 0], tab_ref[h0 + 1, 0]], axis=1)
        inter = jnp.concatenate([tab_ref[h0, 1], tab_ref[h0 + 1, 1]], axis=1)
        toend = jnp.concatenate([tab_ref[h0, 2], tab_ref[h0 + 1, 2]], axis=1)
        s_pair = s_ref[pp]
        y = _dot((scores * dec).astype(BF16), vdiag) + inter * _dot(qb, s_pair.astype(BF16))
        upd = _dot(k_t, (v2.astype(F32) * toend).astype(BF16))
        s_ref[pp] = (jnp.where(scol, etot[h0], etot[h0 + 1]) * s_pair
                     + jnp.where(diag, upd, 0.0))
        for hh in range(2):
            h = h0 + hh
            yh = y[:, hh * 128:(hh + 1) * 128]
            if backward:
                g_ref, yf_ref, gain_ref = fin
                yh = yh + yf_ref[:, h * 128:(h + 1) * 128]
                yh = yh * lax.rsqrt(jnp.mean(yh * yh, axis=-1, keepdims=True) + EPS) * gain_ref[...]
                yh = yh * _silu(g_ref[:, h * 128:(h + 1) * 128].astype(F32))
            o_ref[:, h * 128:(h + 1) * 128] = yh.astype(o_ref.dtype)
        yield


def _scan_kernel(*refs, backward, etot, nb):
    if backward:
        (u_ref, dtr_ref, bias_ref, alog_ref, qk_ref, v_ref, cos_ref, sin_ref, tab_ref,
         z_ref, ssd_f_ref, dskip_ref, ssd_gain_ref, g_ref, ret_f_ref, ret_gain_ref,
         ssd_o_ref, ret_o_ref, ssd_s_ref, ret_s_ref) = refs
    else:
        (u_ref, dtr_ref, bias_ref, alog_ref, qk_ref, v_ref, cos_ref, sin_ref, tab_ref,
         ssd_o_ref, ret_o_ref, ssd_s_ref, ret_s_ref) = refs

    @pl.when(pl.program_id(0) == 0)
    def _():
        ssd_s_ref[...] = jnp.zeros_like(ssd_s_ref)
        ret_s_ref[...] = jnp.zeros_like(ret_s_ref)

    cos = cos_ref[...]
    sin = sin_ref[...]
    chains = []
    for b in range(nb):
        ssd_fin = (z_ref.at[b], ssd_f_ref.at[b], dskip_ref, ssd_gain_ref) if backward else None
        ret_fin = (g_ref.at[b], ret_f_ref.at[b], ret_gain_ref) if backward else None
        chains.append(_ssd_chunk(u_ref.at[b], dtr_ref.at[b], bias_ref, alog_ref,
                                 ssd_o_ref.at[b], ssd_s_ref.at[b], ssd_fin, backward))
        chains.append(_ret_chunk(qk_ref.at[b], v_ref.at[b], cos, sin, tab_ref,
                                 ret_o_ref.at[b], ret_s_ref.at[b], ret_fin, backward, etot))
    while chains:
        chains = [c for c in chains if next(c, _DONE) is not _DONE]


def _scan(u, dtr, main, cos_t, sin_t, bias, alog, backward, fwd=None, dskip=None,
          ssd_gain=None, ret_gain=None):
    bsz, t, _ = u.shape
    nchunks = t // CHUNK
    tables, etot = _ret_tables(backward)
    idx = functools.partial(_chunk_index, nchunks=nchunks, backward=backward)
    row = lambda s: (0, idx(s), 0)
    const = lambda s: (0, 0)
    in_specs = [pl.BlockSpec((bsz, CHUNK, 1536), row),
                pl.BlockSpec((bsz, CHUNK, 128), row),
                pl.BlockSpec((1, 128), const),
                pl.BlockSpec((1, 128), const),
                pl.BlockSpec((bsz, CHUNK, 512), lambda s: (0, idx(s), 9)),
                pl.BlockSpec((bsz, CHUNK, 512), lambda s: (0, idx(s), 10)),
                pl.BlockSpec((CHUNK, 128), lambda s: (idx(s), 0)),
                pl.BlockSpec((CHUNK, 128), lambda s: (idx(s), 0)),
                pl.BlockSpec((RET_H, 3, CHUNK, 128), lambda s: (0, 0, 0, 0))]
    args = [u, dtr, bias, alog, main, main, cos_t, sin_t, jnp.asarray(tables)]
    if backward:
        ssd_f, ret_f = fwd
        in_specs += [pl.BlockSpec((bsz, CHUNK, 1024), lambda s: (0, idx(s), 2)),
                     pl.BlockSpec((bsz, CHUNK, 1024), row),
                     pl.BlockSpec((1, 1024), const),
                     pl.BlockSpec((1, 1024), const),
                     pl.BlockSpec((bsz, CHUNK, 512), lambda s: (0, idx(s), 11)),
                     pl.BlockSpec((bsz, CHUNK, 512), row),
                     pl.BlockSpec((1, 128), const)]
        args += [main, ssd_f, dskip, ssd_gain, main, ret_f, ret_gain]
    out_dtype = BF16 if backward else F32
    return pl.pallas_call(
        functools.partial(_scan_kernel, backward=backward, etot=etot, nb=bsz),
        grid=(nchunks,),
        in_specs=in_specs,
        out_specs=[pl.BlockSpec((bsz, CHUNK, 1024), row),
                   pl.BlockSpec((bsz, CHUNK, 512), row)],
        out_shape=[jax.ShapeDtypeStruct((bsz, t, 1024), out_dtype),
                   jax.ShapeDtypeStruct((bsz, t, 512), out_dtype)],
        scratch_shapes=[pltpu.VMEM((bsz, SSD_G, SSD_N, 512), F32),
                        pltpu.VMEM((bsz, RET_H // 2, 128, 256), F32)],
        compiler_params=pltpu.CompilerParams(
            dimension_semantics=("arbitrary",), vmem_limit_bytes=VMEM_LIMIT),
        name="scan_bwd" if backward else "scan_fwd",
    )(*args)


def _outproj_kernel(x_ref, a_ref, s_ref, r_ref, ada_ref, w_ref, o_ref, *, tm, blk0, ctx_row):
    b = pl.program_id(0)
    i = pl.program_id(1) + blk0
    acc = _dot(a_ref[...], w_ref[0:512, :])
    acc += _dot(s_ref[...], w_ref[512:1536, :])
    acc += _dot(r_ref[...], w_ref[1536:2048, :])
    rows = i * tm + lax.broadcasted_iota(jnp.int32, (tm, 1), 0)
    gate = _row_mod(ada_ref, b, rows, 2 * D, 3 * D, ctx_row)
    o_ref[...] = x_ref[...] + gate * acc


def _outproj(xcat, attn_o, ssd_o, ret_o, ada, w_out, ctx_row, latent_only):
    bsz, t, _ = xcat.shape
    tm = N_CTX if latent_only else 768
    blk0 = 1 if latent_only else 0
    nb = t // tm - blk0
    row = lambda b, i: (b, i + blk0, 0)
    return pl.pallas_call(
        functools.partial(_outproj_kernel, tm=tm, blk0=blk0, ctx_row=ctx_row),
        grid=(bsz, nb),
        in_specs=[pl.BlockSpec((None, tm, D), row),
                  pl.BlockSpec((None, tm, 512), row),
                  pl.BlockSpec((None, tm, 1024), row),
                  pl.BlockSpec((None, tm, 512), row),
                  pl.BlockSpec((8, 3 * D), lambda b, i: (0, 0)),
                  pl.BlockSpec((2 * D, D), lambda b, i: (0, 0))],
        out_specs=pl.BlockSpec((None, tm, D), lambda b, i: (b, i, 0)),
        out_shape=jax.ShapeDtypeStruct((bsz, nb * tm, D), F32),
        compiler_params=pltpu.CompilerParams(
            dimension_semantics=("arbitrary", "arbitrary"), vmem_limit_bytes=VMEM_LIMIT),
        name="outproj",
    )(xcat, attn_o, ssd_o, ret_o, ada, w_out)


def _rope_tables(seq):
    n_rows = seq // GRID_W
    row = np.repeat(np.arange(n_rows, dtype=np.float32), GRID_W)
    col = np.tile(np.arange(GRID_W, dtype=np.float32), n_rows)
    n_freq = QK // 4
    inv_freq = (np.float32(ROPE_BASE) ** (-np.arange(n_freq, dtype=np.float32) / n_freq)).astype(np.float32)
    ang = np.concatenate([row[:, None] * inv_freq, col[:, None] * inv_freq], axis=-1).astype(np.float32)
    cos = np.concatenate([np.ones((N_CTX, QK // 2)), np.cos(ang.astype(np.float64))], axis=0)
    sin = np.concatenate([np.zeros((N_CTX, QK // 2)), np.sin(ang.astype(np.float64))], axis=0)
    cos_t = np.tile(cos, (1, 4)).astype(np.float32)
    sin_t = np.tile(np.concatenate([-sin, sin], axis=1), (1, 2)).astype(np.float32)
    return jnp.asarray(cos_t), jnp.asarray(sin_t)


_MAIN_COLS = ((0, 2048), (3616, 4640), (2048, 3584), (4640, 6176))
_DT_COLS = (3584, 3616)


def _pad_lanes(v, n=128):
    v = v.reshape(1, -1)
    return jnp.pad(v, ((0, 0), (0, n - v.shape[1])))


def kernel(x, c, ctx, c_ctx, w_ada, b_ada, w_in, w_out, attn_q_norm, attn_k_norm,
           lambda_q1, lambda_k1, lambda_q2, lambda_k2, attn_subln,
           ssd_conv_w, ssd_conv_b, ssd_dt_bias, ssd_a_log, ssd_d, ssd_norm, ret_norm):
    bsz, seq, _ = x.shape
    depth = w_ada.shape[0]
    assert bsz + 1 <= 8 and ctx.shape[1] == N_CTX
    ctx_row = bsz
    xcat = jnp.concatenate([ctx, x], axis=1)
    cvec = jnp.zeros((8, D), F32).at[0:bsz].set(c).at[ctx_row].set(c_ctx)
    cos_t, sin_t = _rope_tables(seq)
    gsel = np.arange(512) // QK
    gmat = jnp.asarray((gsel[:, None] == gsel[None, :]).astype(np.float32), BF16)

    for layer in range(depth):
        last = layer == depth - 1
        lam_init = 0.8 - 0.6 * math.exp(-0.3 * layer)
        w_l = w_in[layer]
        w_main = jnp.concatenate([w_l[:, a:b] for a, b in _MAIN_COLS], axis=1).astype(BF16)
        w_dt = jnp.pad(w_l[:, _DT_COLS[0]:_DT_COLS[1]], ((0, 0), (0, 96))).astype(BF16)

        ada = _ada(cvec, w_ada[layer], b_ada[layer].reshape(1, -1))
        main, dtr = _inproj(xcat, ada, w_main, w_dt, ctx_row)

        qp, kp, vt = _attnprep(main, cos_t, sin_t,
                               jnp.tile(attn_q_norm[layer], 8).reshape(1, 512),
                               jnp.tile(attn_k_norm[layer], 8).reshape(1, 512), gmat, tm=ATT_TK)
        lam_vec = jnp.stack([lambda_q1[layer], lambda_k1[layer], lambda_q2[layer], lambda_k2[layer]])
        attn_o = _attn(qp, kp, vt, main, lam_vec, attn_subln[layer].reshape(128, 1), lam_init,
                       tq=N_CTX, tk=ATT_TK)

        u = _conv(main, ssd_conv_w[layer], ssd_conv_b[layer].reshape(1, -1), tm=768)
        bias = _pad_lanes(ssd_dt_bias[layer])
        alog = _pad_lanes(ssd_a_log[layer])
        fwd = _scan(u, dtr, main, cos_t, sin_t, bias, alog, backward=False)
        ssd_o, ret_o = _scan(u, dtr, main, cos_t, sin_t, bias, alog, backward=True, fwd=fwd,
                             dskip=jnp.repeat(ssd_d[layer], SSD_P).reshape(1, 1024),
                             ssd_gain=ssd_norm[layer].reshape(1, 1024),
                             ret_gain=ret_norm[layer].reshape(1, 128))

        xcat = _outproj(xcat, attn_o, ssd_o, ret_o, ada, w_out[layer].astype(BF16), ctx_row,
                        latent_only=last)
    return xcat
```

```python
import functools
import math

import numpy as np
import jax
import jax.numpy as jnp
from jax import lax
from jax.experimental import pallas as pl
from jax.experimental.pallas import tpu as pltpu

F32 = jnp.float32
BF16 = jnp.bfloat16

D = 1024
N_CTX = 256
GRID_W = 64
EPS = 1e-6
ROPE_BASE = 10000.0
CHUNK = 128
QK = 64
ATT_H = 4
SSD_H = 16
SSD_G = 2
SSD_HPG = 8
SSD_P = 64
SSD_N = 128
RET_H = 4
RET_EXP_F = (5.0, 6.0, 7.0, 8.0)
RET_EXP_B = (5.5, 6.5, 7.5, 8.5)
N_MAIN = 6144
LOG2E = math.log2(math.e)
VMEM_LIMIT = 56 * 1024 * 1024
ATT_TK = 768
VT_ROWS = 144
_DONE = object()


def _silu(v):
    return v * (0.5 + 0.5 * jnp.tanh(0.5 * v))


def _split3(v):
    hi = v.astype(BF16)
    r1 = v - hi.astype(F32)
    mid = r1.astype(BF16)
    lo = (r1 - mid.astype(F32)).astype(BF16)
    return hi, mid, lo


def _dot(a, b):
    return jnp.dot(a, b, preferred_element_type=F32)


def _dot_nt(a, b):
    return lax.dot_general(a, b, (((1,), (1,)), ((), ())), preferred_element_type=F32)


def _tri_dot(tri, v):
    hi, mid, lo = _split3(v)
    return _dot(tri, hi) + _dot(tri, mid) + _dot(tri, lo)


def _ada_kernel(c_ref, w_ref, b_ref, o_ref):
    o_ref[...] = _dot(_silu(c_ref[...]), w_ref[...]) + b_ref[...]


def _ada(cvec, w, b):
    n = w.shape[1]
    tn = 768
    return pl.pallas_call(
        _ada_kernel,
        grid=(n // tn,),
        in_specs=[pl.BlockSpec((8, D), lambda j: (0, 0)),
                  pl.BlockSpec((D, tn), lambda j: (0, j)),
                  pl.BlockSpec((1, tn), lambda j: (0, j))],
        out_specs=pl.BlockSpec((8, tn), lambda j: (0, j)),
        out_shape=jax.ShapeDtypeStruct((8, n), F32),
        name="ada",
    )(cvec, w, b)


def _row_mod(ada_ref, b, rows, lo, hi, ctx_row):
    vb = ada_ref[pl.ds(b, 1), lo:hi]
    vc = ada_ref[ctx_row:ctx_row + 1, lo:hi]
    return jnp.where(rows < N_CTX, vc, vb)


def _rope128(x, cos, sin, lo_half):
    partner = jnp.where(lo_half, pltpu.roll(x, 96, 1), pltpu.roll(x, 32, 1))
    return x * cos + partner * sin


C_Q, C_K, C_V, C_XBC = 0, 512, 1024, 3072
PLAIN_COLS = ((1536, 3072), (4608, 6144))


def _inproj_kernel(x_ref, xp_ref, xn_ref, ada_ref, w_ref, wdt_ref, cos_ref, sin_ref,
                   gq_ref, gk_ref, gmat_ref, cw_ref, cb_ref,
                   main_ref, dt_ref, vt_ref, *, tm, t, ctx_row):
    b = pl.program_id(0)
    i = pl.program_id(1)

    def modulated(x, rows):
        xn = x * lax.rsqrt(jnp.mean(x * x, axis=-1, keepdims=True) + EPS)
        shift = _row_mod(ada_ref, b, rows, 0, D, ctx_row)
        scale = _row_mod(ada_ref, b, rows, D, 2 * D, ctx_row)
        return (xn * (1.0 + scale) + shift).astype(BF16)

    loc = lax.broadcasted_iota(jnp.int32, (tm, 1), 0)
    row = i * tm + loc
    h = modulated(x_ref[...], row)

    halo = lax.broadcasted_iota(jnp.int32, (8, 1), 0)
    h_prev = modulated(xp_ref[...], i * tm - 8 + halo)
    h_next = modulated(xn_ref[...], (i + 1) * tm + halo)
    seq_start = (row == 0) | (row == N_CTX)
    seq_end = (row == t - 1) | (row == N_CTX - 1)

    plain = [c0 for lo, hi in PLAIN_COLS for c0 in range(lo, hi, 512)]

    def plain_chunk():
        if plain:
            c0 = plain.pop(0)
            main_ref[:, c0:c0 + 512] = _dot(h, w_ref[:, c0:c0 + 512]).astype(BF16)

    for c0 in range(C_XBC, C_XBC + 1536, 512):
        w_c = w_ref[:, c0:c0 + 512]
        xbc = _dot(h, w_c)
        prev = _dot(h_prev, w_c)[7:8, :]
        nxt = _dot(h_next, w_c)[0:1, :]
        plain_chunk()
        xm1 = jnp.where(loc == 0, prev, pltpu.roll(xbc, 1, 0))
        xm1 = jnp.where(seq_start, 0.0, xm1)
        xp1 = jnp.where(loc == tm - 1, nxt, pltpu.roll(xbc, tm - 1, 0))
        xp1 = jnp.where(seq_end, 0.0, xp1)
        cc = c0 - C_XBC
        cw = cw_ref[:, cc:cc + 512]
        y = cw[0:1] * xm1 + cw[1:2] * xbc + cw[2:3] * xp1 + cb_ref[:, cc:cc + 512]
        main_ref[:, c0:c0 + 512] = _silu(y).astype(BF16)

    cos = cos_ref[...]
    sin = sin_ref[...]
    lane = lax.broadcasted_iota(jnp.int32, (tm, 128), 1)
    lo_half = (lane % QK) < (QK // 2)
    for c0, gain_ref, out_scale in ((C_Q, gq_ref, (QK ** -0.5) * LOG2E), (C_K, gk_ref, 1.0)):
        y = _dot(h, w_ref[:, c0:c0 + 512])
        plain_chunk()
        ss = _dot((y * y).astype(BF16), gmat_ref[...])
        y = y * lax.rsqrt(ss * (1.0 / QK) + EPS) * gain_ref[...]
        for g in range(4):
            yg = _rope128(y[:, g * 128:(g + 1) * 128], cos, sin, lo_half)
            main_ref[:, c0 + g * 128:c0 + (g + 1) * 128] = (yg * out_scale).astype(BF16)

    v = _dot(h, w_ref[:, C_V:C_V + 512])
    plain_chunk()
    main_ref[:, C_V:C_V + 512] = v.astype(BF16)
    ones_rows = (lax.broadcasted_iota(jnp.int32, (VT_ROWS - 128, tm), 0) == 0).astype(BF16)
    for hd in range(ATT_H):
        vt_ref[hd, 0:128, :] = v[:, hd * 128:(hd + 1) * 128].T.astype(BF16)
        vt_ref[hd, 128:VT_ROWS, :] = ones_rows

    while plain:
        plain_chunk()
    dt_ref[...] = _dot(h, wdt_ref[...])


def _inproj(xcat, ada, w_main, w_dt, cos_t, sin_t, gq, gk, gmat, conv_w, conv_b, ctx_row):
    bsz, t, _ = xcat.shape
    tm = 384
    assert ATT_TK % tm == 0
    per_tile = ATT_TK // tm
    r8 = tm // 8
    last8 = t // 8 - 1
    const = lambda b, i: (0, 0)
    return pl.pallas_call(
        functools.partial(_inproj_kernel, tm=tm, t=t, ctx_row=ctx_row),
        grid=(bsz, t // tm),
        in_specs=[pl.BlockSpec((None, tm, D), lambda b, i: (b, i, 0)),
                  pl.BlockSpec((None, 8, D), lambda b, i: (b, jnp.maximum(i * r8 - 1, 0), 0)),
                  pl.BlockSpec((None, 8, D), lambda b, i: (b, jnp.minimum((i + 1) * r8, last8), 0)),
                  pl.BlockSpec((8, 3 * D), const),
                  pl.BlockSpec((D, N_MAIN), const),
                  pl.BlockSpec((D, 128), const),
                  pl.BlockSpec((tm, 128), lambda b, i: (i, 0)),
                  pl.BlockSpec((tm, 128), lambda b, i: (i, 0)),
                  pl.BlockSpec((1, 512), const),
                  pl.BlockSpec((1, 512), const),
                  pl.BlockSpec((512, 512), const),
                  pl.BlockSpec((3, 1536), const),
                  pl.BlockSpec((1, 1536), const)],
        out_specs=[pl.BlockSpec((None, tm, N_MAIN), lambda b, i: (b, i, 0)),
                   pl.BlockSpec((None, tm, 128), lambda b, i: (b, i, 0)),
                   pl.BlockSpec((None, ATT_H, None, VT_ROWS, tm),
                                lambda b, i: (b, 0, i // per_tile, 0, i % per_tile))],
        out_shape=[jax.ShapeDtypeStruct((bsz, t, N_MAIN), BF16),
                   jax.ShapeDtypeStruct((bsz, t, 128), F32),
                   jax.ShapeDtypeStruct((bsz, ATT_H, t // ATT_TK, VT_ROWS, ATT_TK), BF16)],
        compiler_params=pltpu.CompilerParams(
            dimension_semantics=("arbitrary", "arbitrary"), vmem_limit_bytes=VMEM_LIMIT),
        name="inproj",
    )(xcat, xcat, xcat, ada, w_main, w_dt, cos_t, sin_t, gq, gk, gmat, conv_w, conv_b)


def _attn_kernel(q_ref, qn_ref, k_ref, vt_ref, gate_ref, lam_ref, subln_ref, o_ref,
                 q2_ref, m_ref, acc_ref, s_ref, mt_ref, *, tq, tk, nkv, lam_init):
    qi = pl.program_id(2)
    lane = lax.broadcasted_iota(jnp.int32, (tq, 128), 1)
    m_ref[...] = jnp.full((1, 2 * tq), -jnp.inf, F32)
    acc_ref[...] = jnp.zeros((VT_ROWS, 2 * tq), F32)

    def stack_maps(src_ref):
        q = src_ref[...]
        zero = jnp.zeros_like(q)
        q2_ref[0:tq, :] = jnp.where(lane < QK, q, zero)
        q2_ref[tq:2 * tq, :] = jnp.where(lane >= QK, q, zero)

    def hand_off():
        stack_maps(qn_ref)
        scores(0, 2, False)

    def scores(j, slot, ctx_only):
        st = _dot_nt(k_ref[pl.ds(j * tk, tk), :], q2_ref[...])
        if ctx_only:
            krow = j * tk + lax.broadcasted_iota(jnp.int32, (tk, 1), 0)
            st = jnp.where(krow < N_CTX, st, -jnp.inf)
        s_ref[slot] = st
        mt_ref[slot] = jnp.max(st, axis=0, keepdims=True)

    def consume(j, slot):
        m_old = m_ref[...]
        m_new = jnp.maximum(m_old, mt_ref[slot])
        alpha = jnp.exp2(m_old - m_new)
        p = jnp.exp2(s_ref[slot] - m_new)
        acc_ref[...] = alpha * acc_ref[...] + _dot(vt_ref[j], p.astype(BF16))
        m_ref[...] = m_new

    def finalize():
        lam_v = lam_ref[...]
        lam = (jnp.exp(jnp.sum(lam_v[0:1] * lam_v[1:2], axis=-1, keepdims=True))
               - jnp.exp(jnp.sum(lam_v[2:3] * lam_v[3:4], axis=-1, keepdims=True)) + lam_init)
        inv = 1.0 / acc_ref[128:129, :]
        acc = acc_ref[0:128, :]
        o = acc[:, 0:tq] * inv[:, 0:tq] - lam * (acc[:, tq:2 * tq] * inv[:, tq:2 * tq])
        o = o * lax.rsqrt(jnp.mean(o * o, axis=0, keepdims=True) + EPS)
        o = o * subln_ref[...] * (1.0 - lam_init)
        o_ref[...] = (o.T * _silu(gate_ref[...].astype(F32))).astype(BF16)

    @pl.when(qi == 0)
    def _():
        stack_maps(q_ref)
        scores(0, 0, True)
        hand_off()
        consume(0, 0)
        finalize()

    @pl.when(qi != 0)
    def _():
        for j in range(nkv):
            if j + 1 < nkv:
                scores(j + 1, (j + 1) % 2, False)
            else:
                hand_off()
            consume(j, 2 if j == 0 else j % 2)
        finalize()


def _attn(main, vt, lam_vec, subln, lam_init, tq, tk):
    bsz, t, _ = main.shape
    nkv = t // tk
    assert tq == N_CTX and tk >= N_CTX and nkv >= 2
    nq = t // tq
    return pl.pallas_call(
        functools.partial(_attn_kernel, tq=tq, tk=tk, nkv=nkv, lam_init=lam_init),
        grid=(bsz, ATT_H, nq),
        in_specs=[pl.BlockSpec((None, tq, 128), lambda b, h, i: (b, i, h)),
                  pl.BlockSpec((None, tq, 128), lambda b, h, i: (b, jnp.minimum(i + 1, nq - 1), h)),
                  pl.BlockSpec((None, t, 128), lambda b, h, i: (b, 0, C_K // 128 + h)),
                  pl.BlockSpec((None, None, nkv, VT_ROWS, tk), lambda b, h, i: (b, h, 0, 0, 0)),
                  pl.BlockSpec((None, tq, 128), lambda b, h, i: (b, i, 12 + h)),
                  pl.BlockSpec((4, QK), lambda b, h, i: (0, 0)),
                  pl.BlockSpec((128, 1), lambda b, h, i: (0, 0))],
        out_specs=pl.BlockSpec((None, tq, 128), lambda b, h, i: (b, i, h)),
        out_shape=jax.ShapeDtypeStruct((bsz, t, 512), BF16),
        scratch_shapes=[pltpu.VMEM((2 * tq, 128), BF16),
                        pltpu.VMEM((1, 2 * tq), F32),
                        pltpu.VMEM((VT_ROWS, 2 * tq), F32),
                        pltpu.VMEM((3, tk, 2 * tq), F32),
                        pltpu.VMEM((3, 1, 2 * tq), F32)],
        compiler_params=pltpu.CompilerParams(
            dimension_semantics=("arbitrary", "arbitrary", "arbitrary"),
            vmem_limit_bytes=VMEM_LIMIT),
        name="attn",
    )(main, main, main, vt, main, lam_vec, subln)


def _chunk_index(s, nchunks, backward):
    if not backward:
        return s
    nctx = N_CTX // CHUNK
    return jnp.where(s < nctx, nctx - 1 - s, nchunks - 1 + nctx - s)


def _tri_mask(backward):
    r = lax.broadcasted_iota(jnp.int32, (CHUNK, CHUNK), 0)
    c = lax.broadcasted_iota(jnp.int32, (CHUNK, CHUNK), 1)
    return (c >= r) if backward else (c <= r)


def _ssd_chunk(u_ref, dtr_ref, bias_ref, alog_ref, o_ref, s_ref, fin, backward):
    mask = _tri_mask(backward)
    tri = jnp.where(mask, 1.0, 0.0).astype(BF16)
    pre = dtr_ref[...] + bias_ref[...]
    dt = jnp.maximum(pre, 0.0) + jnp.log1p(jnp.exp(-jnp.abs(pre)))
    la = dt * (-LOG2E * jnp.exp(alog_ref[...]))
    cum = _tri_dot(tri, la)
    cum_t = cum.T
    dt_t = dt.T
    edge = 0 if backward else CHUNK - 1
    etot = jnp.exp2(cum[edge:edge + 1, :])
    wdt_t = jnp.exp2(cum_t[:, edge:edge + 1] - cum_t) * dt_t
    first = lax.broadcasted_iota(jnp.int32, (CHUNK, 128), 1) < SSD_P
    first_row = lax.broadcasted_iota(jnp.int32, (1, 128), 1) < SSD_P
    col0 = SSD_H if backward else 0
    yield

    for g in range(SSD_G):
        k = u_ref[:, 1024 + g * SSD_N:1024 + (g + 1) * SSD_N]
        q = u_ref[:, 1280 + g * SSD_N:1280 + (g + 1) * SSD_N]
        scores = _dot_nt(q, k)
        k_t = k.astype(F32).T
        y_inter = _dot(q, s_ref[g].astype(BF16))
        ys = []
        yield
        for pp in range(SSD_HPG // 2):
            h0 = g * SSD_HPG + 2 * pp
            c0 = col0 + h0
            off = h0 * SSD_P
            xs = u_ref[:, off:off + 128]
            zero = jnp.zeros_like(xs)
            vals = jnp.concatenate([jnp.where(first, xs, zero),
                                    jnp.where(first, zero, xs)], axis=0)
            wts, kws, ecs = [], [], []
            for hh in range(2):
                c = c0 + hh
                cum_c = jnp.broadcast_to(cum[:, c:c + 1], (CHUNK, CHUNK))
                dec = jnp.exp2(jnp.where(mask, cum_c - cum_t[c:c + 1, :], -jnp.inf))
                wts.append((scores * dec * dt_t[c:c + 1, :]).astype(BF16))
                kws.append((k_t * wdt_t[c:c + 1, :]).astype(BF16))
                ecs.append(jnp.exp2(cum_c))
            y = (jnp.where(first, ecs[0], ecs[1]) * y_inter[:, pp * 128:(pp + 1) * 128]
                 + _dot(jnp.concatenate(wts, axis=1), vals))
            etot_p = jnp.where(first_row, jnp.broadcast_to(etot[:, c0:c0 + 1], (1, 128)),
                               jnp.broadcast_to(etot[:, c0 + 1:c0 + 2], (1, 128)))
            s_old = s_ref[g, :, pp * 128:(pp + 1) * 128]
            s_ref[g, :, pp * 128:(pp + 1) * 128] = (
                etot_p * s_old + _dot(jnp.concatenate(kws, axis=1), vals))
            if backward:
                z_ref, yf_ref, dskip_ref, _ = fin
                y = y + yf_ref[:, off:off + 128] + xs.astype(F32) * dskip_ref[:, off:off + 128]
                y = y * _silu(z_ref[:, off:off + 128].astype(F32))
                ys.append(y)
            else:
                o_ref[:, off:off + 128] = y
            yield
        if backward:
            gain_ref = fin[3]
            lo, hi = g * 512, (g + 1) * 512
            yg = jnp.concatenate(ys, axis=1)
            yg = yg * lax.rsqrt(jnp.mean(yg * yg, axis=-1, keepdims=True) + EPS)
            o_ref[:, lo:hi] = (yg * gain_ref[:, lo:hi]).astype(o_ref.dtype)


def _ret_tables(backward):
    exps = RET_EXP_B if backward else RET_EXP_F
    lg = np.log1p(-np.exp2(-np.asarray(exps, np.float64)))
    i = np.arange(CHUNK, dtype=np.float64)
    if backward:
        cum = (CHUNK - i)[None, :] * lg[:, None]
        tot = cum[:, 0]
        msk = i[None, :] >= i[:, None]
    else:
        cum = (i + 1.0)[None, :] * lg[:, None]
        tot = cum[:, -1]
        msk = i[None, :] <= i[:, None]
    dec = np.where(msk[None], np.exp(cum[:, :, None] - cum[:, None, :]), 0.0)
    inter = np.broadcast_to(np.exp(cum)[:, :, None], (RET_H, CHUNK, 128))
    toend = np.broadcast_to(np.exp(tot[:, None] - cum)[:, :, None], (RET_H, CHUNK, 128))
    tables = np.stack([dec, inter, toend], axis=1).astype(np.float32)
    return tables, [float(np.exp(v)) for v in tot]


def _ret_chunk(qk_ref, v_ref, cos, sin, tab_ref, o_ref, s_ref, fin, backward, etot):
    lane = lax.broadcasted_iota(jnp.int32, (CHUNK, 128), 1)
    lo_half = (lane % QK) < (QK // 2)
    first = lane < QK
    srow = lax.broadcasted_iota(jnp.int32, (128, 256), 0) < QK
    scol = lax.broadcasted_iota(jnp.int32, (128, 256), 1) < 128
    diag = srow == scol
    for pp in range(RET_H // 2):
        h0 = 2 * pp
        q = _rope128(qk_ref[:, pp * 128:(pp + 1) * 128].astype(F32), cos, sin, lo_half)
        k = _rope128(qk_ref[:, 256 + pp * 128:256 + (pp + 1) * 128].astype(F32) * (QK ** -0.5),
                     cos, sin, lo_half)
        qb = q.astype(BF16)
        kb = k.astype(BF16)
        k_t = kb.astype(F32).T.astype(BF16)
        zero = jnp.zeros_like(kb)
        k2 = jnp.concatenate([jnp.where(first, kb, zero),
                              jnp.where(first, zero, kb)], axis=0)
        scores = _dot_nt(qb, k2)
        v2 = v_ref[:, h0 * 128:(h0 + 2) * 128]
        zv = jnp.zeros((CHUNK, 128), BF16)
        vdiag = jnp.concatenate(
            [jnp.concatenate([v2[:, 0:128], zv], axis=1),
             jnp.concatenate([zv, v2[:, 128:256]], axis=1)], axis=0)
        dec = jnp.concatenate([tab_ref[h0, 0], tab_ref[h0 + 1, 0]], axis=1)
        inter = jnp.concatenate([tab_ref[h0, 1], tab_ref[h0 + 1, 1]], axis=1)
        toend = jnp.concatenate([tab_ref[h0, 2], tab_ref[h0 + 1, 2]], axis=1)
        s_pair = s_ref[pp]
        y = _dot((scores * dec).astype(BF16), vdiag) + inter * _dot(qb, s_pair.astype(BF16))
        upd = _dot(k_t, (v2.astype(F32) * toend).astype(BF16))
        s_ref[pp] = (jnp.where(scol, etot[h0], etot[h0 + 1]) * s_pair
                     + jnp.where(diag, upd, 0.0))
        for hh in range(2):
            h = h0 + hh
            yh = y[:, hh * 128:(hh + 1) * 128]
            if backward:
                g_ref, yf_ref, gain_ref = fin
                yh = yh + yf_ref[:, h * 128:(h + 1) * 128]
                yh = yh * lax.rsqrt(jnp.mean(yh * yh, axis=-1, keepdims=True) + EPS) * gain_ref[...]
                yh = yh * _silu(g_ref[:, h * 128:(h + 1) * 128].astype(F32))
            o_ref[:, h * 128:(h + 1) * 128] = yh.astype(o_ref.dtype)
        yield


def _scan_kernel(*refs, backward, etot, nb):
    if backward:
        (u_ref, dtr_ref, bias_ref, alog_ref, qk_ref, v_ref, cos_ref, sin_ref, tab_ref,
         z_ref, ssd_f_ref, dskip_ref, ssd_gain_ref, g_ref, ret_f_ref, ret_gain_ref,
         ssd_o_ref, ret_o_ref, ssd_s_ref, ret_s_ref) = refs
    else:
        (u_ref, dtr_ref, bias_ref, alog_ref, qk_ref, v_ref, cos_ref, sin_ref, tab_ref,
         ssd_o_ref, ret_o_ref, ssd_s_ref, ret_s_ref) = refs

    @pl.when(pl.program_id(0) == 0)
    def _():
        ssd_s_ref[...] = jnp.zeros_like(ssd_s_ref)
        ret_s_ref[...] = jnp.zeros_like(ret_s_ref)

    cos = cos_ref[...]
    sin = sin_ref[...]
    chains = []
    for b in range(nb):
        ssd_fin = (z_ref.at[b], ssd_f_ref.at[b], dskip_ref, ssd_gain_ref) if backward else None
        ret_fin = (g_ref.at[b], ret_f_ref.at[b], ret_gain_ref) if backward else None
        chains.append(_ssd_chunk(u_ref.at[b], dtr_ref.at[b], bias_ref, alog_ref,
                                 ssd_o_ref.at[b], ssd_s_ref.at[b], ssd_fin, backward))
        chains.append(_ret_chunk(qk_ref.at[b], v_ref.at[b], cos, sin, tab_ref,
                                 ret_o_ref.at[b], ret_s_ref.at[b], ret_fin, backward, etot))
    while chains:
        chains = [c for c in chains if next(c, _DONE) is not _DONE]


def _scan(dtr, main, cos_t, sin_t, bias, alog, backward, fwd=None, dskip=None,
          ssd_gain=None, ret_gain=None):
    bsz, t, _ = main.shape
    nchunks = t // CHUNK
    tables, etot = _ret_tables(backward)
    idx = functools.partial(_chunk_index, nchunks=nchunks, backward=backward)
    row = lambda s: (0, idx(s), 0)
    const = lambda s: (0, 0)
    in_specs = [pl.BlockSpec((bsz, CHUNK, 1536), lambda s: (0, idx(s), C_XBC // 1536)),
                pl.BlockSpec((bsz, CHUNK, 128), row),
                pl.BlockSpec((1, 128), const),
                pl.BlockSpec((1, 128), const),
                pl.BlockSpec((bsz, CHUNK, 512), lambda s: (0, idx(s), 9)),
                pl.BlockSpec((bsz, CHUNK, 512), lambda s: (0, idx(s), 10)),
                pl.BlockSpec((CHUNK, 128), lambda s: (idx(s), 0)),
                pl.BlockSpec((CHUNK, 128), lambda s: (idx(s), 0)),
                pl.BlockSpec((RET_H, 3, CHUNK, 128), lambda s: (0, 0, 0, 0))]
    args = [main, dtr, bias, alog, main, main, cos_t, sin_t, jnp.asarray(tables)]
    if backward:
        ssd_f, ret_f = fwd
        in_specs += [pl.BlockSpec((bsz, CHUNK, 1024), lambda s: (0, idx(s), 2)),
                     pl.BlockSpec((bsz, CHUNK, 1024), row),
                     pl.BlockSpec((1, 1024), const),
                     pl.BlockSpec((1, 1024), const),
                     pl.BlockSpec((bsz, CHUNK, 512), lambda s: (0, idx(s), 11)),
                     pl.BlockSpec((bsz, CHUNK, 512), row),
                     pl.BlockSpec((1, 128), const)]
        args += [main, ssd_f, dskip, ssd_gain, main, ret_f, ret_gain]
    out_dtype = BF16 if backward else F32
    return pl.pallas_call(
        functools.partial(_scan_kernel, backward=backward, etot=etot, nb=bsz),
        grid=(nchunks,),
        in_specs=in_specs,
        out_specs=[pl.BlockSpec((bsz, CHUNK, 1024), row),
                   pl.BlockSpec((bsz, CHUNK, 512), row)],
        out_shape=[jax.ShapeDtypeStruct((bsz, t, 1024), out_dtype),
                   jax.ShapeDtypeStruct((bsz, t, 512), out_dtype)],
        scratch_shapes=[pltpu.VMEM((bsz, SSD_G, SSD_N, 512), F32),
                        pltpu.VMEM((bsz, RET_H // 2, 128, 256), F32)],
        compiler_params=pltpu.CompilerParams(
            dimension_semantics=("arbitrary",), vmem_limit_bytes=VMEM_LIMIT),
        name="scan_bwd" if backward else "scan_fwd",
    )(*args)


def _outproj_kernel(x_ref, a_ref, s_ref, r_ref, ada_ref, w_ref, o_ref, *, tm, blk0, ctx_row):
    b = pl.program_id(0)
    i = pl.program_id(1) + blk0
    acc = _dot(a_ref[...], w_ref[0:512, :])
    acc += _dot(s_ref[...], w_ref[512:1536, :])
    acc += _dot(r_ref[...], w_ref[1536:2048, :])
    rows = i * tm + lax.broadcasted_iota(jnp.int32, (tm, 1), 0)
    gate = _row_mod(ada_ref, b, rows, 2 * D, 3 * D, ctx_row)
    o_ref[...] = x_ref[...] + gate * acc


def _outproj(xcat, attn_o, ssd_o, ret_o, ada, w_out, ctx_row, latent_only):
    bsz, t, _ = xcat.shape
    tm = N_CTX if latent_only else 768
    blk0 = 1 if latent_only else 0
    nb = t // tm - blk0
    row = lambda b, i: (b, i + blk0, 0)
    return pl.pallas_call(
        functools.partial(_outproj_kernel, tm=tm, blk0=blk0, ctx_row=ctx_row),
        grid=(bsz, nb),
        in_specs=[pl.BlockSpec((None, tm, D), row),
                  pl.BlockSpec((None, tm, 512), row),
                  pl.BlockSpec((None, tm, 1024), row),
                  pl.BlockSpec((None, tm, 512), row),
                  pl.BlockSpec((8, 3 * D), lambda b, i: (0, 0)),
                  pl.BlockSpec((2 * D, D), lambda b, i: (0, 0))],
        out_specs=pl.BlockSpec((None, tm, D), lambda b, i: (b, i, 0)),
        out_shape=jax.ShapeDtypeStruct((bsz, nb * tm, D), F32),
        compiler_params=pltpu.CompilerParams(
            dimension_semantics=("arbitrary", "arbitrary"), vmem_limit_bytes=VMEM_LIMIT),
        name="outproj",
    )(xcat, attn_o, ssd_o, ret_o, ada, w_out)


def _rope_tables(seq):
    n_rows = seq // GRID_W
    row = np.repeat(np.arange(n_rows, dtype=np.float32), GRID_W)
    col = np.tile(np.arange(GRID_W, dtype=np.float32), n_rows)
    n_freq = QK // 4
    inv_freq = (np.float32(ROPE_BASE) ** (-np.arange(n_freq, dtype=np.float32) / n_freq)).astype(np.float32)
    ang = np.concatenate([row[:, None] * inv_freq, col[:, None] * inv_freq], axis=-1).astype(np.float32)
    cos = np.concatenate([np.ones((N_CTX, QK // 2)), np.cos(ang.astype(np.float64))], axis=0)
    sin = np.concatenate([np.zeros((N_CTX, QK // 2)), np.sin(ang.astype(np.float64))], axis=0)
    cos_t = np.tile(cos, (1, 4)).astype(np.float32)
    sin_t = np.tile(np.concatenate([-sin, sin], axis=1), (1, 2)).astype(np.float32)
    return jnp.asarray(cos_t), jnp.asarray(sin_t)


_MAIN_COLS = ((0, 2048), (3616, 4640), (2048, 3584), (4640, 6176))
_DT_COLS = (3584, 3616)


def _pad_lanes(v, n=128):
    v = v.reshape(1, -1)
    return jnp.pad(v, ((0, 0), (0, n - v.shape[1])))


def kernel(x, c, ctx, c_ctx, w_ada, b_ada, w_in, w_out, attn_q_norm, attn_k_norm,
           lambda_q1, lambda_k1, lambda_q2, lambda_k2, attn_subln,
           ssd_conv_w, ssd_conv_b, ssd_dt_bias, ssd_a_log, ssd_d, ssd_norm, ret_norm):
    bsz, seq, _ = x.shape
    depth = w_ada.shape[0]
    assert bsz + 1 <= 8 and ctx.shape[1] == N_CTX
    ctx_row = bsz
    xcat = jnp.concatenate([ctx, x], axis=1)
    cvec = jnp.zeros((8, D), F32).at[0:bsz].set(c).at[ctx_row].set(c_ctx)
    cos_t, sin_t = _rope_tables(seq)
    gsel = np.arange(512) // QK
    gmat = jnp.asarray((gsel[:, None] == gsel[None, :]).astype(np.float32), BF16)

    for layer in range(depth):
        last = layer == depth - 1
        lam_init = 0.8 - 0.6 * math.exp(-0.3 * layer)
        w_l = w_in[layer]
        w_main = jnp.concatenate([w_l[:, a:b] for a, b in _MAIN_COLS], axis=1).astype(BF16)
        w_dt = jnp.pad(w_l[:, _DT_COLS[0]:_DT_COLS[1]], ((0, 0), (0, 96))).astype(BF16)

        ada = _ada(cvec, w_ada[layer], b_ada[layer].reshape(1, -1))
        main, dtr, vt = _inproj(xcat, ada, w_main, w_dt, cos_t, sin_t,
                                jnp.tile(attn_q_norm[layer], 8).reshape(1, 512),
                                jnp.tile(attn_k_norm[layer], 8).reshape(1, 512), gmat,
                                ssd_conv_w[layer], ssd_conv_b[layer].reshape(1, -1), ctx_row)

        lam_vec = jnp.stack([lambda_q1[layer], lambda_k1[layer], lambda_q2[layer], lambda_k2[layer]])
        attn_o = _attn(main, vt, lam_vec, attn_subln[layer].reshape(128, 1), lam_init,
                       tq=N_CTX, tk=ATT_TK)

        bias = _pad_lanes(ssd_dt_bias[layer])
        alog = _pad_lanes(ssd_a_log[layer])
        fwd = _scan(dtr, main, cos_t, sin_t, bias, alog, backward=False)
        ssd_o, ret_o = _scan(dtr, main, cos_t, sin_t, bias, alog, backward=True, fwd=fwd,
                             dskip=jnp.repeat(ssd_d[layer], SSD_P).reshape(1, 1024),
                             ssd_gain=ssd_norm[layer].reshape(1, 1024),
                             ret_gain=ret_norm[layer].reshape(1, 128))

        xcat = _outproj(xcat, attn_o, ssd_o, ret_o, ada, w_out[layer].astype(BF16), ctx_row,
                        latent_only=last)
    return xcat
```

```python
import functools
import math

import numpy as np
import jax
import jax.numpy as jnp
from jax import lax
from jax.experimental import pallas as pl
from jax.experimental.pallas import tpu as pltpu

F32 = jnp.float32
BF16 = jnp.bfloat16

D = 1024
N_CTX = 256
GRID_W = 64
EPS = 1e-6
ROPE_BASE = 10000.0
CHUNK = 128
QK = 64
ATT_H = 4
SSD_H = 16
SSD_G = 2
SSD_HPG = 8
SSD_P = 64
SSD_N = 128
RET_H = 4
RET_EXP_F = (5.0, 6.0, 7.0, 8.0)
RET_EXP_B = (5.5, 6.5, 7.5, 8.5)
N_MAIN = 6144
LOG2E = math.log2(math.e)
VMEM_LIMIT = 56 * 1024 * 1024
ATT_TK = 768
VT_ROWS = 144
_DONE = object()


def _silu(v):
    return v * (0.5 + 0.5 * jnp.tanh(0.5 * v))


def _split3(v):
    hi = v.astype(BF16)
    r1 = v - hi.astype(F32)
    mid = r1.astype(BF16)
    lo = (r1 - mid.astype(F32)).astype(BF16)
    return hi, mid, lo


def _dot(a, b):
    return jnp.dot(a, b, preferred_element_type=F32)


def _dot_nt(a, b):
    return lax.dot_general(a, b, (((1,), (1,)), ((), ())), preferred_element_type=F32)


def _tri_dot(tri, v):
    hi, mid, lo = _split3(v)
    return _dot(tri, hi) + _dot(tri, mid) + _dot(tri, lo)


def _ada_kernel(c_ref, w_ref, b_ref, o_ref):
    o_ref[...] = _dot(_silu(c_ref[...]), w_ref[...]) + b_ref[...]


def _ada(cvec, w, b):
    n = w.shape[1]
    tn = 768
    return pl.pallas_call(
        _ada_kernel,
        grid=(n // tn,),
        in_specs=[pl.BlockSpec((8, D), lambda j: (0, 0)),
                  pl.BlockSpec((D, tn), lambda j: (0, j)),
                  pl.BlockSpec((1, tn), lambda j: (0, j))],
        out_specs=pl.BlockSpec((8, tn), lambda j: (0, j)),
        out_shape=jax.ShapeDtypeStruct((8, n), F32),
        name="ada",
    )(cvec, w, b)


def _row_mod(ada_ref, b, rows, lo, hi, ctx_row):
    vb = ada_ref[pl.ds(b, 1), lo:hi]
    vc = ada_ref[ctx_row:ctx_row + 1, lo:hi]
    return jnp.where(rows < N_CTX, vc, vb)


def _rope128(x, cos, sin, lo_half):
    partner = jnp.where(lo_half, pltpu.roll(x, 96, 1), pltpu.roll(x, 32, 1))
    return x * cos + partner * sin


C_Q, C_K, C_V, C_XBC = 0, 512, 1024, 3072
PLAIN_COLS = ((1536, 3072), (4608, 6144))


def _inproj_kernel(x_ref, xp_ref, xn_ref, ada_ref, w_ref, cos_ref, sin_ref,
                   gq_ref, gk_ref, gmat_ref, cw_ref, cb_ref,
                   main_ref, dt_ref, vt_ref, *, tm, t, ctx_row):
    b = pl.program_id(0)
    i = pl.program_id(1)

    def modulated(x, rows):
        xn = x * lax.rsqrt(jnp.mean(x * x, axis=-1, keepdims=True) + EPS)
        shift = _row_mod(ada_ref, b, rows, 0, D, ctx_row)
        scale = _row_mod(ada_ref, b, rows, D, 2 * D, ctx_row)
        return (xn * (1.0 + scale) + shift).astype(BF16)

    loc = lax.broadcasted_iota(jnp.int32, (tm, 1), 0)
    row = i * tm + loc
    h = modulated(x_ref[...], row)

    halo = lax.broadcasted_iota(jnp.int32, (8, 1), 0)
    h_prev = modulated(xp_ref[...], i * tm - 8 + halo)
    h_next = modulated(xn_ref[...], (i + 1) * tm + halo)
    seq_start = (row == 0) | (row == N_CTX)
    seq_end = (row == t - 1) | (row == N_CTX - 1)

    plain = [c0 for lo, hi in PLAIN_COLS for c0 in range(lo, hi, 512)]

    def plain_chunk():
        if plain:
            c0 = plain.pop(0)
            main_ref[:, c0:c0 + 512] = _dot(h, w_ref[:, c0:c0 + 512]).astype(BF16)

    for c0 in range(C_XBC, C_XBC + 1536, 512):
        w_c = w_ref[:, c0:c0 + 512]
        xbc = _dot(h, w_c)
        prev = _dot(h_prev, w_c)[7:8, :]
        nxt = _dot(h_next, w_c)[0:1, :]
        plain_chunk()
        xm1 = jnp.where(loc == 0, prev, pltpu.roll(xbc, 1, 0))
        xm1 = jnp.where(seq_start, 0.0, xm1)
        xp1 = jnp.where(loc == tm - 1, nxt, pltpu.roll(xbc, tm - 1, 0))
        xp1 = jnp.where(seq_end, 0.0, xp1)
        cc = c0 - C_XBC
        cw = cw_ref[:, cc:cc + 512]
        y = cw[0:1] * xm1 + cw[1:2] * xbc + cw[2:3] * xp1 + cb_ref[:, cc:cc + 512]
        main_ref[:, c0:c0 + 512] = _silu(y).astype(BF16)

    cos = cos_ref[...]
    sin = sin_ref[...]
    lane = lax.broadcasted_iota(jnp.int32, (tm, 128), 1)
    lo_half = (lane % QK) < (QK // 2)
    for c0, gain_ref, out_scale in ((C_Q, gq_ref, (QK ** -0.5) * LOG2E), (C_K, gk_ref, 1.0)):
        y = _dot(h, w_ref[:, c0:c0 + 512])
        plain_chunk()
        ss = _dot((y * y).astype(BF16), gmat_ref[...])
        y = y * lax.rsqrt(ss * (1.0 / QK) + EPS) * gain_ref[...]
        for g in range(4):
            yg = _rope128(y[:, g * 128:(g + 1) * 128], cos, sin, lo_half)
            main_ref[:, c0 + g * 128:c0 + (g + 1) * 128] = (yg * out_scale).astype(BF16)

    v = _dot(h, w_ref[:, C_V:C_V + 512])
    plain_chunk()
    main_ref[:, C_V:C_V + 512] = v.astype(BF16)
    ones_rows = (lax.broadcasted_iota(jnp.int32, (VT_ROWS - 128, tm), 0) == 0).astype(BF16)
    for hd in range(ATT_H):
        vt_ref[hd, 0:128, :] = v[:, hd * 128:(hd + 1) * 128].T.astype(BF16)
        vt_ref[hd, 128:VT_ROWS, :] = ones_rows

    while plain:
        plain_chunk()
    dt_ref[...] = _dot(h, w_ref[:, N_MAIN:N_MAIN + 128])


def _inproj(xcat, ada, w_all, cos_t, sin_t, gq, gk, gmat, conv_w, conv_b, ctx_row):
    bsz, t, _ = xcat.shape
    tm = 384
    assert ATT_TK % tm == 0
    per_tile = ATT_TK // tm
    r8 = tm // 8
    last8 = t // 8 - 1
    const = lambda b, i: (0, 0)
    return pl.pallas_call(
        functools.partial(_inproj_kernel, tm=tm, t=t, ctx_row=ctx_row),
        grid=(bsz, t // tm),
        in_specs=[pl.BlockSpec((None, tm, D), lambda b, i: (b, i, 0)),
                  pl.BlockSpec((None, 8, D), lambda b, i: (b, jnp.maximum(i * r8 - 1, 0), 0)),
                  pl.BlockSpec((None, 8, D), lambda b, i: (b, jnp.minimum((i + 1) * r8, last8), 0)),
                  pl.BlockSpec((8, 3 * D), const),
                  pl.BlockSpec((D, N_MAIN + 128), const),
                  pl.BlockSpec((tm, 128), lambda b, i: (i, 0)),
                  pl.BlockSpec((tm, 128), lambda b, i: (i, 0)),
                  pl.BlockSpec((1, 512), const),
                  pl.BlockSpec((1, 512), const),
                  pl.BlockSpec((512, 512), const),
                  pl.BlockSpec((3, 1536), const),
                  pl.BlockSpec((1, 1536), const)],
        out_specs=[pl.BlockSpec((None, tm, N_MAIN), lambda b, i: (b, i, 0)),
                   pl.BlockSpec((None, tm, 128), lambda b, i: (b, i, 0)),
                   pl.BlockSpec((None, ATT_H, None, VT_ROWS, tm),
                                lambda b, i: (b, 0, i // per_tile, 0, i % per_tile))],
        out_shape=[jax.ShapeDtypeStruct((bsz, t, N_MAIN), BF16),
                   jax.ShapeDtypeStruct((bsz, t, 128), F32),
                   jax.ShapeDtypeStruct((bsz, ATT_H, t // ATT_TK, VT_ROWS, ATT_TK), BF16)],
        compiler_params=pltpu.CompilerParams(
            dimension_semantics=("arbitrary", "arbitrary"), vmem_limit_bytes=VMEM_LIMIT),
        name="inproj",
    )(xcat, xcat, xcat, ada, w_all, cos_t, sin_t, gq, gk, gmat, conv_w, conv_b)


def _attn_kernel(q_ref, qn_ref, k_ref, vt_ref, gate_ref, lam_ref, subln_ref, o_ref,
                 q2_ref, m_ref, acc_ref, s_ref, mt_ref, *, tq, tk, nkv, lam_init):
    qi = pl.program_id(2)
    lane = lax.broadcasted_iota(jnp.int32, (tq, 128), 1)
    m_ref[...] = jnp.full((1, 2 * tq), -jnp.inf, F32)
    acc_ref[...] = jnp.zeros((VT_ROWS, 2 * tq), F32)

    def stack_maps(src_ref):
        q = src_ref[...]
        zero = jnp.zeros_like(q)
        q2_ref[0:tq, :] = jnp.where(lane < QK, q, zero)
        q2_ref[tq:2 * tq, :] = jnp.where(lane >= QK, q, zero)

    def hand_off():
        stack_maps(qn_ref)
        scores(0, 2, False)

    def scores(j, slot, ctx_only):
        st = _dot_nt(k_ref[pl.ds(j * tk, tk), :], q2_ref[...])
        if ctx_only:
            krow = j * tk + lax.broadcasted_iota(jnp.int32, (tk, 1), 0)
            st = jnp.where(krow < N_CTX, st, -jnp.inf)
        s_ref[slot] = st
        mt_ref[slot] = jnp.max(st, axis=0, keepdims=True)

    def consume(j, slot):
        m_old = m_ref[...]
        m_new = jnp.maximum(m_old, mt_ref[slot])
        alpha = jnp.exp2(m_old - m_new)
        p = jnp.exp2(s_ref[slot] - m_new)
        acc_ref[...] = alpha * acc_ref[...] + _dot(vt_ref[j], p.astype(BF16))
        m_ref[...] = m_new

    def finalize():
        lam_v = lam_ref[...]
        lam = (jnp.exp(jnp.sum(lam_v[0:1] * lam_v[1:2], axis=-1, keepdims=True))
               - jnp.exp(jnp.sum(lam_v[2:3] * lam_v[3:4], axis=-1, keepdims=True)) + lam_init)
        inv = 1.0 / acc_ref[128:129, :]
        acc = acc_ref[0:128, :]
        o = acc[:, 0:tq] * inv[:, 0:tq] - lam * (acc[:, tq:2 * tq] * inv[:, tq:2 * tq])
        o = o * lax.rsqrt(jnp.mean(o * o, axis=0, keepdims=True) + EPS)
        o = o * subln_ref[...] * (1.0 - lam_init)
        o_ref[...] = (o.T * _silu(gate_ref[...].astype(F32))).astype(BF16)

    @pl.when(qi == 0)
    def _():
        stack_maps(q_ref)
        scores(0, 0, True)
        hand_off()
        consume(0, 0)
        finalize()

    @pl.when(qi != 0)
    def _():
        for j in range(nkv):
            if j + 1 < nkv:
                scores(j + 1, (j + 1) % 2, False)
            else:
                hand_off()
            consume(j, 2 if j == 0 else j % 2)
        finalize()


def _attn(main, vt, lam_vec, subln, lam_init, tq, tk):
    bsz, t, _ = main.shape
    nkv = t // tk
    assert tq == N_CTX and tk >= N_CTX and nkv >= 2
    nq = t // tq
    return pl.pallas_call(
        functools.partial(_attn_kernel, tq=tq, tk=tk, nkv=nkv, lam_init=lam_init),
        grid=(bsz, ATT_H, nq),
        in_specs=[pl.BlockSpec((None, tq, 128), lambda b, h, i: (b, i, h)),
                  pl.BlockSpec((None, tq, 128), lambda b, h, i: (b, jnp.minimum(i + 1, nq - 1), h)),
                  pl.BlockSpec((None, t, 128), lambda b, h, i: (b, 0, C_K // 128 + h)),
                  pl.BlockSpec((None, None, nkv, VT_ROWS, tk), lambda b, h, i: (b, h, 0, 0, 0)),
                  pl.BlockSpec((None, tq, 128), lambda b, h, i: (b, i, 12 + h)),
                  pl.BlockSpec((4, QK), lambda b, h, i: (0, 0)),
                  pl.BlockSpec((128, 1), lambda b, h, i: (0, 0))],
        out_specs=pl.BlockSpec((None, tq, 128), lambda b, h, i: (b, i, h)),
        out_shape=jax.ShapeDtypeStruct((bsz, t, 512), BF16),
        scratch_shapes=[pltpu.VMEM((2 * tq, 128), BF16),
                        pltpu.VMEM((1, 2 * tq), F32),
                        pltpu.VMEM((VT_ROWS, 2 * tq), F32),
                        pltpu.VMEM((3, tk, 2 * tq), F32),
                        pltpu.VMEM((3, 1, 2 * tq), F32)],
        compiler_params=pltpu.CompilerParams(
            dimension_semantics=("arbitrary", "arbitrary", "arbitrary"),
            vmem_limit_bytes=VMEM_LIMIT),
        name="attn",
    )(main, main, main, vt, main, lam_vec, subln)


def _chunk_index(s, nchunks, backward):
    if not backward:
        return s
    nctx = N_CTX // CHUNK
    return jnp.where(s < nctx, nctx - 1 - s, nchunks - 1 + nctx - s)


def _tri_mask(backward):
    r = lax.broadcasted_iota(jnp.int32, (CHUNK, CHUNK), 0)
    c = lax.broadcasted_iota(jnp.int32, (CHUNK, CHUNK), 1)
    return (c >= r) if backward else (c <= r)


def _ssd_chunk(u_ref, dtr_ref, bias_ref, alog_ref, o_ref, s_ref, fin, backward):
    mask = _tri_mask(backward)
    tri = jnp.where(mask, 1.0, 0.0).astype(BF16)
    pre = dtr_ref[...] + bias_ref[...]
    dt = jnp.maximum(pre, 0.0) + jnp.log1p(jnp.exp(-jnp.abs(pre)))
    la = dt * (-LOG2E * jnp.exp(alog_ref[...]))
    cum = _tri_dot(tri, la)
    cum_t = cum.T
    dt_t = dt.T
    edge = 0 if backward else CHUNK - 1
    etot = jnp.exp2(cum[edge:edge + 1, :])
    wdt_t = jnp.exp2(cum_t[:, edge:edge + 1] - cum_t) * dt_t
    first = lax.broadcasted_iota(jnp.int32, (CHUNK, 128), 1) < SSD_P
    first_row = lax.broadcasted_iota(jnp.int32, (1, 128), 1) < SSD_P
    col0 = SSD_H if backward else 0
    yield

    for g in range(SSD_G):
        k = u_ref[:, 1024 + g * SSD_N:1024 + (g + 1) * SSD_N]
        q = u_ref[:, 1280 + g * SSD_N:1280 + (g + 1) * SSD_N]
        scores = _dot_nt(q, k)
        k_t = k.astype(F32).T
        y_inter = _dot(q, s_ref[g].astype(BF16))
        ys = []
        yield
        for pp in range(SSD_HPG // 2):
            h0 = g * SSD_HPG + 2 * pp
            c0 = col0 + h0
            off = h0 * SSD_P
            xs = u_ref[:, off:off + 128]
            zero = jnp.zeros_like(xs)
            vals = jnp.concatenate([jnp.where(first, xs, zero),
                                    jnp.where(first, zero, xs)], axis=0)
            wts, kws, ecs = [], [], []
            for hh in range(2):
                c = c0 + hh
                cum_c = jnp.broadcast_to(cum[:, c:c + 1], (CHUNK, CHUNK))
                dec = jnp.exp2(jnp.where(mask, cum_c - cum_t[c:c + 1, :], -jnp.inf))
                wts.append((scores * dec * dt_t[c:c + 1, :]).astype(BF16))
                kws.append((k_t * wdt_t[c:c + 1, :]).astype(BF16))
                ecs.append(jnp.exp2(cum_c))
            y = (jnp.where(first, ecs[0], ecs[1]) * y_inter[:, pp * 128:(pp + 1) * 128]
                 + _dot(jnp.concatenate(wts, axis=1), vals))
            etot_p = jnp.where(first_row, jnp.broadcast_to(etot[:, c0:c0 + 1], (1, 128)),
                               jnp.broadcast_to(etot[:, c0 + 1:c0 + 2], (1, 128)))
            s_old = s_ref[g, :, pp * 128:(pp + 1) * 128]
            s_ref[g, :, pp * 128:(pp + 1) * 128] = (
                etot_p * s_old + _dot(jnp.concatenate(kws, axis=1), vals))
            if backward:
                z_ref, yf_ref, dskip_ref = fin[0:3]
                y = y + yf_ref[:, off:off + 128] + xs.astype(F32) * dskip_ref[:, off:off + 128]
                y = y * _silu(z_ref[:, off:off + 128].astype(F32))
                ys.append(y)
            else:
                o_ref[:, off:off + 128] = y
            yield
        if backward:
            gain_ref = fin[3]
            lo, hi = g * 512, (g + 1) * 512
            yg = jnp.concatenate(ys, axis=1)
            yg = yg * lax.rsqrt(jnp.mean(yg * yg, axis=-1, keepdims=True) + EPS)
            o_ref[:, lo:hi] = (yg * gain_ref[:, lo:hi]).astype(o_ref.dtype)


def _ret_tables(backward):
    exps = RET_EXP_B if backward else RET_EXP_F
    lg = np.log1p(-np.exp2(-np.asarray(exps, np.float64)))
    i = np.arange(CHUNK, dtype=np.float64)
    if backward:
        cum = (CHUNK - i)[None, :] * lg[:, None]
        tot = cum[:, 0]
        msk = i[None, :] >= i[:, None]
    else:
        cum = (i + 1.0)[None, :] * lg[:, None]
        tot = cum[:, -1]
        msk = i[None, :] <= i[:, None]
    dec = np.where(msk[None], np.exp(cum[:, :, None] - cum[:, None, :]), 0.0)
    inter = np.broadcast_to(np.exp(cum)[:, :, None], (RET_H, CHUNK, 128))
    toend = np.broadcast_to(np.exp(tot[:, None] - cum)[:, :, None], (RET_H, CHUNK, 128))
    tables = np.stack([dec, inter, toend], axis=1).astype(np.float32)
    return tables, [float(np.exp(v)) for v in tot]


def _ret_chunk(qk_ref, v_ref, cos, sin, tab_ref, o_ref, s_ref, fin, backward, etot):
    lane = lax.broadcasted_iota(jnp.int32, (CHUNK, 128), 1)
    lo_half = (lane % QK) < (QK // 2)
    first = lane < QK
    srow = lax.broadcasted_iota(jnp.int32, (128, 256), 0) < QK
    scol = lax.broadcasted_iota(jnp.int32, (128, 256), 1) < 128
    diag = srow == scol
    for pp in range(RET_H // 2):
        h0 = 2 * pp
        q = _rope128(qk_ref[:, pp * 128:(pp + 1) * 128].astype(F32), cos, sin, lo_half)
        k = _rope128(qk_ref[:, 256 + pp * 128:256 + (pp + 1) * 128].astype(F32) * (QK ** -0.5),
                     cos, sin, lo_half)
        qb = q.astype(BF16)
        kb = k.astype(BF16)
        k_t = kb.astype(F32).T.astype(BF16)
        zero = jnp.zeros_like(kb)
        k2 = jnp.concatenate([jnp.where(first, kb, zero),
                              jnp.where(first, zero, kb)], axis=0)
        scores = _dot_nt(qb, k2)
        v2 = v_ref[:, h0 * 128:(h0 + 2) * 128]
        zv = jnp.zeros((CHUNK, 128), BF16)
        vdiag = jnp.concatenate(
            [jnp.concatenate([v2[:, 0:128], zv], axis=1),
             jnp.concatenate([zv, v2[:, 128:256]], axis=1)], axis=0)
        dec = jnp.concatenate([tab_ref[h0, 0], tab_ref[h0 + 1, 0]], axis=1)
        inter = jnp.concatenate([tab_ref[h0, 1], tab_ref[h0 + 1, 1]], axis=1)
        toend = jnp.concatenate([tab_ref[h0, 2], tab_ref[h0 + 1, 2]], axis=1)
        s_pair = s_ref[pp]
        y = _dot((scores * dec).astype(BF16), vdiag) + inter * _dot(qb, s_pair.astype(BF16))
        upd = _dot(k_t, (v2.astype(F32) * toend).astype(BF16))
        s_ref[pp] = (jnp.where(scol, etot[h0], etot[h0 + 1]) * s_pair
                     + jnp.where(diag, upd, 0.0))
        for hh in range(2):
            h = h0 + hh
            yh = y[:, hh * 128:(hh + 1) * 128]
            if backward:
                g_ref, yf_ref, gain_ref = fin
                yh = yh + yf_ref[:, h * 128:(h + 1) * 128]
                yh = yh * lax.rsqrt(jnp.mean(yh * yh, axis=-1, keepdims=True) + EPS) * gain_ref[...]
                yh = yh * _silu(g_ref[:, h * 128:(h + 1) * 128].astype(F32))
            o_ref[:, h * 128:(h + 1) * 128] = yh.astype(o_ref.dtype)
        yield


def _scan_kernel(*refs, backward, etot, nb):
    if backward:
        (u_ref, dtr_ref, bias_ref, alog_ref, qk_ref, v_ref, cos_ref, sin_ref, tab_ref,
         z_ref, ssd_f_ref, dskip_ref, ssd_gain_ref, g_ref, ret_f_ref, ret_gain_ref,
         ssd_o_ref, ret_o_ref, ssd_s_ref, ret_s_ref) = refs
    else:
        (u_ref, dtr_ref, bias_ref, alog_ref, qk_ref, v_ref, cos_ref, sin_ref, tab_ref,
         ssd_o_ref, ret_o_ref, ssd_s_ref, ret_s_ref) = refs

    @pl.when(pl.program_id(0) == 0)
    def _():
        ssd_s_ref[...] = jnp.zeros_like(ssd_s_ref)
        ret_s_ref[...] = jnp.zeros_like(ret_s_ref)

    cos = cos_ref[...]
    sin = sin_ref[...]
    chains = []
    for b in range(nb):
        ssd_fin = (z_ref.at[b], ssd_f_ref.at[b], dskip_ref, ssd_gain_ref) if backward else None
        ret_fin = (g_ref.at[b], ret_f_ref.at[b], ret_gain_ref) if backward else None
        chains.append(_ssd_chunk(u_ref.at[b], dtr_ref.at[b], bias_ref, alog_ref,
                                 ssd_o_ref.at[b], ssd_s_ref.at[b], ssd_fin, backward))
        chains.append(_ret_chunk(qk_ref.at[b], v_ref.at[b], cos, sin, tab_ref,
                                 ret_o_ref.at[b], ret_s_ref.at[b], ret_fin, backward, etot))
    while chains:
        chains = [c for c in chains if next(c, _DONE) is not _DONE]


def _scan(dtr, main, cos_t, sin_t, bias, alog, backward, fwd=None, dskip=None,
          ssd_gain=None, ret_gain=None):
    bsz, t, _ = main.shape
    nchunks = t // CHUNK
    tables, etot = _ret_tables(backward)
    idx = functools.partial(_chunk_index, nchunks=nchunks, backward=backward)
    row = lambda s: (0, idx(s), 0)
    const = lambda s: (0, 0)
    in_specs = [pl.BlockSpec((bsz, CHUNK, 1536), lambda s: (0, idx(s), C_XBC // 1536)),
                pl.BlockSpec((bsz, CHUNK, 128), row),
                pl.BlockSpec((1, 128), const),
                pl.BlockSpec((1, 128), const),
                pl.BlockSpec((bsz, CHUNK, 512), lambda s: (0, idx(s), 9)),
                pl.BlockSpec((bsz, CHUNK, 512), lambda s: (0, idx(s), 10)),
                pl.BlockSpec((CHUNK, 128), lambda s: (idx(s), 0)),
                pl.BlockSpec((CHUNK, 128), lambda s: (idx(s), 0)),
                pl.BlockSpec((RET_H, 3, CHUNK, 128), lambda s: (0, 0, 0, 0))]
    args = [main, dtr, bias, alog, main, main, cos_t, sin_t, jnp.asarray(tables)]
    if backward:
        ssd_f, ret_f = fwd
        in_specs += [pl.BlockSpec((bsz, CHUNK, 1024), lambda s: (0, idx(s), 2)),
                     pl.BlockSpec((bsz, CHUNK, 1024), row),
                     pl.BlockSpec((1, 1024), const),
                     pl.BlockSpec((1, 1024), const),
                     pl.BlockSpec((bsz, CHUNK, 512), lambda s: (0, idx(s), 11)),
                     pl.BlockSpec((bsz, CHUNK, 512), row),
                     pl.BlockSpec((1, 128), const)]
        args += [main, ssd_f, dskip, ssd_gain, main, ret_f, ret_gain]
    out_dtype = BF16 if backward else F32
    return pl.pallas_call(
        functools.partial(_scan_kernel, backward=backward, etot=etot, nb=bsz),
        grid=(nchunks,),
        in_specs=in_specs,
        out_specs=[pl.BlockSpec((bsz, CHUNK, 1024), row),
                   pl.BlockSpec((bsz, CHUNK, 512), row)],
        out_shape=[jax.ShapeDtypeStruct((bsz, t, 1024), out_dtype),
                   jax.ShapeDtypeStruct((bsz, t, 512), out_dtype)],
        scratch_shapes=[pltpu.VMEM((bsz, SSD_G, SSD_N, 512), F32),
                        pltpu.VMEM((bsz, RET_H // 2, 128, 256), F32)],
        compiler_params=pltpu.CompilerParams(
            dimension_semantics=("arbitrary",), vmem_limit_bytes=VMEM_LIMIT),
        name="scan_bwd" if backward else "scan_fwd",
    )(*args)


def _outproj_kernel(x_ref, a_ref, s_ref, r_ref, ada_ref, w_ref, o_ref, *, tm, blk0, ctx_row):
    b = pl.program_id(0)
    i = pl.program_id(1) + blk0
    acc = _dot(a_ref[...], w_ref[0:512, :])
    acc += _dot(s_ref[...], w_ref[512:1536, :])
    acc += _dot(r_ref[...], w_ref[1536:2048, :])
    rows = i * tm + lax.broadcasted_iota(jnp.int32, (tm, 1), 0)
    gate = _row_mod(ada_ref, b, rows, 2 * D, 3 * D, ctx_row)
    o_ref[...] = x_ref[...] + gate * acc


def _outproj(xcat, attn_o, ssd_o, ret_o, ada, w_out, ctx_row, latent_only):
    bsz, t, _ = xcat.shape
    tm = N_CTX if latent_only else 768
    blk0 = 1 if latent_only else 0
    nb = t // tm - blk0
    row = lambda b, i: (b, i + blk0, 0)
    return pl.pallas_call(
        functools.partial(_outproj_kernel, tm=tm, blk0=blk0, ctx_row=ctx_row),
        grid=(bsz, nb),
        in_specs=[pl.BlockSpec((None, tm, D), row),
                  pl.BlockSpec((None, tm, 512), row),
                  pl.BlockSpec((None, tm, 1024), row),
                  pl.BlockSpec((None, tm, 512), row),
                  pl.BlockSpec((8, 3 * D), lambda b, i: (0, 0)),
                  pl.BlockSpec((2 * D, D), lambda b, i: (0, 0))],
        out_specs=pl.BlockSpec((None, tm, D), lambda b, i: (b, i, 0)),
        out_shape=jax.ShapeDtypeStruct((bsz, nb * tm, D), F32),
        compiler_params=pltpu.CompilerParams(
            dimension_semantics=("arbitrary", "arbitrary"), vmem_limit_bytes=VMEM_LIMIT),
        name="outproj",
    )(xcat, attn_o, ssd_o, ret_o, ada, w_out)


def _rope_tables(seq):
    n_rows = seq // GRID_W
    row = np.repeat(np.arange(n_rows, dtype=np.float32), GRID_W)
    col = np.tile(np.arange(GRID_W, dtype=np.float32), n_rows)
    n_freq = QK // 4
    inv_freq = (np.float32(ROPE_BASE) ** (-np.arange(n_freq, dtype=np.float32) / n_freq)).astype(np.float32)
    ang = np.concatenate([row[:, None] * inv_freq, col[:, None] * inv_freq], axis=-1).astype(np.float32)
    cos = np.concatenate([np.ones((N_CTX, QK // 2)), np.cos(ang.astype(np.float64))], axis=0)
    sin = np.concatenate([np.zeros((N_CTX, QK // 2)), np.sin(ang.astype(np.float64))], axis=0)
    cos_t = np.tile(cos, (1, 4)).astype(np.float32)
    sin_t = np.tile(np.concatenate([-sin, sin], axis=1), (1, 2)).astype(np.float32)
    return jnp.asarray(cos_t), jnp.asarray(sin_t)


_MAIN_COLS = ((0, 2048), (3616, 4640), (2048, 3584), (4640, 6176))
_DT_COLS = (3584, 3616)


def _pad_lanes(v, n=128):
    v = v.reshape(1, -1)
    return jnp.pad(v, ((0, 0), (0, n - v.shape[1])))


def kernel(x, c, ctx, c_ctx, w_ada, b_ada, w_in, w_out, attn_q_norm, attn_k_norm,
           lambda_q1, lambda_k1, lambda_q2, lambda_k2, attn_subln,
           ssd_conv_w, ssd_conv_b, ssd_dt_bias, ssd_a_log, ssd_d, ssd_norm, ret_norm):
    bsz, seq, _ = x.shape
    depth = w_ada.shape[0]
    assert bsz + 1 <= 8 and ctx.shape[1] == N_CTX
    ctx_row = bsz
    xcat = jnp.concatenate([ctx, x], axis=1)
    cvec = jnp.zeros((8, D), F32).at[0:bsz].set(c).at[ctx_row].set(c_ctx)
    cos_t, sin_t = _rope_tables(seq)
    gsel = np.arange(512) // QK
    gmat = jnp.asarray((gsel[:, None] == gsel[None, :]).astype(np.float32), BF16)

    for layer in range(depth):
        last = layer == depth - 1
        lam_init = 0.8 - 0.6 * math.exp(-0.3 * layer)
        w_l = w_in[layer]
        w_all = jnp.concatenate([w_l[:, a:b] for a, b in _MAIN_COLS + (_DT_COLS,)]
                                + [jnp.zeros((D, 128 - (_DT_COLS[1] - _DT_COLS[0])), F32)],
                                axis=1).astype(BF16)

        ada = _ada(cvec, w_ada[layer], b_ada[layer].reshape(1, -1))
        main, dtr, vt = _inproj(xcat, ada, w_all, cos_t, sin_t,
                                jnp.tile(attn_q_norm[layer], 8).reshape(1, 512),
                                jnp.tile(attn_k_norm[layer], 8).reshape(1, 512), gmat,
                                ssd_conv_w[layer], ssd_conv_b[layer].reshape(1, -1), ctx_row)

        lam_vec = jnp.stack([lambda_q1[layer], lambda_k1[layer], lambda_q2[layer], lambda_k2[layer]])
        attn_o = _attn(main, vt, lam_vec, attn_subln[layer].reshape(128, 1), lam_init,
                       tq=N_CTX, tk=ATT_TK)

        bias = _pad_lanes(ssd_dt_bias[layer])
        alog = _pad_lanes(ssd_a_log[layer])
        fwd = _scan(dtr, main, cos_t, sin_t, bias, alog, backward=False)
        ssd_o, ret_o = _scan(dtr, main, cos_t, sin_t, bias, alog, backward=True, fwd=fwd,
                             dskip=jnp.repeat(ssd_d[layer], SSD_P).reshape(1, 1024),
                             ssd_gain=ssd_norm[layer].reshape(1, 1024),
                             ret_gain=ret_norm[layer].reshape(1, 128))

        xcat = _outproj(xcat, attn_o, ssd_o, ret_o, ada, w_out[layer].astype(BF16), ctx_row,
                        latent_only=last)
    return xcat
```

```python
import functools
import math

import numpy as np
import jax
import jax.numpy as jnp
from jax import lax
from jax.experimental import pallas as pl
from jax.experimental.pallas import tpu as pltpu

F32 = jnp.float32
BF16 = jnp.bfloat16

D = 1024
N_CTX = 256
GRID_W = 64
EPS = 1e-6
ROPE_BASE = 10000.0
CHUNK = 128
QK = 64
ATT_H = 4
SSD_H = 16
SSD_G = 2
SSD_HPG = 8
SSD_P = 64
SSD_N = 128
RET_H = 4
RET_EXP_F = (5.0, 6.0, 7.0, 8.0)
RET_EXP_B = (5.5, 6.5, 7.5, 8.5)
N_MAIN = 6144
LOG2E = math.log2(math.e)
VMEM_LIMIT = 56 * 1024 * 1024
ATT_TK = 768
VT_ROWS = 144
_DONE = object()


def _silu(v):
    return v * (0.5 + 0.5 * jnp.tanh(0.5 * v))


def _split3(v):
    hi = v.astype(BF16)
    r1 = v - hi.astype(F32)
    mid = r1.astype(BF16)
    lo = (r1 - mid.astype(F32)).astype(BF16)
    return hi, mid, lo


def _dot(a, b):
    return jnp.dot(a, b, preferred_element_type=F32)


def _dot_nt(a, b):
    return lax.dot_general(a, b, (((1,), (1,)), ((), ())), preferred_element_type=F32)


def _tri_dot(tri, v):
    hi, mid, lo = _split3(v)
    return _dot(tri, hi) + _dot(tri, mid) + _dot(tri, lo)


def _ada_kernel(c_ref, w_ref, b_ref, o_ref):
    o_ref[...] = _dot(_silu(c_ref[...]), w_ref[...]) + b_ref[...]


def _ada(cvec, w, b):
    n = w.shape[1]
    tn = 768
    return pl.pallas_call(
        _ada_kernel,
        grid=(n // tn,),
        in_specs=[pl.BlockSpec((8, D), lambda j: (0, 0)),
                  pl.BlockSpec((D, tn), lambda j: (0, j)),
                  pl.BlockSpec((1, tn), lambda j: (0, j))],
        out_specs=pl.BlockSpec((8, tn), lambda j: (0, j)),
        out_shape=jax.ShapeDtypeStruct((8, n), F32),
        name="ada",
    )(cvec, w, b)


def _row_mod(ada_ref, b, rows, lo, hi, ctx_row):
    vb = ada_ref[pl.ds(b, 1), lo:hi]
    vc = ada_ref[ctx_row:ctx_row + 1, lo:hi]
    return jnp.where(rows < N_CTX, vc, vb)


def _rope128(x, cos, sin, lo_half):
    partner = jnp.where(lo_half, pltpu.roll(x, 96, 1), pltpu.roll(x, 32, 1))
    return x * cos + partner * sin


C_Q, C_K, C_V, C_XBC = 0, 512, 1024, 3072
PLAIN_COLS = ((1536, 3072), (4608, 6144))


def _inproj_kernel(x_ref, xp_ref, xn_ref, ada_ref, w_ref, cos_ref, sin_ref,
                   gq_ref, gk_ref, gmat_ref, cw_ref, cb_ref,
                   main_ref, dt_ref, vt_ref, *, tm, t, ctx_row):
    b = pl.program_id(0)
    i = pl.program_id(1)

    def modulated(x, rows):
        xn = x * lax.rsqrt(jnp.mean(x * x, axis=-1, keepdims=True) + EPS)
        shift = _row_mod(ada_ref, b, rows, 0, D, ctx_row)
        scale = _row_mod(ada_ref, b, rows, D, 2 * D, ctx_row)
        return xn * (1.0 + scale) + shift

    loc = lax.broadcasted_iota(jnp.int32, (tm, 1), 0)
    row = i * tm + loc
    h = modulated(x_ref[...], row).astype(BF16)

    halo = lax.broadcasted_iota(jnp.int32, (8, 1), 0)
    h_halo = jnp.concatenate([modulated(xp_ref[...], i * tm - 8 + halo),
                              modulated(xn_ref[...], (i + 1) * tm + halo)], axis=0)
    h_ext = jnp.concatenate([h, h_halo.astype(BF16)], axis=0)
    seq_start = (row == 0) | (row == N_CTX)
    seq_end = (row == t - 1) | (row == N_CTX - 1)

    cos = cos_ref[...]
    sin = sin_ref[...]
    lane = lax.broadcasted_iota(jnp.int32, (tm, 128), 1)
    lo_half = (lane % QK) < (QK // 2)

    def proj(c0, width=512):
        return _dot(h, w_ref[:, c0:c0 + width])

    def conv_mm(c0):
        res = _dot(h_ext, w_ref[:, c0:c0 + 512])
        return res[0:tm], res[tm + 7:tm + 8], res[tm + 8:tm + 9]

    def conv_epilogue(c0, res):
        xbc, prev, nxt = res
        xm1 = jnp.where(loc == 0, prev, pltpu.roll(xbc, 1, 0))
        xm1 = jnp.where(seq_start, 0.0, xm1)
        xp1 = jnp.where(loc == tm - 1, nxt, pltpu.roll(xbc, tm - 1, 0))
        xp1 = jnp.where(seq_end, 0.0, xp1)
        cc = c0 - C_XBC
        cw = cw_ref[:, cc:cc + 512]
        y = cw[0:1] * xm1 + cw[1:2] * xbc + cw[2:3] * xp1 + cb_ref[:, cc:cc + 512]
        main_ref[:, c0:c0 + 512] = _silu(y).astype(BF16)

    def qk_epilogue(c0, gain_ref, out_scale, y):
        ss = _dot((y * y).astype(BF16), gmat_ref[...])
        y = y * lax.rsqrt(ss * (1.0 / QK) + EPS) * gain_ref[...]
        for g in range(4):
            yg = _rope128(y[:, g * 128:(g + 1) * 128], cos, sin, lo_half)
            main_ref[:, c0 + g * 128:c0 + (g + 1) * 128] = (yg * out_scale).astype(BF16)

    def v_epilogue(v):
        main_ref[:, C_V:C_V + 512] = v.astype(BF16)
        ones_rows = (lax.broadcasted_iota(jnp.int32, (VT_ROWS - 128, tm), 0) == 0).astype(BF16)
        for hd in range(ATT_H):
            vt_ref[hd, 0:128, :] = v[:, hd * 128:(hd + 1) * 128].T.astype(BF16)
            vt_ref[hd, 128:VT_ROWS, :] = ones_rows

    def plain_epilogue(c0, y):
        main_ref[:, c0:c0 + 512] = y.astype(BF16)

    def dt_epilogue(y):
        dt_ref[...] = y

    heavy = [(functools.partial(conv_mm, c0), functools.partial(conv_epilogue, c0))
             for c0 in range(C_XBC, C_XBC + 1536, 512)]
    heavy += [(functools.partial(proj, C_Q),
               functools.partial(qk_epilogue, C_Q, gq_ref, (QK ** -0.5) * LOG2E)),
              (functools.partial(proj, C_K), functools.partial(qk_epilogue, C_K, gk_ref, 1.0)),
              (functools.partial(proj, C_V), v_epilogue)]
    light = [(functools.partial(proj, c0), functools.partial(plain_epilogue, c0))
             for lo, hi in PLAIN_COLS for c0 in range(lo, hi, 512)]
    light.append((functools.partial(proj, N_MAIN, 128), dt_epilogue))
    stages = []
    while heavy or light:
        if heavy:
            stages.append(heavy.pop(0))
        if light:
            stages.append(light.pop(0))
    pending = None
    for mm, epilogue in stages:
        res = mm()
        if pending is not None:
            pending()
        pending = functools.partial(epilogue, res)
    pending()


def _inproj(xcat, ada, w_all, cos_t, sin_t, gq, gk, gmat, conv_w, conv_b, ctx_row):
    bsz, t, _ = xcat.shape
    tm = 384
    assert ATT_TK % tm == 0
    per_tile = ATT_TK // tm
    r8 = tm // 8
    last8 = t // 8 - 1
    const = lambda b, i: (0, 0)
    return pl.pallas_call(
        functools.partial(_inproj_kernel, tm=tm, t=t, ctx_row=ctx_row),
        grid=(bsz, t // tm),
        in_specs=[pl.BlockSpec((None, tm, D), lambda b, i: (b, i, 0)),
                  pl.BlockSpec((None, 8, D), lambda b, i: (b, jnp.maximum(i * r8 - 1, 0), 0)),
                  pl.BlockSpec((None, 8, D), lambda b, i: (b, jnp.minimum((i + 1) * r8, last8), 0)),
                  pl.BlockSpec((8, 3 * D), const),
                  pl.BlockSpec((D, N_MAIN + 128), const),
                  pl.BlockSpec((tm, 128), lambda b, i: (i, 0)),
                  pl.BlockSpec((tm, 128), lambda b, i: (i, 0)),
                  pl.BlockSpec((1, 512), const),
                  pl.BlockSpec((1, 512), const),
                  pl.BlockSpec((512, 512), const),
                  pl.BlockSpec((3, 1536), const),
                  pl.BlockSpec((1, 1536), const)],
        out_specs=[pl.BlockSpec((None, tm, N_MAIN), lambda b, i: (b, i, 0)),
                   pl.BlockSpec((None, tm, 128), lambda b, i: (b, i, 0)),
                   pl.BlockSpec((None, ATT_H, None, VT_ROWS, tm),
                                lambda b, i: (b, 0, i // per_tile, 0, i % per_tile))],
        out_shape=[jax.ShapeDtypeStruct((bsz, t, N_MAIN), BF16),
                   jax.ShapeDtypeStruct((bsz, t, 128), F32),
                   jax.ShapeDtypeStruct((bsz, ATT_H, t // ATT_TK, VT_ROWS, ATT_TK), BF16)],
        compiler_params=pltpu.CompilerParams(
            dimension_semantics=("arbitrary", "arbitrary"), vmem_limit_bytes=VMEM_LIMIT),
        name="inproj",
    )(xcat, xcat, xcat, ada, w_all, cos_t, sin_t, gq, gk, gmat, conv_w, conv_b)


def _attn_kernel(q_ref, qn_ref, k_ref, vt_ref, gate_ref, lam_ref, subln_ref, o_ref,
                 q2_ref, m_ref, acc_ref, s_ref, mt_ref, *, tq, tk, nkv, lam_init):
    qi = pl.program_id(2)
    lane = lax.broadcasted_iota(jnp.int32, (tq, 128), 1)
    m_ref[...] = jnp.full((1, 2 * tq), -jnp.inf, F32)
    acc_ref[...] = jnp.zeros((VT_ROWS, 2 * tq), F32)

    def stack_maps(src_ref):
        q = src_ref[...]
        zero = jnp.zeros_like(q)
        q2_ref[0:tq, :] = jnp.where(lane < QK, q, zero)
        q2_ref[tq:2 * tq, :] = jnp.where(lane >= QK, q, zero)

    def hand_off():
        stack_maps(qn_ref)
        scores(0, 2, False)

    def scores(j, slot, ctx_only):
        st = _dot_nt(k_ref[pl.ds(j * tk, tk), :], q2_ref[...])
        if ctx_only:
            krow = j * tk + lax.broadcasted_iota(jnp.int32, (tk, 1), 0)
            st = jnp.where(krow < N_CTX, st, -jnp.inf)
        s_ref[slot] = st
        mt_ref[slot] = jnp.max(st, axis=0, keepdims=True)

    def consume(j, slot):
        m_old = m_ref[...]
        m_new = jnp.maximum(m_old, mt_ref[slot])
        alpha = jnp.exp2(m_old - m_new)
        p = jnp.exp2(s_ref[slot] - m_new)
        acc_ref[...] = alpha * acc_ref[...] + _dot(vt_ref[j], p.astype(BF16))
        m_ref[...] = m_new

    def finalize():
        lam_v = lam_ref[...]
        lam = (jnp.exp(jnp.sum(lam_v[0:1] * lam_v[1:2], axis=-1, keepdims=True))
               - jnp.exp(jnp.sum(lam_v[2:3] * lam_v[3:4], axis=-1, keepdims=True)) + lam_init)
        inv = 1.0 / acc_ref[128:129, :]
        acc = acc_ref[0:128, :]
        o = acc[:, 0:tq] * inv[:, 0:tq] - lam * (acc[:, tq:2 * tq] * inv[:, tq:2 * tq])
        o = o * lax.rsqrt(jnp.mean(o * o, axis=0, keepdims=True) + EPS)
        o = o * subln_ref[...] * (1.0 - lam_init)
        o_ref[...] = (o.T * _silu(gate_ref[...].astype(F32))).astype(BF16)

    @pl.when(qi == 0)
    def _():
        stack_maps(q_ref)
        scores(0, 0, True)
        hand_off()
        consume(0, 0)
        finalize()

    @pl.when(qi != 0)
    def _():
        for j in range(nkv):
            if j + 1 < nkv:
                scores(j + 1, (j + 1) % 2, False)
            else:
                hand_off()
            consume(j, 2 if j == 0 else j % 2)
        finalize()


def _attn(main, vt, lam_vec, subln, lam_init, tq, tk):
    bsz, t, _ = main.shape
    nkv = t // tk
    assert tq == N_CTX and tk >= N_CTX and nkv >= 2
    nq = t // tq
    return pl.pallas_call(
        functools.partial(_attn_kernel, tq=tq, tk=tk, nkv=nkv, lam_init=lam_init),
        grid=(bsz, ATT_H, nq),
        in_specs=[pl.BlockSpec((None, tq, 128), lambda b, h, i: (b, i, h)),
                  pl.BlockSpec((None, tq, 128), lambda b, h, i: (b, jnp.minimum(i + 1, nq - 1), h)),
                  pl.BlockSpec((None, t, 128), lambda b, h, i: (b, 0, C_K // 128 + h)),
                  pl.BlockSpec((None, None, nkv, VT_ROWS, tk), lambda b, h, i: (b, h, 0, 0, 0)),
                  pl.BlockSpec((None, tq, 128), lambda b, h, i: (b, i, 12 + h)),
                  pl.BlockSpec((4, QK), lambda b, h, i: (0, 0)),
                  pl.BlockSpec((128, 1), lambda b, h, i: (0, 0))],
        out_specs=pl.BlockSpec((None, tq, 128), lambda b, h, i: (b, i, h)),
        out_shape=jax.ShapeDtypeStruct((bsz, t, 512), BF16),
        scratch_shapes=[pltpu.VMEM((2 * tq, 128), BF16),
                        pltpu.VMEM((1, 2 * tq), F32),
                        pltpu.VMEM((VT_ROWS, 2 * tq), F32),
                        pltpu.VMEM((3, tk, 2 * tq), F32),
                        pltpu.VMEM((3, 1, 2 * tq), F32)],
        compiler_params=pltpu.CompilerParams(
            dimension_semantics=("arbitrary", "arbitrary", "arbitrary"),
            vmem_limit_bytes=VMEM_LIMIT),
        name="attn",
    )(main, main, main, vt, main, lam_vec, subln)


def _chunk_index(s, nchunks, backward):
    if not backward:
        return s
    nctx = N_CTX // CHUNK
    return jnp.where(s < nctx, nctx - 1 - s, nchunks - 1 + nctx - s)


def _tri_mask(backward):
    r = lax.broadcasted_iota(jnp.int32, (CHUNK, CHUNK), 0)
    c = lax.broadcasted_iota(jnp.int32, (CHUNK, CHUNK), 1)
    return (c >= r) if backward else (c <= r)


def _ssd_chunk(u_ref, dtr_ref, bias_ref, alog_ref, o_ref, s_ref, fin, backward):
    mask = _tri_mask(backward)
    tri = jnp.where(mask, 1.0, 0.0).astype(BF16)
    pre = dtr_ref[...] + bias_ref[...]
    dt = jnp.maximum(pre, 0.0) + jnp.log1p(jnp.exp(-jnp.abs(pre)))
    la = dt * (-LOG2E * jnp.exp(alog_ref[...]))
    cum = _tri_dot(tri, la)
    cum_t = cum.T
    dt_t = dt.T
    edge = 0 if backward else CHUNK - 1
    etot = jnp.exp2(cum[edge:edge + 1, :])
    wdt_t = jnp.exp2(cum_t[:, edge:edge + 1] - cum_t) * dt_t
    first = lax.broadcasted_iota(jnp.int32, (CHUNK, 128), 1) < SSD_P
    first_row = lax.broadcasted_iota(jnp.int32, (1, 128), 1) < SSD_P
    col0 = SSD_H if backward else 0
    yield

    for g in range(SSD_G):
        k = u_ref[:, 1024 + g * SSD_N:1024 + (g + 1) * SSD_N]
        q = u_ref[:, 1280 + g * SSD_N:1280 + (g + 1) * SSD_N]
        scores = _dot_nt(q, k)
        k_t = k.astype(F32).T
        y_inter = _dot(q, s_ref[g].astype(BF16))
        ys = []
        yield
        for pp in range(SSD_HPG // 2):
            h0 = g * SSD_HPG + 2 * pp
            c0 = col0 + h0
            off = h0 * SSD_P
            xs = u_ref[:, off:off + 128]
            zero = jnp.zeros_like(xs)
            vals = jnp.concatenate([jnp.where(first, xs, zero),
                                    jnp.where(first, zero, xs)], axis=0)
            wts, kws, ecs = [], [], []
            for hh in range(2):
                c = c0 + hh
                cum_c = jnp.broadcast_to(cum[:, c:c + 1], (CHUNK, CHUNK))
                dec = jnp.exp2(jnp.where(mask, cum_c - cum_t[c:c + 1, :], -jnp.inf))
                wts.append((scores * dec * dt_t[c:c + 1, :]).astype(BF16))
                kws.append((k_t * wdt_t[c:c + 1, :]).astype(BF16))
                ecs.append(jnp.exp2(cum_c))
            y = (jnp.where(first, ecs[0], ecs[1]) * y_inter[:, pp * 128:(pp + 1) * 128]
                 + _dot(jnp.concatenate(wts, axis=1), vals))
            etot_p = jnp.where(first_row, jnp.broadcast_to(etot[:, c0:c0 + 1], (1, 128)),
                               jnp.broadcast_to(etot[:, c0 + 1:c0 + 2], (1, 128)))
            s_old = s_ref[g, :, pp * 128:(pp + 1) * 128]
            s_ref[g, :, pp * 128:(pp + 1) * 128] = (
                etot_p * s_old + _dot(jnp.concatenate(kws, axis=1), vals))
            if backward:
                z_ref, yf_ref, dskip_ref = fin[0:3]
                y = y + yf_ref[:, off:off + 128] + xs.astype(F32) * dskip_ref[:, off:off + 128]
                y = y * _silu(z_ref[:, off:off + 128].astype(F32))
                ys.append(y)
            else:
                o_ref[:, off:off + 128] = y
            yield
        if backward:
            gain_ref = fin[3]
            lo, hi = g * 512, (g + 1) * 512
            yg = jnp.concatenate(ys, axis=1)
            yg = yg * lax.rsqrt(jnp.mean(yg * yg, axis=-1, keepdims=True) + EPS)
            o_ref[:, lo:hi] = (yg * gain_ref[:, lo:hi]).astype(o_ref.dtype)


def _ret_tables(backward):
    exps = RET_EXP_B if backward else RET_EXP_F
    lg = np.log1p(-np.exp2(-np.asarray(exps, np.float64)))
    i = np.arange(CHUNK, dtype=np.float64)
    if backward:
        cum = (CHUNK - i)[None, :] * lg[:, None]
        tot = cum[:, 0]
        msk = i[None, :] >= i[:, None]
    else:
        cum = (i + 1.0)[None, :] * lg[:, None]
        tot = cum[:, -1]
        msk = i[None, :] <= i[:, None]
    dec = np.where(msk[None], np.exp(cum[:, :, None] - cum[:, None, :]), 0.0)
    inter = np.broadcast_to(np.exp(cum)[:, :, None], (RET_H, CHUNK, 128))
    toend = np.broadcast_to(np.exp(tot[:, None] - cum)[:, :, None], (RET_H, CHUNK, 128))
    tables = np.stack([dec, inter, toend], axis=1).astype(np.float32)
    return tables, [float(np.exp(v)) for v in tot]


def _ret_chunk(qk_ref, v_ref, cos, sin, tab_ref, o_ref, s_ref, fin, backward, etot):
    lane = lax.broadcasted_iota(jnp.int32, (CHUNK, 128), 1)
    lo_half = (lane % QK) < (QK // 2)
    first = lane < QK
    srow = lax.broadcasted_iota(jnp.int32, (128, 256), 0) < QK
    scol = lax.broadcasted_iota(jnp.int32, (128, 256), 1) < 128
    diag = srow == scol
    for pp in range(RET_H // 2):
        h0 = 2 * pp
        q = _rope128(qk_ref[:, pp * 128:(pp + 1) * 128].astype(F32), cos, sin, lo_half)
        k = _rope128(qk_ref[:, 256 + pp * 128:256 + (pp + 1) * 128].astype(F32) * (QK ** -0.5),
                     cos, sin, lo_half)
        qb = q.astype(BF16)
        kb = k.astype(BF16)
        k_t = kb.astype(F32).T.astype(BF16)
        zero = jnp.zeros_like(kb)
        k2 = jnp.concatenate([jnp.where(first, kb, zero),
                              jnp.where(first, zero, kb)], axis=0)
        scores = _dot_nt(qb, k2)
        v2 = v_ref[:, h0 * 128:(h0 + 2) * 128]
        zv = jnp.zeros((CHUNK, 128), BF16)
        vdiag = jnp.concatenate(
            [jnp.concatenate([v2[:, 0:128], zv], axis=1),
             jnp.concatenate([zv, v2[:, 128:256]], axis=1)], axis=0)
        dec = jnp.concatenate([tab_ref[h0, 0], tab_ref[h0 + 1, 0]], axis=1)
        inter = jnp.concatenate([tab_ref[h0, 1], tab_ref[h0 + 1, 1]], axis=1)
        toend = jnp.concatenate([tab_ref[h0, 2], tab_ref[h0 + 1, 2]], axis=1)
        s_pair = s_ref[pp]
        y = _dot((scores * dec).astype(BF16), vdiag) + inter * _dot(qb, s_pair.astype(BF16))
        upd = _dot(k_t, (v2.astype(F32) * toend).astype(BF16))
        s_ref[pp] = (jnp.where(scol, etot[h0], etot[h0 + 1]) * s_pair
                     + jnp.where(diag, upd, 0.0))
        for hh in range(2):
            h = h0 + hh
            yh = y[:, hh * 128:(hh + 1) * 128]
            if backward:
                g_ref, yf_ref, gain_ref = fin
                yh = yh + yf_ref[:, h * 128:(h + 1) * 128]
                yh = yh * lax.rsqrt(jnp.mean(yh * yh, axis=-1, keepdims=True) + EPS) * gain_ref[...]
                yh = yh * _silu(g_ref[:, h * 128:(h + 1) * 128].astype(F32))
            o_ref[:, h * 128:(h + 1) * 128] = yh.astype(o_ref.dtype)
        yield


def _scan_kernel(*refs, backward, etot, nb):
    if backward:
        (u_ref, dtr_ref, bias_ref, alog_ref, qk_ref, v_ref, cos_ref, sin_ref, tab_ref,
         z_ref, ssd_f_ref, dskip_ref, ssd_gain_ref, g_ref, ret_f_ref, ret_gain_ref,
         ssd_o_ref, ret_o_ref, ssd_s_ref, ret_s_ref) = refs
    else:
        (u_ref, dtr_ref, bias_ref, alog_ref, qk_ref, v_ref, cos_ref, sin_ref, tab_ref,
         ssd_o_ref, ret_o_ref, ssd_s_ref, ret_s_ref) = refs

    @pl.when(pl.program_id(0) == 0)
    def _():
        ssd_s_ref[...] = jnp.zeros_like(ssd_s_ref)
        ret_s_ref[...] = jnp.zeros_like(ret_s_ref)

    cos = cos_ref[...]
    sin = sin_ref[...]
    chains = []
    for b in range(nb):
        ssd_fin = (z_ref.at[b], ssd_f_ref.at[b], dskip_ref, ssd_gain_ref) if backward else None
        ret_fin = (g_ref.at[b], ret_f_ref.at[b], ret_gain_ref) if backward else None
        chains.append(_ssd_chunk(u_ref.at[b], dtr_ref.at[b], bias_ref, alog_ref,
                                 ssd_o_ref.at[b], ssd_s_ref.at[b], ssd_fin, backward))
        chains.append(_ret_chunk(qk_ref.at[b], v_ref.at[b], cos, sin, tab_ref,
                                 ret_o_ref.at[b], ret_s_ref.at[b], ret_fin, backward, etot))
    while chains:
        chains = [c for c in chains if next(c, _DONE) is not _DONE]


def _scan(dtr, main, cos_t, sin_t, bias, alog, backward, fwd=None, dskip=None,
          ssd_gain=None, ret_gain=None):
    bsz, t, _ = main.shape
    nchunks = t // CHUNK
    tables, etot = _ret_tables(backward)
    idx = functools.partial(_chunk_index, nchunks=nchunks, backward=backward)
    row = lambda s: (0, idx(s), 0)
    const = lambda s: (0, 0)
    in_specs = [pl.BlockSpec((bsz, CHUNK, 1536), lambda s: (0, idx(s), C_XBC // 1536)),
                pl.BlockSpec((bsz, CHUNK, 128), row),
                pl.BlockSpec((1, 128), const),
                pl.BlockSpec((1, 128), const),
                pl.BlockSpec((bsz, CHUNK, 512), lambda s: (0, idx(s), 9)),
                pl.BlockSpec((bsz, CHUNK, 512), lambda s: (0, idx(s), 10)),
                pl.BlockSpec((CHUNK, 128), lambda s: (idx(s), 0)),
                pl.BlockSpec((CHUNK, 128), lambda s: (idx(s), 0)),
                pl.BlockSpec((RET_H, 3, CHUNK, 128), lambda s: (0, 0, 0, 0))]
    args = [main, dtr, bias, alog, main, main, cos_t, sin_t, jnp.asarray(tables)]
    if backward:
        ssd_f, ret_f = fwd
        in_specs += [pl.BlockSpec((bsz, CHUNK, 1024), lambda s: (0, idx(s), 2)),
                     pl.BlockSpec((bsz, CHUNK, 1024), row),
                     pl.BlockSpec((1, 1024), const),
                     pl.BlockSpec((1, 1024), const),
                     pl.BlockSpec((bsz, CHUNK, 512), lambda s: (0, idx(s), 11)),
                     pl.BlockSpec((bsz, CHUNK, 512), row),
                     pl.BlockSpec((1, 128), const)]
        args += [main, ssd_f, dskip, ssd_gain, main, ret_f, ret_gain]
    out_dtype = BF16 if backward else F32
    return pl.pallas_call(
        functools.partial(_scan_kernel, backward=backward, etot=etot, nb=bsz),
        grid=(nchunks,),
        in_specs=in_specs,
        out_specs=[pl.BlockSpec((bsz, CHUNK, 1024), row),
                   pl.BlockSpec((bsz, CHUNK, 512), row)],
        out_shape=[jax.ShapeDtypeStruct((bsz, t, 1024), out_dtype),
                   jax.ShapeDtypeStruct((bsz, t, 512), out_dtype)],
        scratch_shapes=[pltpu.VMEM((bsz, SSD_G, SSD_N, 512), F32),
                        pltpu.VMEM((bsz, RET_H // 2, 128, 256), F32)],
        compiler_params=pltpu.CompilerParams(
            dimension_semantics=("arbitrary",), vmem_limit_bytes=VMEM_LIMIT),
        name="scan_bwd" if backward else "scan_fwd",
    )(*args)


def _outproj_kernel(x_ref, a_ref, s_ref, r_ref, ada_ref, w_ref, o_ref, *, tm, blk0, ctx_row):
    b = pl.program_id(0)
    i = pl.program_id(1) + blk0
    acc = _dot(a_ref[...], w_ref[0:512, :])
    acc += _dot(s_ref[...], w_ref[512:1536, :])
    acc += _dot(r_ref[...], w_ref[1536:2048, :])
    rows = i * tm + lax.broadcasted_iota(jnp.int32, (tm, 1), 0)
    gate = _row_mod(ada_ref, b, rows, 2 * D, 3 * D, ctx_row)
    o_ref[...] = x_ref[...] + gate * acc


def _outproj(xcat, attn_o, ssd_o, ret_o, ada, w_out, ctx_row, latent_only):
    bsz, t, _ = xcat.shape
    tm = N_CTX if latent_only else 768
    blk0 = 1 if latent_only else 0
    nb = t // tm - blk0
    row = lambda b, i: (b, i + blk0, 0)
    return pl.pallas_call(
        functools.partial(_outproj_kernel, tm=tm, blk0=blk0, ctx_row=ctx_row),
        grid=(bsz, nb),
        in_specs=[pl.BlockSpec((None, tm, D), row),
                  pl.BlockSpec((None, tm, 512), row),
                  pl.BlockSpec((None, tm, 1024), row),
                  pl.BlockSpec((None, tm, 512), row),
                  pl.BlockSpec((8, 3 * D), lambda b, i: (0, 0)),
                  pl.BlockSpec((2 * D, D), lambda b, i: (0, 0))],
        out_specs=pl.BlockSpec((None, tm, D), lambda b, i: (b, i, 0)),
        out_shape=jax.ShapeDtypeStruct((bsz, nb * tm, D), F32),
        compiler_params=pltpu.CompilerParams(
            dimension_semantics=("arbitrary", "arbitrary"), vmem_limit_bytes=VMEM_LIMIT),
        name="outproj",
    )(xcat, attn_o, ssd_o, ret_o, ada, w_out)


def _rope_tables(seq):
    n_rows = seq // GRID_W
    row = np.repeat(np.arange(n_rows, dtype=np.float32), GRID_W)
    col = np.tile(np.arange(GRID_W, dtype=np.float32), n_rows)
    n_freq = QK // 4
    inv_freq = (np.float32(ROPE_BASE) ** (-np.arange(n_freq, dtype=np.float32) / n_freq)).astype(np.float32)
    ang = np.concatenate([row[:, None] * inv_freq, col[:, None] * inv_freq], axis=-1).astype(np.float32)
    cos = np.concatenate([np.ones((N_CTX, QK // 2)), np.cos(ang.astype(np.float64))], axis=0)
    sin = np.concatenate([np.zeros((N_CTX, QK // 2)), np.sin(ang.astype(np.float64))], axis=0)
    cos_t = np.tile(cos, (1, 4)).astype(np.float32)
    sin_t = np.tile(np.concatenate([-sin, sin], axis=1), (1, 2)).astype(np.float32)
    return jnp.asarray(cos_t), jnp.asarray(sin_t)


_MAIN_COLS = ((0, 2048), (3616, 4640), (2048, 3584), (4640, 6176))
_DT_COLS = (3584, 3616)


def _pad_lanes(v, n=128):
    v = v.reshape(1, -1)
    return jnp.pad(v, ((0, 0), (0, n - v.shape[1])))


def kernel(x, c, ctx, c_ctx, w_ada, b_ada, w_in, w_out, attn_q_norm, attn_k_norm,
           lambda_q1, lambda_k1, lambda_q2, lambda_k2, attn_subln,
           ssd_conv_w, ssd_conv_b, ssd_dt_bias, ssd_a_log, ssd_d, ssd_norm, ret_norm):
    bsz, seq, _ = x.shape
    depth = w_ada.shape[0]
    assert bsz + 1 <= 8 and ctx.shape[1] == N_CTX
    ctx_row = bsz
    xcat = jnp.concatenate([ctx, x], axis=1)
    cvec = jnp.zeros((8, D), F32).at[0:bsz].set(c).at[ctx_row].set(c_ctx)
    cos_t, sin_t = _rope_tables(seq)
    gsel = np.arange(512) // QK
    gmat = jnp.asarray((gsel[:, None] == gsel[None, :]).astype(np.float32), BF16)

    for layer in range(depth):
        last = layer == depth - 1
        lam_init = 0.8 - 0.6 * math.exp(-0.3 * layer)
        w_l = w_in[layer]
        w_all = jnp.concatenate([w_l[:, a:b] for a, b in _MAIN_COLS + (_DT_COLS,)]
                                + [jnp.zeros((D, 128 - (_DT_COLS[1] - _DT_COLS[0])), F32)],
                                axis=1).astype(BF16)

        ada = _ada(cvec, w_ada[layer], b_ada[layer].reshape(1, -1))
        main, dtr, vt = _inproj(xcat, ada, w_all, cos_t, sin_t,
                                jnp.tile(attn_q_norm[layer], 8).reshape(1, 512),
                                jnp.tile(attn_k_norm[layer], 8).reshape(1, 512), gmat,
                                ssd_conv_w[layer], ssd_conv_b[layer].reshape(1, -1), ctx_row)

        lam_vec = jnp.stack([lambda_q1[layer], lambda_k1[layer], lambda_q2[layer], lambda_k2[layer]])
        attn_o = _attn(main, vt, lam_vec, attn_subln[layer].reshape(128, 1), lam_init,
                       tq=N_CTX, tk=ATT_TK)

        bias = _pad_lanes(ssd_dt_bias[layer])
        alog = _pad_lanes(ssd_a_log[layer])
        fwd = _scan(dtr, main, cos_t, sin_t, bias, alog, backward=False)
        ssd_o, ret_o = _scan(dtr, main, cos_t, sin_t, bias, alog, backward=True, fwd=fwd,
                             dskip=jnp.repeat(ssd_d[layer], SSD_P).reshape(1, 1024),
                             ssd_gain=ssd_norm[layer].reshape(1, 1024),
                             ret_gain=ret_norm[layer].reshape(1, 128))

        xcat = _outproj(xcat, attn_o, ssd_o, ret_o, ada, w_out[layer].astype(BF16), ctx_row,
                        latent_only=last)
    return xcat
```

```python
import functools
import math

import numpy as np
import jax
import jax.numpy as jnp
from jax import lax
from jax.experimental import pallas as pl
from jax.experimental.pallas import tpu as pltpu

F32 = jnp.float32
BF16 = jnp.bfloat16

D = 1024
N_CTX = 256
GRID_W = 64
EPS = 1e-6
ROPE_BASE = 10000.0
CHUNK = 128
QK = 64
ATT_H = 4
SSD_H = 16
SSD_G = 2
SSD_HPG = 8
SSD_P = 64
SSD_N = 128
RET_H = 4
RET_EXP_F = (5.0, 6.0, 7.0, 8.0)
RET_EXP_B = (5.5, 6.5, 7.5, 8.5)
N_MAIN = 6144
LOG2E = math.log2(math.e)
VMEM_LIMIT = 56 * 1024 * 1024
ATT_TK = 768
VT_ROWS = 144
_DONE = object()


def _silu(v):
    return v * (0.5 + 0.5 * jnp.tanh(0.5 * v))


def _split3(v):
    hi = v.astype(BF16)
    r1 = v - hi.astype(F32)
    mid = r1.astype(BF16)
    lo = (r1 - mid.astype(F32)).astype(BF16)
    return hi, mid, lo


def _dot(a, b):
    return jnp.dot(a, b, preferred_element_type=F32)


def _dot_nt(a, b):
    return lax.dot_general(a, b, (((1,), (1,)), ((), ())), preferred_element_type=F32)


def _tri_dot(tri, v):
    hi, mid, lo = _split3(v)
    return _dot(tri, hi) + _dot(tri, mid) + _dot(tri, lo)


def _ada_kernel(c_ref, w_ref, b_ref, o_ref):
    o_ref[...] = _dot(_silu(c_ref[...]), w_ref[...]) + b_ref[...]


def _ada(cvec, w, b):
    n = w.shape[1]
    tn = 768
    return pl.pallas_call(
        _ada_kernel,
        grid=(n // tn,),
        in_specs=[pl.BlockSpec((8, D), lambda j: (0, 0)),
                  pl.BlockSpec((D, tn), lambda j: (0, j)),
                  pl.BlockSpec((1, tn), lambda j: (0, j))],
        out_specs=pl.BlockSpec((8, tn), lambda j: (0, j)),
        out_shape=jax.ShapeDtypeStruct((8, n), F32),
        name="ada",
    )(cvec, w, b)


def _row_mod(ada_ref, b, rows, lo, hi, ctx_row):
    vb = ada_ref[pl.ds(b, 1), lo:hi]
    vc = ada_ref[ctx_row:ctx_row + 1, lo:hi]
    return jnp.where(rows < N_CTX, vc, vb)


def _rope128(x, cos, sin, lo_half):
    partner = jnp.where(lo_half, pltpu.roll(x, 96, 1), pltpu.roll(x, 32, 1))
    return x * cos + partner * sin


C_Q, C_K, C_V, C_XBC = 0, 512, 1024, 3072
PLAIN_COLS = ((1536, 3072), (4608, 6144))


def _inproj_kernel(x_ref, xp_ref, xn_ref, ada_ref, w_ref, cos_ref, sin_ref,
                   gq_ref, gk_ref, gmat_ref, cw_ref, cb_ref,
                   main_ref, dt_ref, vt_ref, *, tm, t, ctx_row):
    b = pl.program_id(0)
    i = pl.program_id(1)

    def modulated(x, rows):
        xn = x * lax.rsqrt(jnp.mean(x * x, axis=-1, keepdims=True) + EPS)
        shift = _row_mod(ada_ref, b, rows, 0, D, ctx_row)
        scale = _row_mod(ada_ref, b, rows, D, 2 * D, ctx_row)
        return xn * (1.0 + scale) + shift

    loc = lax.broadcasted_iota(jnp.int32, (tm, 1), 0)
    row = i * tm + loc
    h = modulated(x_ref[...], row).astype(BF16)

    halo = lax.broadcasted_iota(jnp.int32, (8, 1), 0)
    h_halo = jnp.concatenate([modulated(xp_ref[...], i * tm - 8 + halo),
                              modulated(xn_ref[...], (i + 1) * tm + halo)], axis=0)
    h_ext = jnp.concatenate([h, h_halo.astype(BF16)], axis=0)
    seq_start = (row == 0) | (row == N_CTX)
    seq_end = (row == t - 1) | (row == N_CTX - 1)

    cos = cos_ref[...]
    sin = sin_ref[...]
    lane = lax.broadcasted_iota(jnp.int32, (tm, 128), 1)
    lo_half = (lane % QK) < (QK // 2)

    def proj(c0, width=512):
        return _dot(h, w_ref[:, c0:c0 + width])

    def conv_mm(c0):
        res = _dot(h_ext, w_ref[:, c0:c0 + 512])
        return res[0:tm], res[tm + 7:tm + 8], res[tm + 8:tm + 9]

    def conv_epilogue(c0, res):
        xbc, prev, nxt = res
        xm1 = jnp.where(loc == 0, prev, pltpu.roll(xbc, 1, 0))
        xm1 = jnp.where(seq_start, 0.0, xm1)
        xp1 = jnp.where(loc == tm - 1, nxt, pltpu.roll(xbc, tm - 1, 0))
        xp1 = jnp.where(seq_end, 0.0, xp1)
        cc = c0 - C_XBC
        cw = cw_ref[:, cc:cc + 512]
        y = cw[0:1] * xm1 + cw[1:2] * xbc + cw[2:3] * xp1 + cb_ref[:, cc:cc + 512]
        main_ref[:, c0:c0 + 512] = _silu(y).astype(BF16)

    def qk_epilogue(c0, gain_ref, out_scale, y):
        ss = _dot((y * y).astype(BF16), gmat_ref[...])
        y = y * lax.rsqrt(ss * (1.0 / QK) + EPS) * gain_ref[...]
        for g in range(4):
            yg = _rope128(y[:, g * 128:(g + 1) * 128], cos, sin, lo_half)
            main_ref[:, c0 + g * 128:c0 + (g + 1) * 128] = (yg * out_scale).astype(BF16)

    def v_epilogue(v):
        main_ref[:, C_V:C_V + 512] = v.astype(BF16)
        ones_rows = (lax.broadcasted_iota(jnp.int32, (VT_ROWS - 128, tm), 0) == 0).astype(BF16)
        for hd in range(ATT_H):
            vt_ref[hd, 0:128, :] = v[:, hd * 128:(hd + 1) * 128].T.astype(BF16)
            vt_ref[hd, 128:VT_ROWS, :] = ones_rows

    def plain_epilogue(c0, y):
        main_ref[:, c0:c0 + 512] = y.astype(BF16)

    def dt_epilogue(y):
        dt_ref[...] = y

    heavy = [(functools.partial(conv_mm, c0), functools.partial(conv_epilogue, c0))
             for c0 in range(C_XBC, C_XBC + 1536, 512)]
    heavy += [(functools.partial(proj, C_Q),
               functools.partial(qk_epilogue, C_Q, gq_ref, (QK ** -0.5) * LOG2E)),
              (functools.partial(proj, C_K), functools.partial(qk_epilogue, C_K, gk_ref, 1.0)),
              (functools.partial(proj, C_V), v_epilogue)]
    light = [(functools.partial(proj, c0), functools.partial(plain_epilogue, c0))
             for lo, hi in PLAIN_COLS for c0 in range(lo, hi, 512)]
    light.append((functools.partial(proj, N_MAIN, 128), dt_epilogue))
    stages = []
    while heavy or light:
        if heavy:
            stages.append(heavy.pop(0))
        if light:
            stages.append(light.pop(0))
    pending = None
    for mm, epilogue in stages:
        res = mm()
        if pending is not None:
            pending()
        pending = functools.partial(epilogue, res)
    pending()


def _inproj(xcat, ada, w_all, layer, cos_t, sin_t, gq, gk, gmat, conv_w, conv_b, ctx_row):
    bsz, t, _ = xcat.shape
    tm = 384
    assert ATT_TK % tm == 0
    per_tile = ATT_TK // tm
    r8 = tm // 8
    last8 = t // 8 - 1
    const = lambda b, i: (0, 0)
    return pl.pallas_call(
        functools.partial(_inproj_kernel, tm=tm, t=t, ctx_row=ctx_row),
        grid=(bsz, t // tm),
        in_specs=[pl.BlockSpec((None, tm, D), lambda b, i: (b, i, 0)),
                  pl.BlockSpec((None, 8, D), lambda b, i: (b, jnp.maximum(i * r8 - 1, 0), 0)),
                  pl.BlockSpec((None, 8, D), lambda b, i: (b, jnp.minimum((i + 1) * r8, last8), 0)),
                  pl.BlockSpec((8, 3 * D), const),
                  pl.BlockSpec((None, D, N_MAIN + 128), lambda b, i: (layer, 0, 0)),
                  pl.BlockSpec((tm, 128), lambda b, i: (i, 0)),
                  pl.BlockSpec((tm, 128), lambda b, i: (i, 0)),
                  pl.BlockSpec((1, 512), const),
                  pl.BlockSpec((1, 512), const),
                  pl.BlockSpec((512, 512), const),
                  pl.BlockSpec((3, 1536), const),
                  pl.BlockSpec((1, 1536), const)],
        out_specs=[pl.BlockSpec((None, tm, N_MAIN), lambda b, i: (b, i, 0)),
                   pl.BlockSpec((None, tm, 128), lambda b, i: (b, i, 0)),
                   pl.BlockSpec((None, ATT_H, None, VT_ROWS, tm),
                                lambda b, i: (b, 0, i // per_tile, 0, i % per_tile))],
        out_shape=[jax.ShapeDtypeStruct((bsz, t, N_MAIN), BF16),
                   jax.ShapeDtypeStruct((bsz, t, 128), F32),
                   jax.ShapeDtypeStruct((bsz, ATT_H, t // ATT_TK, VT_ROWS, ATT_TK), BF16)],
        compiler_params=pltpu.CompilerParams(
            dimension_semantics=("arbitrary", "arbitrary"), vmem_limit_bytes=VMEM_LIMIT),
        name="inproj",
    )(xcat, xcat, xcat, ada, w_all, cos_t, sin_t, gq, gk, gmat, conv_w, conv_b)


def _attn_kernel(q_ref, qn_ref, k_ref, vt_ref, gate_ref, lam_ref, subln_ref, o_ref,
                 q2_ref, m_ref, acc_ref, s_ref, mt_ref, *, tq, tk, nkv, lam_init):
    qi = pl.program_id(2)
    lane = lax.broadcasted_iota(jnp.int32, (tq, 128), 1)
    m_ref[...] = jnp.full((1, 2 * tq), -jnp.inf, F32)
    acc_ref[...] = jnp.zeros((VT_ROWS, 2 * tq), F32)

    def stack_maps(src_ref):
        q = src_ref[...]
        zero = jnp.zeros_like(q)
        q2_ref[0:tq, :] = jnp.where(lane < QK, q, zero)
        q2_ref[tq:2 * tq, :] = jnp.where(lane >= QK, q, zero)

    def hand_off():
        stack_maps(qn_ref)
        scores(0, 2, False)

    def scores(j, slot, ctx_only):
        st = _dot_nt(k_ref[pl.ds(j * tk, tk), :], q2_ref[...])
        if ctx_only:
            krow = j * tk + lax.broadcasted_iota(jnp.int32, (tk, 1), 0)
            st = jnp.where(krow < N_CTX, st, -jnp.inf)
        s_ref[slot] = st
        mt_ref[slot] = jnp.max(st, axis=0, keepdims=True)

    def consume(j, slot):
        m_old = m_ref[...]
        m_new = jnp.maximum(m_old, mt_ref[slot])
        alpha = jnp.exp2(m_old - m_new)
        p = jnp.exp2(s_ref[slot] - m_new)
        acc_ref[...] = alpha * acc_ref[...] + _dot(vt_ref[j], p.astype(BF16))
        m_ref[...] = m_new

    def finalize():
        lam_v = lam_ref[...]
        lam = (jnp.exp(jnp.sum(lam_v[0:1] * lam_v[1:2], axis=-1, keepdims=True))
               - jnp.exp(jnp.sum(lam_v[2:3] * lam_v[3:4], axis=-1, keepdims=True)) + lam_init)
        inv = 1.0 / acc_ref[128:129, :]
        acc = acc_ref[0:128, :]
        o = acc[:, 0:tq] * inv[:, 0:tq] - lam * (acc[:, tq:2 * tq] * inv[:, tq:2 * tq])
        o = o * lax.rsqrt(jnp.mean(o * o, axis=0, keepdims=True) + EPS)
        o = o * subln_ref[...] * (1.0 - lam_init)
        o_ref[...] = (o.T * _silu(gate_ref[...].astype(F32))).astype(BF16)

    @pl.when(qi == 0)
    def _():
        stack_maps(q_ref)
        scores(0, 0, True)
        hand_off()
        consume(0, 0)
        finalize()

    @pl.when(qi != 0)
    def _():
        for j in range(nkv):
            if j + 1 < nkv:
                scores(j + 1, (j + 1) % 2, False)
            else:
                hand_off()
            consume(j, 2 if j == 0 else j % 2)
        finalize()


def _attn(main, vt, lam_vec, subln, lam_init, tq, tk):
    bsz, t, _ = main.shape
    nkv = t // tk
    assert tq == N_CTX and tk >= N_CTX and nkv >= 2
    nq = t // tq
    return pl.pallas_call(
        functools.partial(_attn_kernel, tq=tq, tk=tk, nkv=nkv, lam_init=lam_init),
        grid=(bsz, ATT_H, nq),
        in_specs=[pl.BlockSpec((None, tq, 128), lambda b, h, i: (b, i, h)),
                  pl.BlockSpec((None, tq, 128), lambda b, h, i: (b, jnp.minimum(i + 1, nq - 1), h)),
                  pl.BlockSpec((None, t, 128), lambda b, h, i: (b, 0, C_K // 128 + h)),
                  pl.BlockSpec((None, None, nkv, VT_ROWS, tk), lambda b, h, i: (b, h, 0, 0, 0)),
                  pl.BlockSpec((None, tq, 128), lambda b, h, i: (b, i, 12 + h)),
                  pl.BlockSpec((4, QK), lambda b, h, i: (0, 0)),
                  pl.BlockSpec((128, 1), lambda b, h, i: (0, 0))],
        out_specs=pl.BlockSpec((None, tq, 128), lambda b, h, i: (b, i, h)),
        out_shape=jax.ShapeDtypeStruct((bsz, t, 512), BF16),
        scratch_shapes=[pltpu.VMEM((2 * tq, 128), BF16),
                        pltpu.VMEM((1, 2 * tq), F32),
                        pltpu.VMEM((VT_ROWS, 2 * tq), F32),
                        pltpu.VMEM((3, tk, 2 * tq), F32),
                        pltpu.VMEM((3, 1, 2 * tq), F32)],
        compiler_params=pltpu.CompilerParams(
            dimension_semantics=("arbitrary", "arbitrary", "arbitrary"),
            vmem_limit_bytes=VMEM_LIMIT),
        name="attn",
    )(main, main, main, vt, main, lam_vec, subln)


def _chunk_index(s, nchunks, backward):
    if not backward:
        return s
    nctx = N_CTX // CHUNK
    return jnp.where(s < nctx, nctx - 1 - s, nchunks - 1 + nctx - s)


def _tri_mask(backward):
    r = lax.broadcasted_iota(jnp.int32, (CHUNK, CHUNK), 0)
    c = lax.broadcasted_iota(jnp.int32, (CHUNK, CHUNK), 1)
    return (c >= r) if backward else (c <= r)


def _ssd_chunk(u_ref, dtr_ref, bias_ref, alog_ref, o_ref, s_ref, fin, backward):
    mask = _tri_mask(backward)
    tri = jnp.where(mask, 1.0, 0.0).astype(BF16)
    pre = dtr_ref[...] + bias_ref[...]
    dt = jnp.maximum(pre, 0.0) + jnp.log1p(jnp.exp(-jnp.abs(pre)))
    la = dt * (-LOG2E * jnp.exp(alog_ref[...]))
    cum = _tri_dot(tri, la)
    cum_t = cum.T
    dt_t = dt.T
    edge = 0 if backward else CHUNK - 1
    etot = jnp.exp2(cum[edge:edge + 1, :])
    wdt_t = jnp.exp2(cum_t[:, edge:edge + 1] - cum_t) * dt_t
    first = lax.broadcasted_iota(jnp.int32, (CHUNK, 128), 1) < SSD_P
    first_row = lax.broadcasted_iota(jnp.int32, (1, 128), 1) < SSD_P
    col0 = SSD_H if backward else 0
    yield

    for g in range(SSD_G):
        k = u_ref[:, 1024 + g * SSD_N:1024 + (g + 1) * SSD_N]
        q = u_ref[:, 1280 + g * SSD_N:1280 + (g + 1) * SSD_N]
        scores = _dot_nt(q, k)
        k_t = k.astype(F32).T
        y_inter = _dot(q, s_ref[g].astype(BF16))
        ys = []
        yield
        for pp in range(SSD_HPG // 2):
            h0 = g * SSD_HPG + 2 * pp
            c0 = col0 + h0
            off = h0 * SSD_P
            xs = u_ref[:, off:off + 128]
            zero = jnp.zeros_like(xs)
            vals = jnp.concatenate([jnp.where(first, xs, zero),
                                    jnp.where(first, zero, xs)], axis=0)
            wts, kws, ecs = [], [], []
            for hh in range(2):
                c = c0 + hh
                cum_c = jnp.broadcast_to(cum[:, c:c + 1], (CHUNK, CHUNK))
                dec = jnp.exp2(jnp.where(mask, cum_c - cum_t[c:c + 1, :], -jnp.inf))
                wts.append((scores * dec * dt_t[c:c + 1, :]).astype(BF16))
                kws.append((k_t * wdt_t[c:c + 1, :]).astype(BF16))
                ecs.append(jnp.exp2(cum_c))
            y = (jnp.where(first, ecs[0], ecs[1]) * y_inter[:, pp * 128:(pp + 1) * 128]
                 + _dot(jnp.concatenate(wts, axis=1), vals))
            etot_p = jnp.where(first_row, jnp.broadcast_to(etot[:, c0:c0 + 1], (1, 128)),
                               jnp.broadcast_to(etot[:, c0 + 1:c0 + 2], (1, 128)))
            s_old = s_ref[g, :, pp * 128:(pp + 1) * 128]
            s_ref[g, :, pp * 128:(pp + 1) * 128] = (
                etot_p * s_old + _dot(jnp.concatenate(kws, axis=1), vals))
            if backward:
                z_ref, yf_ref, dskip_ref = fin[0:3]
                y = y + yf_ref[:, off:off + 128] + xs.astype(F32) * dskip_ref[:, off:off + 128]
                y = y * _silu(z_ref[:, off:off + 128].astype(F32))
                ys.append(y)
            else:
                o_ref[:, off:off + 128] = y
            yield
        if backward:
            gain_ref = fin[3]
            lo, hi = g * 512, (g + 1) * 512
            yg = jnp.concatenate(ys, axis=1)
            yg = yg * lax.rsqrt(jnp.mean(yg * yg, axis=-1, keepdims=True) + EPS)
            o_ref[:, lo:hi] = (yg * gain_ref[:, lo:hi]).astype(o_ref.dtype)


def _ret_tables(backward):
    exps = RET_EXP_B if backward else RET_EXP_F
    lg = np.log1p(-np.exp2(-np.asarray(exps, np.float64)))
    i = np.arange(CHUNK, dtype=np.float64)
    if backward:
        cum = (CHUNK - i)[None, :] * lg[:, None]
        tot = cum[:, 0]
        msk = i[None, :] >= i[:, None]
    else:
        cum = (i + 1.0)[None, :] * lg[:, None]
        tot = cum[:, -1]
        msk = i[None, :] <= i[:, None]
    dec = np.where(msk[None], np.exp(cum[:, :, None] - cum[:, None, :]), 0.0)
    inter = np.broadcast_to(np.exp(cum)[:, :, None], (RET_H, CHUNK, 128))
    toend = np.broadcast_to(np.exp(tot[:, None] - cum)[:, :, None], (RET_H, CHUNK, 128))
    tables = np.stack([dec, inter, toend], axis=1).astype(np.float32)
    return tables, [float(np.exp(v)) for v in tot]


def _ret_chunk(qk_ref, v_ref, cos, sin, tab_ref, o_ref, s_ref, fin, backward, etot):
    lane = lax.broadcasted_iota(jnp.int32, (CHUNK, 128), 1)
    lo_half = (lane % QK) < (QK // 2)
    first = lane < QK
    srow = lax.broadcasted_iota(jnp.int32, (128, 256), 0) < QK
    scol = lax.broadcasted_iota(jnp.int32, (128, 256), 1) < 128
    diag = srow == scol
    for pp in range(RET_H // 2):
        h0 = 2 * pp
        q = _rope128(qk_ref[:, pp * 128:(pp + 1) * 128].astype(F32), cos, sin, lo_half)
        k = _rope128(qk_ref[:, 256 + pp * 128:256 + (pp + 1) * 128].astype(F32) * (QK ** -0.5),
                     cos, sin, lo_half)
        qb = q.astype(BF16)
        kb = k.astype(BF16)
        k_t = kb.astype(F32).T.astype(BF16)
        zero = jnp.zeros_like(kb)
        k2 = jnp.concatenate([jnp.where(first, kb, zero),
                              jnp.where(first, zero, kb)], axis=0)
        scores = _dot_nt(qb, k2)
        v2 = v_ref[:, h0 * 128:(h0 + 2) * 128]
        zv = jnp.zeros((CHUNK, 128), BF16)
        vdiag = jnp.concatenate(
            [jnp.concatenate([v2[:, 0:128], zv], axis=1),
             jnp.concatenate([zv, v2[:, 128:256]], axis=1)], axis=0)
        dec = jnp.concatenate([tab_ref[h0, 0], tab_ref[h0 + 1, 0]], axis=1)
        inter = jnp.concatenate([tab_ref[h0, 1], tab_ref[h0 + 1, 1]], axis=1)
        toend = jnp.concatenate([tab_ref[h0, 2], tab_ref[h0 + 1, 2]], axis=1)
        s_pair = s_ref[pp]
        y = _dot((scores * dec).astype(BF16), vdiag) + inter * _dot(qb, s_pair.astype(BF16))
        upd = _dot(k_t, (v2.astype(F32) * toend).astype(BF16))
        s_ref[pp] = (jnp.where(scol, etot[h0], etot[h0 + 1]) * s_pair
                     + jnp.where(diag, upd, 0.0))
        for hh in range(2):
            h = h0 + hh
            yh = y[:, hh * 128:(hh + 1) * 128]
            if backward:
                g_ref, yf_ref, gain_ref = fin
                yh = yh + yf_ref[:, h * 128:(h + 1) * 128]
                yh = yh * lax.rsqrt(jnp.mean(yh * yh, axis=-1, keepdims=True) + EPS) * gain_ref[...]
                yh = yh * _silu(g_ref[:, h * 128:(h + 1) * 128].astype(F32))
            o_ref[:, h * 128:(h + 1) * 128] = yh.astype(o_ref.dtype)
        yield


def _scan_kernel(*refs, backward, etot, nb):
    if backward:
        (u_ref, dtr_ref, bias_ref, alog_ref, qk_ref, v_ref, cos_ref, sin_ref, tab_ref,
         z_ref, ssd_f_ref, dskip_ref, ssd_gain_ref, g_ref, ret_f_ref, ret_gain_ref,
         ssd_o_ref, ret_o_ref, ssd_s_ref, ret_s_ref) = refs
    else:
        (u_ref, dtr_ref, bias_ref, alog_ref, qk_ref, v_ref, cos_ref, sin_ref, tab_ref,
         ssd_o_ref, ret_o_ref, ssd_s_ref, ret_s_ref) = refs

    @pl.when(pl.program_id(0) == 0)
    def _():
        ssd_s_ref[...] = jnp.zeros_like(ssd_s_ref)
        ret_s_ref[...] = jnp.zeros_like(ret_s_ref)

    cos = cos_ref[...]
    sin = sin_ref[...]
    chains = []
    for b in range(nb):
        ssd_fin = (z_ref.at[b], ssd_f_ref.at[b], dskip_ref, ssd_gain_ref) if backward else None
        ret_fin = (g_ref.at[b], ret_f_ref.at[b], ret_gain_ref) if backward else None
        chains.append(_ssd_chunk(u_ref.at[b], dtr_ref.at[b], bias_ref, alog_ref,
                                 ssd_o_ref.at[b], ssd_s_ref.at[b], ssd_fin, backward))
        chains.append(_ret_chunk(qk_ref.at[b], v_ref.at[b], cos, sin, tab_ref,
                                 ret_o_ref.at[b], ret_s_ref.at[b], ret_fin, backward, etot))
    while chains:
        chains = [c for c in chains if next(c, _DONE) is not _DONE]


def _scan(dtr, main, cos_t, sin_t, bias, alog, backward, fwd=None, dskip=None,
          ssd_gain=None, ret_gain=None):
    bsz, t, _ = main.shape
    nchunks = t // CHUNK
    tables, etot = _ret_tables(backward)
    idx = functools.partial(_chunk_index, nchunks=nchunks, backward=backward)
    row = lambda s: (0, idx(s), 0)
    const = lambda s: (0, 0)
    in_specs = [pl.BlockSpec((bsz, CHUNK, 1536), lambda s: (0, idx(s), C_XBC // 1536)),
                pl.BlockSpec((bsz, CHUNK, 128), row),
                pl.BlockSpec((1, 128), const),
                pl.BlockSpec((1, 128), const),
                pl.BlockSpec((bsz, CHUNK, 512), lambda s: (0, idx(s), 9)),
                pl.BlockSpec((bsz, CHUNK, 512), lambda s: (0, idx(s), 10)),
                pl.BlockSpec((CHUNK, 128), lambda s: (idx(s), 0)),
                pl.BlockSpec((CHUNK, 128), lambda s: (idx(s), 0)),
                pl.BlockSpec((RET_H, 3, CHUNK, 128), lambda s: (0, 0, 0, 0))]
    args = [main, dtr, bias, alog, main, main, cos_t, sin_t, jnp.asarray(tables)]
    if backward:
        ssd_f, ret_f = fwd
        in_specs += [pl.BlockSpec((bsz, CHUNK, 1024), lambda s: (0, idx(s), 2)),
                     pl.BlockSpec((bsz, CHUNK, 1024), row),
                     pl.BlockSpec((1, 1024), const),
                     pl.BlockSpec((1, 1024), const),
                     pl.BlockSpec((bsz, CHUNK, 512), lambda s: (0, idx(s), 11)),
                     pl.BlockSpec((bsz, CHUNK, 512), row),
                     pl.BlockSpec((1, 128), const)]
        args += [main, ssd_f, dskip, ssd_gain, main, ret_f, ret_gain]
    out_dtype = BF16 if backward else F32
    return pl.pallas_call(
        functools.partial(_scan_kernel, backward=backward, etot=etot, nb=bsz),
        grid=(nchunks,),
        in_specs=in_specs,
        out_specs=[pl.BlockSpec((bsz, CHUNK, 1024), row),
                   pl.BlockSpec((bsz, CHUNK, 512), row)],
        out_shape=[jax.ShapeDtypeStruct((bsz, t, 1024), out_dtype),
                   jax.ShapeDtypeStruct((bsz, t, 512), out_dtype)],
        scratch_shapes=[pltpu.VMEM((bsz, SSD_G, SSD_N, 512), F32),
                        pltpu.VMEM((bsz, RET_H // 2, 128, 256), F32)],
        compiler_params=pltpu.CompilerParams(
            dimension_semantics=("arbitrary",), vmem_limit_bytes=VMEM_LIMIT),
        name="scan_bwd" if backward else "scan_fwd",
    )(*args)


def _outproj_kernel(x_ref, a_ref, s_ref, r_ref, ada_ref, w_ref, o_ref, *, tm, blk0, ctx_row):
    b = pl.program_id(0)
    i = pl.program_id(1) + blk0
    acc = _dot(a_ref[...], w_ref[0:512, :])
    acc += _dot(s_ref[...], w_ref[512:1536, :])
    acc += _dot(r_ref[...], w_ref[1536:2048, :])
    rows = i * tm + lax.broadcasted_iota(jnp.int32, (tm, 1), 0)
    gate = _row_mod(ada_ref, b, rows, 2 * D, 3 * D, ctx_row)
    o_ref[...] = x_ref[...] + gate * acc


def _outproj(xcat, attn_o, ssd_o, ret_o, ada, w_out, layer, ctx_row, latent_only):
    bsz, t, _ = xcat.shape
    tm = N_CTX if latent_only else 768
    blk0 = 1 if latent_only else 0
    nb = t // tm - blk0
    row = lambda b, i: (b, i + blk0, 0)
    return pl.pallas_call(
        functools.partial(_outproj_kernel, tm=tm, blk0=blk0, ctx_row=ctx_row),
        grid=(bsz, nb),
        in_specs=[pl.BlockSpec((None, tm, D), row),
                  pl.BlockSpec((None, tm, 512), row),
                  pl.BlockSpec((None, tm, 1024), row),
                  pl.BlockSpec((None, tm, 512), row),
                  pl.BlockSpec((8, 3 * D), lambda b, i: (0, 0)),
                  pl.BlockSpec((None, 2 * D, D), lambda b, i: (layer, 0, 0))],
        out_specs=pl.BlockSpec((None, tm, D), lambda b, i: (b, i, 0)),
        out_shape=jax.ShapeDtypeStruct((bsz, nb * tm, D), F32),
        compiler_params=pltpu.CompilerParams(
            dimension_semantics=("arbitrary", "arbitrary"), vmem_limit_bytes=VMEM_LIMIT),
        name="outproj",
    )(xcat, attn_o, ssd_o, ret_o, ada, w_out)


def _rope_tables(seq):
    n_rows = seq // GRID_W
    row = np.repeat(np.arange(n_rows, dtype=np.float32), GRID_W)
    col = np.tile(np.arange(GRID_W, dtype=np.float32), n_rows)
    n_freq = QK // 4
    inv_freq = (np.float32(ROPE_BASE) ** (-np.arange(n_freq, dtype=np.float32) / n_freq)).astype(np.float32)
    ang = np.concatenate([row[:, None] * inv_freq, col[:, None] * inv_freq], axis=-1).astype(np.float32)
    cos = np.concatenate([np.ones((N_CTX, QK // 2)), np.cos(ang.astype(np.float64))], axis=0)
    sin = np.concatenate([np.zeros((N_CTX, QK // 2)), np.sin(ang.astype(np.float64))], axis=0)
    cos_t = np.tile(cos, (1, 4)).astype(np.float32)
    sin_t = np.tile(np.concatenate([-sin, sin], axis=1), (1, 2)).astype(np.float32)
    return jnp.asarray(cos_t), jnp.asarray(sin_t)


_MAIN_COLS = ((0, 2048), (3616, 4640), (2048, 3584), (4640, 6176))
_DT_COLS = (3584, 3616)


def _pad_lanes(v, n=128):
    v = v.reshape(1, -1)
    return jnp.pad(v, ((0, 0), (0, n - v.shape[1])))


def kernel(x, c, ctx, c_ctx, w_ada, b_ada, w_in, w_out, attn_q_norm, attn_k_norm,
           lambda_q1, lambda_k1, lambda_q2, lambda_k2, attn_subln,
           ssd_conv_w, ssd_conv_b, ssd_dt_bias, ssd_a_log, ssd_d, ssd_norm, ret_norm):
    bsz, seq, _ = x.shape
    depth = w_ada.shape[0]
    assert bsz + 1 <= 8 and ctx.shape[1] == N_CTX
    ctx_row = bsz
    xcat = jnp.concatenate([ctx, x], axis=1)
    cvec = jnp.zeros((8, D), F32).at[0:bsz].set(c).at[ctx_row].set(c_ctx)
    cos_t, sin_t = _rope_tables(seq)
    gsel = np.arange(512) // QK
    gmat = jnp.asarray((gsel[:, None] == gsel[None, :]).astype(np.float32), BF16)

    w_perm = jnp.concatenate(
        [w_in[:, :, a:b] for a, b in _MAIN_COLS + (_DT_COLS,)]
        + [jnp.zeros((depth, D, 128 - (_DT_COLS[1] - _DT_COLS[0])), F32)], axis=2).astype(BF16)
    w_out_bf = w_out.astype(BF16)

    for layer in range(depth):
        last = layer == depth - 1
        lam_init = 0.8 - 0.6 * math.exp(-0.3 * layer)

        ada = _ada(cvec, w_ada[layer], b_ada[layer].reshape(1, -1))
        main, dtr, vt = _inproj(xcat, ada, w_perm, layer, cos_t, sin_t,
                                jnp.tile(attn_q_norm[layer], 8).reshape(1, 512),
                                jnp.tile(attn_k_norm[layer], 8).reshape(1, 512), gmat,
                                ssd_conv_w[layer], ssd_conv_b[layer].reshape(1, -1), ctx_row)

        lam_vec = jnp.stack([lambda_q1[layer], lambda_k1[layer], lambda_q2[layer], lambda_k2[layer]])
        attn_o = _attn(main, vt, lam_vec, attn_subln[layer].reshape(128, 1), lam_init,
                       tq=N_CTX, tk=ATT_TK)

        bias = _pad_lanes(ssd_dt_bias[layer])
        alog = _pad_lanes(ssd_a_log[layer])
        fwd = _scan(dtr, main, cos_t, sin_t, bias, alog, backward=False)
        ssd_o, ret_o = _scan(dtr, main, cos_t, sin_t, bias, alog, backward=True, fwd=fwd,
                             dskip=jnp.repeat(ssd_d[layer], SSD_P).reshape(1, 1024),
                             ssd_gain=ssd_norm[layer].reshape(1, 1024),
                             ret_gain=ret_norm[layer].reshape(1, 128))

        xcat = _outproj(xcat, attn_o, ssd_o, ret_o, ada, w_out_bf, layer, ctx_row,
                        latent_only=last)
    return xcat
```

```python
import functools
import math

import numpy as np
import jax
import jax.numpy as jnp
from jax import lax
from jax.experimental import pallas as pl
from jax.experimental.pallas import tpu as pltpu

F32 = jnp.float32
BF16 = jnp.bfloat16

D = 1024
N_CTX = 256
GRID_W = 64
EPS = 1e-6
ROPE_BASE = 10000.0
CHUNK = 128
QK = 64
ATT_H = 4
SSD_H = 16
SSD_G = 2
SSD_HPG = 8
SSD_P = 64
SSD_N = 128
RET_H = 4
RET_EXP_F = (5.0, 6.0, 7.0, 8.0)
RET_EXP_B = (5.5, 6.5, 7.5, 8.5)
N_MAIN = 6144
LOG2E = math.log2(math.e)
VMEM_LIMIT = 56 * 1024 * 1024
ATT_TK = 768
VT_ROWS = 144
_DONE = object()


def _silu(v):
    return v * (0.5 + 0.5 * jnp.tanh(0.5 * v))


def _split3(v):
    hi = v.astype(BF16)
    r1 = v - hi.astype(F32)
    mid = r1.astype(BF16)
    lo = (r1 - mid.astype(F32)).astype(BF16)
    return hi, mid, lo


def _dot(a, b):
    return jnp.dot(a, b, preferred_element_type=F32)


def _dot_nt(a, b):
    return lax.dot_general(a, b, (((1,), (1,)), ((), ())), preferred_element_type=F32)


def _tri_dot(tri, v):
    hi, mid, lo = _split3(v)
    return _dot(tri, hi) + _dot(tri, mid) + _dot(tri, lo)


def _ada_kernel(c_ref, w_ref, b_ref, o_ref):
    o_ref[...] = _dot(_silu(c_ref[...]), w_ref[...]) + b_ref[...]


def _ada(cvec, w, b):
    n = w.shape[1]
    tn = 768
    return pl.pallas_call(
        _ada_kernel,
        grid=(n // tn,),
        in_specs=[pl.BlockSpec((8, D), lambda j: (0, 0)),
                  pl.BlockSpec((D, tn), lambda j: (0, j)),
                  pl.BlockSpec((1, tn), lambda j: (0, j))],
        out_specs=pl.BlockSpec((8, tn), lambda j: (0, j)),
        out_shape=jax.ShapeDtypeStruct((8, n), F32),
        name="ada",
    )(cvec, w, b)


def _row_mod(ada_ref, b, rows, lo, hi, ctx_row):
    vb = ada_ref[pl.ds(b, 1), lo:hi]
    vc = ada_ref[ctx_row:ctx_row + 1, lo:hi]
    return jnp.where(rows < N_CTX, vc, vb)


def _rope128(x, cos, sin, lo_half):
    partner = jnp.where(lo_half, pltpu.roll(x, 96, 1), pltpu.roll(x, 32, 1))
    return x * cos + partner * sin


C_Q, C_K, C_V, C_XBC = 0, 512, 1024, 3072
PLAIN_COLS = ((1536, 3072), (4608, 6144))


def _inproj_kernel(x_ref, xp_ref, xn_ref, ada_ref, w_ref, cos_ref, sin_ref,
                   gq_ref, gk_ref, gmat_ref, cw_ref, cb_ref,
                   main_ref, dt_ref, vt_ref, *, tm, t, ctx_row):
    b = pl.program_id(0)
    i = pl.program_id(1)

    def modulated(x, rows):
        xn = x * lax.rsqrt(jnp.mean(x * x, axis=-1, keepdims=True) + EPS)
        shift = _row_mod(ada_ref, b, rows, 0, D, ctx_row)
        scale = _row_mod(ada_ref, b, rows, D, 2 * D, ctx_row)
        return xn * (1.0 + scale) + shift

    loc = lax.broadcasted_iota(jnp.int32, (tm, 1), 0)
    row = i * tm + loc
    h = modulated(x_ref[...], row).astype(BF16)

    halo = lax.broadcasted_iota(jnp.int32, (8, 1), 0)
    h_halo = jnp.concatenate([modulated(xp_ref[...], i * tm - 8 + halo),
                              modulated(xn_ref[...], (i + 1) * tm + halo)], axis=0)
    h_ext = jnp.concatenate([h, h_halo.astype(BF16)], axis=0)
    seq_start = (row == 0) | (row == N_CTX)
    seq_end = (row == t - 1) | (row == N_CTX - 1)

    cos = cos_ref[...]
    sin = sin_ref[...]
    lane = lax.broadcasted_iota(jnp.int32, (tm, 128), 1)
    lo_half = (lane % QK) < (QK // 2)

    def proj(c0, width=512):
        return _dot(h, w_ref[:, c0:c0 + width])

    def conv_mm(c0):
        res = _dot(h_ext, w_ref[:, c0:c0 + 512])
        return res[0:tm], res[tm + 7:tm + 8], res[tm + 8:tm + 9]

    def conv_epilogue(c0, res):
        xbc, prev, nxt = res
        xm1 = jnp.where(loc == 0, prev, pltpu.roll(xbc, 1, 0))
        xm1 = jnp.where(seq_start, 0.0, xm1)
        xp1 = jnp.where(loc == tm - 1, nxt, pltpu.roll(xbc, tm - 1, 0))
        xp1 = jnp.where(seq_end, 0.0, xp1)
        cc = c0 - C_XBC
        cw = cw_ref[:, cc:cc + 512]
        y = cw[0:1] * xm1 + cw[1:2] * xbc + cw[2:3] * xp1 + cb_ref[:, cc:cc + 512]
        main_ref[:, c0:c0 + 512] = _silu(y).astype(BF16)

    def qk_epilogue(c0, gain_ref, out_scale, y):
        ss = _dot((y * y).astype(BF16), gmat_ref[...])
        y = y * lax.rsqrt(ss * (1.0 / QK) + EPS) * gain_ref[...]
        for g in range(4):
            yg = _rope128(y[:, g * 128:(g + 1) * 128], cos, sin, lo_half)
            main_ref[:, c0 + g * 128:c0 + (g + 1) * 128] = (yg * out_scale).astype(BF16)

    def v_epilogue(v):
        main_ref[:, C_V:C_V + 512] = v.astype(BF16)
        ones_rows = (lax.broadcasted_iota(jnp.int32, (VT_ROWS - 128, tm), 0) == 0).astype(BF16)
        for hd in range(ATT_H):
            vt_ref[hd, 0:128, :] = v[:, hd * 128:(hd + 1) * 128].T.astype(BF16)
            vt_ref[hd, 128:VT_ROWS, :] = ones_rows

    def plain_epilogue(c0, y):
        main_ref[:, c0:c0 + 512] = y.astype(BF16)

    def dt_epilogue(y):
        dt_ref[...] = y

    heavy = [(functools.partial(conv_mm, c0), functools.partial(conv_epilogue, c0))
             for c0 in range(C_XBC, C_XBC + 1536, 512)]
    heavy += [(functools.partial(proj, C_Q),
               functools.partial(qk_epilogue, C_Q, gq_ref, (QK ** -0.5) * LOG2E)),
              (functools.partial(proj, C_K), functools.partial(qk_epilogue, C_K, gk_ref, 1.0)),
              (functools.partial(proj, C_V), v_epilogue)]
    light = [(functools.partial(proj, c0), functools.partial(plain_epilogue, c0))
             for lo, hi in PLAIN_COLS for c0 in range(lo, hi, 512)]
    light.append((functools.partial(proj, N_MAIN, 128), dt_epilogue))
    stages = []
    while heavy or light:
        if heavy:
            stages.append(heavy.pop(0))
        if light:
            stages.append(light.pop(0))
    pending = None
    for mm, epilogue in stages:
        res = mm()
        if pending is not None:
            pending()
        pending = functools.partial(epilogue, res)
    pending()


def _inproj(xcat, ada, w_all, layer, cos_t, sin_t, gq, gk, gmat, conv_w, conv_b, ctx_row):
    bsz, t, _ = xcat.shape
    tm = 384
    assert ATT_TK % tm == 0
    per_tile = ATT_TK // tm
    r8 = tm // 8
    last8 = t // 8 - 1
    const = lambda b, i: (0, 0)
    return pl.pallas_call(
        functools.partial(_inproj_kernel, tm=tm, t=t, ctx_row=ctx_row),
        grid=(bsz, t // tm),
        in_specs=[pl.BlockSpec((None, tm, D), lambda b, i: (b, i, 0)),
                  pl.BlockSpec((None, 8, D), lambda b, i: (b, jnp.maximum(i * r8 - 1, 0), 0)),
                  pl.BlockSpec((None, 8, D), lambda b, i: (b, jnp.minimum((i + 1) * r8, last8), 0)),
                  pl.BlockSpec((8, 3 * D), const),
                  pl.BlockSpec((None, D, N_MAIN + 128), lambda b, i: (layer, 0, 0)),
                  pl.BlockSpec((tm, 128), lambda b, i: (i, 0)),
                  pl.BlockSpec((tm, 128), lambda b, i: (i, 0)),
                  pl.BlockSpec((1, 512), const),
                  pl.BlockSpec((1, 512), const),
                  pl.BlockSpec((512, 512), const),
                  pl.BlockSpec((3, 1536), const),
                  pl.BlockSpec((1, 1536), const)],
        out_specs=[pl.BlockSpec((None, tm, N_MAIN), lambda b, i: (b, i, 0)),
                   pl.BlockSpec((None, tm, 128), lambda b, i: (b, i, 0)),
                   pl.BlockSpec((None, ATT_H, None, VT_ROWS, tm),
                                lambda b, i: (b, 0, i // per_tile, 0, i % per_tile))],
        out_shape=[jax.ShapeDtypeStruct((bsz, t, N_MAIN), BF16),
                   jax.ShapeDtypeStruct((bsz, t, 128), F32),
                   jax.ShapeDtypeStruct((bsz, ATT_H, t // ATT_TK, VT_ROWS, ATT_TK), BF16)],
        compiler_params=pltpu.CompilerParams(
            dimension_semantics=("arbitrary", "arbitrary"), vmem_limit_bytes=VMEM_LIMIT),
        name="inproj",
    )(xcat, xcat, xcat, ada, w_all, cos_t, sin_t, gq, gk, gmat, conv_w, conv_b)


def _attn_kernel(q_ref, qn_ref, k_ref, vt_ref, gate_ref, lam_ref, subln_ref, o_ref,
                 q2_ref, m_ref, acc_ref, s_ref, mt_ref, *, tq, tk, nkv, lam_init):
    qi = pl.program_id(2)
    lane = lax.broadcasted_iota(jnp.int32, (tq, 128), 1)
    m_ref[...] = jnp.full((1, 2 * tq), -jnp.inf, F32)
    acc_ref[...] = jnp.zeros((VT_ROWS, 2 * tq), F32)

    def stack_maps(src_ref):
        q = src_ref[...]
        zero = jnp.zeros_like(q)
        q2_ref[0:tq, :] = jnp.where(lane < QK, q, zero)
        q2_ref[tq:2 * tq, :] = jnp.where(lane >= QK, q, zero)

    def hand_off():
        stack_maps(qn_ref)
        scores(0, 2, False)

    def scores(j, slot, ctx_only):
        st = _dot_nt(k_ref[pl.ds(j * tk, tk), :], q2_ref[...])
        if ctx_only:
            krow = j * tk + lax.broadcasted_iota(jnp.int32, (tk, 1), 0)
            st = jnp.where(krow < N_CTX, st, -jnp.inf)
        s_ref[slot] = st
        mt_ref[slot] = jnp.max(st, axis=0, keepdims=True)

    def consume(j, slot):
        m_old = m_ref[...]
        m_new = jnp.maximum(m_old, mt_ref[slot])
        alpha = jnp.exp2(m_old - m_new)
        p = jnp.exp2(s_ref[slot] - m_new)
        acc_ref[...] = alpha * acc_ref[...] + _dot(vt_ref[j], p.astype(BF16))
        m_ref[...] = m_new

    def finalize():
        lam_v = lam_ref[...]
        lam = (jnp.exp(jnp.sum(lam_v[0:1] * lam_v[1:2], axis=-1, keepdims=True))
               - jnp.exp(jnp.sum(lam_v[2:3] * lam_v[3:4], axis=-1, keepdims=True)) + lam_init)
        inv = 1.0 / acc_ref[128:129, :]
        acc = acc_ref[0:128, :]
        o = acc[:, 0:tq] * inv[:, 0:tq] - lam * (acc[:, tq:2 * tq] * inv[:, tq:2 * tq])
        o = o * lax.rsqrt(jnp.mean(o * o, axis=0, keepdims=True) + EPS)
        o = o * subln_ref[...] * (1.0 - lam_init)
        o_ref[...] = (o.T * _silu(gate_ref[...].astype(F32))).astype(BF16)

    @pl.when(qi == 0)
    def _():
        stack_maps(q_ref)
        scores(0, 0, True)
        hand_off()
        consume(0, 0)
        finalize()

    @pl.when(qi != 0)
    def _():
        for j in range(nkv):
            if j + 1 < nkv:
                scores(j + 1, (j + 1) % 2, False)
            else:
                hand_off()
            consume(j, 2 if j == 0 else j % 2)
        finalize()


def _attn(main, vt, lam_vec, subln, lam_init, tq, tk):
    bsz, t, _ = main.shape
    nkv = t // tk
    assert tq == N_CTX and tk >= N_CTX and nkv >= 2
    nq = t // tq
    return pl.pallas_call(
        functools.partial(_attn_kernel, tq=tq, tk=tk, nkv=nkv, lam_init=lam_init),
        grid=(bsz, ATT_H, nq),
        in_specs=[pl.BlockSpec((None, tq, 128), lambda b, h, i: (b, i, h)),
                  pl.BlockSpec((None, tq, 128), lambda b, h, i: (b, jnp.minimum(i + 1, nq - 1), h)),
                  pl.BlockSpec((None, t, 128), lambda b, h, i: (b, 0, C_K // 128 + h)),
                  pl.BlockSpec((None, None, nkv, VT_ROWS, tk), lambda b, h, i: (b, h, 0, 0, 0)),
                  pl.BlockSpec((None, tq, 128), lambda b, h, i: (b, i, 12 + h)),
                  pl.BlockSpec((4, QK), lambda b, h, i: (0, 0)),
                  pl.BlockSpec((128, 1), lambda b, h, i: (0, 0))],
        out_specs=pl.BlockSpec((None, tq, 128), lambda b, h, i: (b, i, h)),
        out_shape=jax.ShapeDtypeStruct((bsz, t, 512), BF16),
        scratch_shapes=[pltpu.VMEM((2 * tq, 128), BF16),
                        pltpu.VMEM((1, 2 * tq), F32),
                        pltpu.VMEM((VT_ROWS, 2 * tq), F32),
                        pltpu.VMEM((3, tk, 2 * tq), F32),
                        pltpu.VMEM((3, 1, 2 * tq), F32)],
        compiler_params=pltpu.CompilerParams(
            dimension_semantics=("arbitrary", "arbitrary", "arbitrary"),
            vmem_limit_bytes=VMEM_LIMIT),
        name="attn",
    )(main, main, main, vt, main, lam_vec, subln)


def _chunk_index(s, nchunks, backward):
    if not backward:
        return s
    nctx = N_CTX // CHUNK
    return jnp.where(s < nctx, nctx - 1 - s, nchunks - 1 + nctx - s)


def _tri_mask(backward):
    r = lax.broadcasted_iota(jnp.int32, (CHUNK, CHUNK), 0)
    c = lax.broadcasted_iota(jnp.int32, (CHUNK, CHUNK), 1)
    return (c >= r) if backward else (c <= r)


def _ssd_chunk(u_ref, dtr_ref, bias_ref, alog_ref, o_ref, s_ref, fin, backward):
    mask = _tri_mask(backward)
    tri = jnp.where(mask, 1.0, 0.0).astype(BF16)
    pre = dtr_ref[...] + bias_ref[...]
    dt = jnp.maximum(pre, 0.0) + jnp.log1p(jnp.exp(-jnp.abs(pre)))
    la = dt * (-LOG2E * jnp.exp(alog_ref[...]))
    cum = _tri_dot(tri, la)
    cum_t = cum.T
    dt_t = dt.T
    edge = 0 if backward else CHUNK - 1
    etot = jnp.exp2(cum[edge:edge + 1, :])
    wdt_t = jnp.exp2(cum_t[:, edge:edge + 1] - cum_t) * dt_t
    first = lax.broadcasted_iota(jnp.int32, (CHUNK, 128), 1) < SSD_P
    first_row = lax.broadcasted_iota(jnp.int32, (1, 128), 1) < SSD_P
    col0 = SSD_H if backward else 0
    yield

    for g in range(SSD_G):
        k = u_ref[:, 1024 + g * SSD_N:1024 + (g + 1) * SSD_N]
        q = u_ref[:, 1280 + g * SSD_N:1280 + (g + 1) * SSD_N]
        scores = _dot_nt(q, k)
        k_t = k.astype(F32).T
        y_inter = _dot(q, s_ref[g].astype(BF16))
        ys = []
        yield
        for pp in range(SSD_HPG // 2):
            h0 = g * SSD_HPG + 2 * pp
            c0 = col0 + h0
            off = h0 * SSD_P
            xs = u_ref[:, off:off + 128]
            zero = jnp.zeros_like(xs)
            vals = jnp.concatenate([jnp.where(first, xs, zero),
                                    jnp.where(first, zero, xs)], axis=0)
            wts, kws, ecs = [], [], []
            for hh in range(2):
                c = c0 + hh
                cum_c = jnp.broadcast_to(cum[:, c:c + 1], (CHUNK, CHUNK))
                dec = jnp.exp2(jnp.where(mask, cum_c - cum_t[c:c + 1, :], -jnp.inf))
                wts.append((scores * dec * dt_t[c:c + 1, :]).astype(BF16))
                kws.append((k_t * wdt_t[c:c + 1, :]).astype(BF16))
                ecs.append(jnp.exp2(cum_c))
            y = (jnp.where(first, ecs[0], ecs[1]) * y_inter[:, pp * 128:(pp + 1) * 128]
                 + _dot(jnp.concatenate(wts, axis=1), vals))
            etot_p = jnp.where(first_row, jnp.broadcast_to(etot[:, c0:c0 + 1], (1, 128)),
                               jnp.broadcast_to(etot[:, c0 + 1:c0 + 2], (1, 128)))
            s_old = s_ref[g, :, pp * 128:(pp + 1) * 128]
            s_ref[g, :, pp * 128:(pp + 1) * 128] = (
                etot_p * s_old + _dot(jnp.concatenate(kws, axis=1), vals))
            if backward:
                z_ref, yf_ref, dskip_ref = fin[0:3]
                y = y + yf_ref[:, off:off + 128] + xs.astype(F32) * dskip_ref[:, off:off + 128]
                y = y * _silu(z_ref[:, off:off + 128].astype(F32))
                ys.append(y)
            else:
                o_ref[:, off:off + 128] = y
            yield
        if backward:
            gain_ref = fin[3]
            lo, hi = g * 512, (g + 1) * 512
            yg = jnp.concatenate(ys, axis=1)
            yg = yg * lax.rsqrt(jnp.mean(yg * yg, axis=-1, keepdims=True) + EPS)
            o_ref[:, lo:hi] = (yg * gain_ref[:, lo:hi]).astype(o_ref.dtype)


def _ret_tables(backward):
    exps = RET_EXP_B if backward else RET_EXP_F
    lg = np.log1p(-np.exp2(-np.asarray(exps, np.float64)))
    i = np.arange(CHUNK, dtype=np.float64)
    if backward:
        cum = (CHUNK - i)[None, :] * lg[:, None]
        tot = cum[:, 0]
        msk = i[None, :] >= i[:, None]
    else:
        cum = (i + 1.0)[None, :] * lg[:, None]
        tot = cum[:, -1]
        msk = i[None, :] <= i[:, None]
    dec = np.where(msk[None], np.exp(cum[:, :, None] - cum[:, None, :]), 0.0)
    inter = np.broadcast_to(np.exp(cum)[:, :, None], (RET_H, CHUNK, 128))
    toend = np.broadcast_to(np.exp(tot[:, None] - cum)[:, :, None], (RET_H, CHUNK, 128))
    tables = np.stack([dec, inter, toend], axis=1).astype(np.float32)
    return tables, [float(np.exp(v)) for v in tot]


def _ret_chunk(qk_ref, v_ref, cos, sin, tab_ref, o_ref, s_ref, fin, backward, etot):
    lane = lax.broadcasted_iota(jnp.int32, (CHUNK, 128), 1)
    lo_half = (lane % QK) < (QK // 2)
    first = lane < QK
    srow = lax.broadcasted_iota(jnp.int32, (128, 256), 0) < QK
    scol = lax.broadcasted_iota(jnp.int32, (128, 256), 1) < 128
    diag = srow == scol
    for pp in range(RET_H // 2):
        h0 = 2 * pp
        q = _rope128(qk_ref[:, pp * 128:(pp + 1) * 128].astype(F32), cos, sin, lo_half)
        k = _rope128(qk_ref[:, 256 + pp * 128:256 + (pp + 1) * 128].astype(F32) * (QK ** -0.5),
                     cos, sin, lo_half)
        qb = q.astype(BF16)
        kb = k.astype(BF16)
        k_t = kb.astype(F32).T.astype(BF16)
        zero = jnp.zeros_like(kb)
        k2 = jnp.concatenate([jnp.where(first, kb, zero),
                              jnp.where(first, zero, kb)], axis=0)
        scores = _dot_nt(qb, k2)
        v2 = v_ref[:, h0 * 128:(h0 + 2) * 128]
        zv = jnp.zeros((CHUNK, 128), BF16)
        vdiag = jnp.concatenate(
            [jnp.concatenate([v2[:, 0:128], zv], axis=1),
             jnp.concatenate([zv, v2[:, 128:256]], axis=1)], axis=0)
        dec = jnp.concatenate([tab_ref[h0, 0], tab_ref[h0 + 1, 0]], axis=1)
        inter = jnp.concatenate([tab_ref[h0, 1], tab_ref[h0 + 1, 1]], axis=1)
        toend = jnp.concatenate([tab_ref[h0, 2], tab_ref[h0 + 1, 2]], axis=1)
        s_pair = s_ref[pp]
        y = _dot((scores * dec).astype(BF16), vdiag) + inter * _dot(qb, s_pair.astype(BF16))
        upd = _dot(k_t, (v2.astype(F32) * toend).astype(BF16))
        s_ref[pp] = (jnp.where(scol, etot[h0], etot[h0 + 1]) * s_pair
                     + jnp.where(diag, upd, 0.0))
        for hh in range(2):
            h = h0 + hh
            yh = y[:, hh * 128:(hh + 1) * 128]
            if backward:
                g_ref, yf_ref, gain_ref = fin
                yh = yh + yf_ref[:, h * 128:(h + 1) * 128]
                yh = yh * lax.rsqrt(jnp.mean(yh * yh, axis=-1, keepdims=True) + EPS) * gain_ref[...]
                yh = yh * _silu(g_ref[:, h * 128:(h + 1) * 128].astype(F32))
            o_ref[:, h * 128:(h + 1) * 128] = yh.astype(o_ref.dtype)
        yield


def _scan_kernel(*refs, backward, etot, nb):
    if backward:
        (u_ref, dtr_ref, bias_ref, alog_ref, qk_ref, v_ref, cos_ref, sin_ref, tab_ref,
         z_ref, ssd_f_ref, dskip_ref, ssd_gain_ref, g_ref, ret_f_ref, ret_gain_ref,
         ssd_o_ref, ret_o_ref, ssd_s_ref, ret_s_ref) = refs
    else:
        (u_ref, dtr_ref, bias_ref, alog_ref, qk_ref, v_ref, cos_ref, sin_ref, tab_ref,
         ssd_o_ref, ret_o_ref, ssd_s_ref, ret_s_ref) = refs

    @pl.when(pl.program_id(0) == 0)
    def _():
        ssd_s_ref[...] = jnp.zeros_like(ssd_s_ref)
        ret_s_ref[...] = jnp.zeros_like(ret_s_ref)

    cos = cos_ref[...]
    sin = sin_ref[...]
    chains = []
    for b in range(nb):
        ssd_fin = (z_ref.at[b], ssd_f_ref.at[b], dskip_ref, ssd_gain_ref) if backward else None
        ret_fin = (g_ref.at[b], ret_f_ref.at[b], ret_gain_ref) if backward else None
        chains.append(_ssd_chunk(u_ref.at[b], dtr_ref.at[b], bias_ref, alog_ref,
                                 ssd_o_ref.at[b], ssd_s_ref.at[b], ssd_fin, backward))
        chains.append(_ret_chunk(qk_ref.at[b], v_ref.at[b], cos, sin, tab_ref,
                                 ret_o_ref.at[b], ret_s_ref.at[b], ret_fin, backward, etot))
    while chains:
        chains = [c for c in chains if next(c, _DONE) is not _DONE]


def _scan(dtr, main, cos_t, sin_t, bias, alog, backward, fwd=None, dskip=None,
          ssd_gain=None, ret_gain=None):
    bsz, t, _ = main.shape
    nchunks = t // CHUNK
    tables, etot = _ret_tables(backward)
    idx = functools.partial(_chunk_index, nchunks=nchunks, backward=backward)
    row = lambda s: (0, idx(s), 0)
    const = lambda s: (0, 0)
    in_specs = [pl.BlockSpec((bsz, CHUNK, 1536), lambda s: (0, idx(s), C_XBC // 1536)),
                pl.BlockSpec((bsz, CHUNK, 128), row),
                pl.BlockSpec((1, 128), const),
                pl.BlockSpec((1, 128), const),
                pl.BlockSpec((bsz, CHUNK, 512), lambda s: (0, idx(s), 9)),
                pl.BlockSpec((bsz, CHUNK, 512), lambda s: (0, idx(s), 10)),
                pl.BlockSpec((CHUNK, 128), lambda s: (idx(s), 0)),
                pl.BlockSpec((CHUNK, 128), lambda s: (idx(s), 0)),
                pl.BlockSpec((RET_H, 3, CHUNK, 128), lambda s: (0, 0, 0, 0))]
    args = [main, dtr, bias, alog, main, main, cos_t, sin_t, jnp.asarray(tables)]
    if backward:
        ssd_f, ret_f = fwd
        in_specs += [pl.BlockSpec((bsz, CHUNK, 1024), lambda s: (0, idx(s), 2)),
                     pl.BlockSpec((bsz, CHUNK, 1024), row),
                     pl.BlockSpec((1, 1024), const),
                     pl.BlockSpec((1, 1024), const),
                     pl.BlockSpec((bsz, CHUNK, 512), lambda s: (0, idx(s), 11)),
                     pl.BlockSpec((bsz, CHUNK, 512), row),
                     pl.BlockSpec((1, 128), const)]
        args += [main, ssd_f, dskip, ssd_gain, main, ret_f, ret_gain]
    out_dtype = BF16 if backward else F32
    return pl.pallas_call(
        functools.partial(_scan_kernel, backward=backward, etot=etot, nb=bsz),
        grid=(nchunks,),
        in_specs=in_specs,
        out_specs=[pl.BlockSpec((bsz, CHUNK, 1024), row),
                   pl.BlockSpec((bsz, CHUNK, 512), row)],
        out_shape=[jax.ShapeDtypeStruct((bsz, t, 1024), out_dtype),
                   jax.ShapeDtypeStruct((bsz, t, 512), out_dtype)],
        scratch_shapes=[pltpu.VMEM((bsz, SSD_G, SSD_N, 512), F32),
                        pltpu.VMEM((bsz, RET_H // 2, 128, 256), F32)],
        compiler_params=pltpu.CompilerParams(
            dimension_semantics=("arbitrary",), vmem_limit_bytes=VMEM_LIMIT),
        name="scan_bwd" if backward else "scan_fwd",
    )(*args)


def _outproj_kernel(x_ref, a_ref, s_ref, r_ref, ada_ref, w_ref, o_ref, *, tm, blk0, ctx_row):
    b = pl.program_id(0)
    i = pl.program_id(1) + blk0
    acc = _dot(a_ref[...], w_ref[0:512, :])
    acc += _dot(s_ref[...], w_ref[512:1536, :])
    acc += _dot(r_ref[...], w_ref[1536:2048, :])
    rows = i * tm + lax.broadcasted_iota(jnp.int32, (tm, 1), 0)
    gate = _row_mod(ada_ref, b, rows, 2 * D, 3 * D, ctx_row)
    o_ref[...] = x_ref[...] + gate * acc


def _outproj(xcat, attn_o, ssd_o, ret_o, ada, w_out, layer, ctx_row, latent_only):
    bsz, t, _ = xcat.shape
    tm = N_CTX if latent_only else 768
    blk0 = 1 if latent_only else 0
    nb = t // tm - blk0
    row = lambda b, i: (b, i + blk0, 0)
    return pl.pallas_call(
        functools.partial(_outproj_kernel, tm=tm, blk0=blk0, ctx_row=ctx_row),
        grid=(bsz, nb),
        in_specs=[pl.BlockSpec((None, tm, D), row),
                  pl.BlockSpec((None, tm, 512), row),
                  pl.BlockSpec((None, tm, 1024), row),
                  pl.BlockSpec((None, tm, 512), row),
                  pl.BlockSpec((8, 3 * D), lambda b, i: (0, 0)),
                  pl.BlockSpec((None, 2 * D, D), lambda b, i: (layer, 0, 0))],
        out_specs=pl.BlockSpec((None, tm, D), lambda b, i: (b, i, 0)),
        out_shape=jax.ShapeDtypeStruct((bsz, nb * tm, D), F32),
        compiler_params=pltpu.CompilerParams(
            dimension_semantics=("arbitrary", "arbitrary"), vmem_limit_bytes=VMEM_LIMIT),
        name="outproj",
    )(xcat, attn_o, ssd_o, ret_o, ada, w_out)


def _rope_tables(seq):
    n_rows = seq // GRID_W
    row = np.repeat(np.arange(n_rows, dtype=np.float32), GRID_W)
    col = np.tile(np.arange(GRID_W, dtype=np.float32), n_rows)
    n_freq = QK // 4
    inv_freq = (np.float32(ROPE_BASE) ** (-np.arange(n_freq, dtype=np.float32) / n_freq)).astype(np.float32)
    ang = np.concatenate([row[:, None] * inv_freq, col[:, None] * inv_freq], axis=-1).astype(np.float32)
    cos = np.concatenate([np.ones((N_CTX, QK // 2)), np.cos(ang.astype(np.float64))], axis=0)
    sin = np.concatenate([np.zeros((N_CTX, QK // 2)), np.sin(ang.astype(np.float64))], axis=0)
    cos_t = np.tile(cos, (1, 4)).astype(np.float32)
    sin_t = np.tile(np.concatenate([-sin, sin], axis=1), (1, 2)).astype(np.float32)
    return jnp.asarray(cos_t), jnp.asarray(sin_t)


_MAIN_COLS = ((0, 2048), (3616, 4640), (2048, 3584), (4640, 6176))
_DT_COLS = (3584, 3616)


def _wprep_kernel(w_ref, o_ref):
    w = w_ref[...]
    pad = jnp.zeros((w.shape[0], 128 - (_DT_COLS[1] - _DT_COLS[0])), F32)
    pieces = [w[:, a:b] for a, b in _MAIN_COLS + (_DT_COLS,)] + [pad]
    o_ref[...] = jnp.concatenate(pieces, axis=1).astype(BF16)


def _wprep(w_in):
    depth, d, n = w_in.shape
    tr = 128
    return pl.pallas_call(
        _wprep_kernel,
        grid=(depth, d // tr),
        in_specs=[pl.BlockSpec((None, tr, n), lambda l, r: (l, r, 0))],
        out_specs=pl.BlockSpec((None, tr, N_MAIN + 128), lambda l, r: (l, r, 0)),
        out_shape=jax.ShapeDtypeStruct((depth, d, N_MAIN + 128), BF16),
        name="wprep",
    )(w_in)


def _pad_lanes(v, n=128):
    v = v.reshape(1, -1)
    return jnp.pad(v, ((0, 0), (0, n - v.shape[1])))


def kernel(x, c, ctx, c_ctx, w_ada, b_ada, w_in, w_out, attn_q_norm, attn_k_norm,
           lambda_q1, lambda_k1, lambda_q2, lambda_k2, attn_subln,
           ssd_conv_w, ssd_conv_b, ssd_dt_bias, ssd_a_log, ssd_d, ssd_norm, ret_norm):
    bsz, seq, _ = x.shape
    depth = w_ada.shape[0]
    assert bsz + 1 <= 8 and ctx.shape[1] == N_CTX
    ctx_row = bsz
    xcat = jnp.concatenate([ctx, x], axis=1)
    cvec = jnp.zeros((8, D), F32).at[0:bsz].set(c).at[ctx_row].set(c_ctx)
    cos_t, sin_t = _rope_tables(seq)
    gsel = np.arange(512) // QK
    gmat = jnp.asarray((gsel[:, None] == gsel[None, :]).astype(np.float32), BF16)

    w_perm = _wprep(w_in)
    w_out_bf = w_out.astype(BF16)

    for layer in range(depth):
        last = layer == depth - 1
        lam_init = 0.8 - 0.6 * math.exp(-0.3 * layer)

        ada = _ada(cvec, w_ada[layer], b_ada[layer].reshape(1, -1))
        main, dtr, vt = _inproj(xcat, ada, w_perm, layer, cos_t, sin_t,
                                jnp.tile(attn_q_norm[layer], 8).reshape(1, 512),
                                jnp.tile(attn_k_norm[layer], 8).reshape(1, 512), gmat,
                                ssd_conv_w[layer], ssd_conv_b[layer].reshape(1, -1), ctx_row)

        lam_vec = jnp.stack([lambda_q1[layer], lambda_k1[layer], lambda_q2[layer], lambda_k2[layer]])
        attn_o = _attn(main, vt, lam_vec, attn_subln[layer].reshape(128, 1), lam_init,
                       tq=N_CTX, tk=ATT_TK)

        bias = _pad_lanes(ssd_dt_bias[layer])
        alog = _pad_lanes(ssd_a_log[layer])
        fwd = _scan(dtr, main, cos_t, sin_t, bias, alog, backward=False)
        ssd_o, ret_o = _scan(dtr, main, cos_t, sin_t, bias, alog, backward=True, fwd=fwd,
                             dskip=jnp.repeat(ssd_d[layer], SSD_P).reshape(1, 1024),
                             ssd_gain=ssd_norm[layer].reshape(1, 1024),
                             ret_gain=ret_norm[layer].reshape(1, 128))

        xcat = _outproj(xcat, attn_o, ssd_o, ret_o, ada, w_out_bf, layer, ctx_row,
                        latent_only=last)
    return xcat
```

```python
import functools
import math

import numpy as np
import jax
import jax.numpy as jnp
from jax import lax
from jax.experimental import pallas as pl
from jax.experimental.pallas import tpu as pltpu

F32 = jnp.float32
BF16 = jnp.bfloat16

D = 1024
N_CTX = 256
GRID_W = 64
EPS = 1e-6
ROPE_BASE = 10000.0
CHUNK = 128
QK = 64
ATT_H = 4
SSD_H = 16
SSD_G = 2
SSD_HPG = 8
SSD_P = 64
SSD_N = 128
RET_H = 4
RET_EXP_F = (5.0, 6.0, 7.0, 8.0)
RET_EXP_B = (5.5, 6.5, 7.5, 8.5)
N_MAIN = 6144
LOG2E = math.log2(math.e)
VMEM_LIMIT = 56 * 1024 * 1024
ATT_TK = 768
VT_ROWS = 144
_DONE = object()


def _silu(v):
    return v * (0.5 + 0.5 * jnp.tanh(0.5 * v))


def _split3(v):
    hi = v.astype(BF16)
    r1 = v - hi.astype(F32)
    mid = r1.astype(BF16)
    lo = (r1 - mid.astype(F32)).astype(BF16)
    return hi, mid, lo


def _dot(a, b):
    return jnp.dot(a, b, preferred_element_type=F32)


def _dot_nt(a, b):
    return lax.dot_general(a, b, (((1,), (1,)), ((), ())), preferred_element_type=F32)


def _tri_dot(tri, v):
    hi, mid, lo = _split3(v)
    return _dot(tri, hi) + _dot(tri, mid) + _dot(tri, lo)


def _ada_kernel(c_ref, w_ref, b_ref, o_ref):
    o_ref[...] = _dot(_silu(c_ref[...]), w_ref[...]) + b_ref[...]


def _ada(cvec, w, layer, b):
    n = w.shape[2]
    tn = 768
    return pl.pallas_call(
        _ada_kernel,
        grid=(n // tn,),
        in_specs=[pl.BlockSpec((8, D), lambda j: (0, 0)),
                  pl.BlockSpec((None, D, tn), lambda j: (layer, 0, j)),
                  pl.BlockSpec((1, tn), lambda j: (0, j))],
        out_specs=pl.BlockSpec((8, tn), lambda j: (0, j)),
        out_shape=jax.ShapeDtypeStruct((8, n), F32),
        name="ada",
    )(cvec, w, b)


def _row_mod(ada_ref, b, rows, lo, hi, ctx_row):
    vb = ada_ref[pl.ds(b, 1), lo:hi]
    vc = ada_ref[ctx_row:ctx_row + 1, lo:hi]
    return jnp.where(rows < N_CTX, vc, vb)


def _rope128(x, cos, sin, lo_half):
    partner = jnp.where(lo_half, pltpu.roll(x, 96, 1), pltpu.roll(x, 32, 1))
    return x * cos + partner * sin


C_Q, C_K, C_V, C_XBC = 0, 512, 1024, 3072
PLAIN_COLS = ((1536, 3072), (4608, 6144))


def _inproj_kernel(x_ref, xp_ref, xn_ref, ada_ref, w_ref, cos_ref, sin_ref,
                   gq_ref, gk_ref, gmat_ref, cw_ref, cb_ref,
                   main_ref, dt_ref, vt_ref, *, tm, t, ctx_row):
    b = pl.program_id(0)
    i = pl.program_id(1)

    def modulated(x, rows):
        xn = x * lax.rsqrt(jnp.mean(x * x, axis=-1, keepdims=True) + EPS)
        shift = _row_mod(ada_ref, b, rows, 0, D, ctx_row)
        scale = _row_mod(ada_ref, b, rows, D, 2 * D, ctx_row)
        return xn * (1.0 + scale) + shift

    loc = lax.broadcasted_iota(jnp.int32, (tm, 1), 0)
    row = i * tm + loc
    h = modulated(x_ref[...], row).astype(BF16)

    halo = lax.broadcasted_iota(jnp.int32, (8, 1), 0)
    h_halo = jnp.concatenate([modulated(xp_ref[...], i * tm - 8 + halo),
                              modulated(xn_ref[...], (i + 1) * tm + halo)], axis=0)
    h_ext = jnp.concatenate([h, h_halo.astype(BF16)], axis=0)
    seq_start = (row == 0) | (row == N_CTX)
    seq_end = (row == t - 1) | (row == N_CTX - 1)

    cos = cos_ref[...]
    sin = sin_ref[...]
    lane = lax.broadcasted_iota(jnp.int32, (tm, 128), 1)
    lo_half = (lane % QK) < (QK // 2)

    def proj(c0, width=512):
        return _dot(h, w_ref[:, c0:c0 + width])

    def conv_mm(c0):
        res = _dot(h_ext, w_ref[:, c0:c0 + 512])
        return res[0:tm], res[tm + 7:tm + 8], res[tm + 8:tm + 9]

    def conv_epilogue(c0, res):
        xbc, prev, nxt = res
        xm1 = jnp.where(loc == 0, prev, pltpu.roll(xbc, 1, 0))
        xm1 = jnp.where(seq_start, 0.0, xm1)
        xp1 = jnp.where(loc == tm - 1, nxt, pltpu.roll(xbc, tm - 1, 0))
        xp1 = jnp.where(seq_end, 0.0, xp1)
        cc = c0 - C_XBC
        cw = cw_ref[:, cc:cc + 512]
        y = cw[0:1] * xm1 + cw[1:2] * xbc + cw[2:3] * xp1 + cb_ref[:, cc:cc + 512]
        main_ref[:, c0:c0 + 512] = _silu(y).astype(BF16)

    def qk_epilogue(c0, gain_ref, out_scale, y):
        ss = _dot((y * y).astype(BF16), gmat_ref[...])
        y = y * lax.rsqrt(ss * (1.0 / QK) + EPS) * gain_ref[...]
        for g in range(4):
            yg = _rope128(y[:, g * 128:(g + 1) * 128], cos, sin, lo_half)
            main_ref[:, c0 + g * 128:c0 + (g + 1) * 128] = (yg * out_scale).astype(BF16)

    def v_epilogue(v):
        main_ref[:, C_V:C_V + 512] = v.astype(BF16)
        ones_rows = (lax.broadcasted_iota(jnp.int32, (VT_ROWS - 128, tm), 0) == 0).astype(BF16)
        for hd in range(ATT_H):
            vt_ref[hd, 0:128, :] = v[:, hd * 128:(hd + 1) * 128].T.astype(BF16)
            vt_ref[hd, 128:VT_ROWS, :] = ones_rows

    def plain_epilogue(c0, y):
        main_ref[:, c0:c0 + 512] = y.astype(BF16)

    def dt_epilogue(y):
        dt_ref[...] = y

    heavy = [(functools.partial(conv_mm, c0), functools.partial(conv_epilogue, c0))
             for c0 in range(C_XBC, C_XBC + 1536, 512)]
    heavy += [(functools.partial(proj, C_Q),
               functools.partial(qk_epilogue, C_Q, gq_ref, (QK ** -0.5) * LOG2E)),
              (functools.partial(proj, C_K), functools.partial(qk_epilogue, C_K, gk_ref, 1.0)),
              (functools.partial(proj, C_V), v_epilogue)]
    light = [(functools.partial(proj, c0), functools.partial(plain_epilogue, c0))
             for lo, hi in PLAIN_COLS for c0 in range(lo, hi, 512)]
    light.append((functools.partial(proj, N_MAIN, 128), dt_epilogue))
    stages = []
    while heavy or light:
        if heavy:
            stages.append(heavy.pop(0))
        if light:
            stages.append(light.pop(0))
    pending = None
    for mm, epilogue in stages:
        res = mm()
        if pending is not None:
            pending()
        pending = functools.partial(epilogue, res)
    pending()


def _inproj(xcat, ada, w_all, layer, cos_t, sin_t, gq, gk, gmat, conv_w, conv_b, ctx_row):
    bsz, t, _ = xcat.shape
    tm = 384
    assert ATT_TK % tm == 0
    per_tile = ATT_TK // tm
    r8 = tm // 8
    last8 = t // 8 - 1
    const = lambda b, i: (0, 0)
    return pl.pallas_call(
        functools.partial(_inproj_kernel, tm=tm, t=t, ctx_row=ctx_row),
        grid=(bsz, t // tm),
        in_specs=[pl.BlockSpec((None, tm, D), lambda b, i: (b, i, 0)),
                  pl.BlockSpec((None, 8, D), lambda b, i: (b, jnp.maximum(i * r8 - 1, 0), 0)),
                  pl.BlockSpec((None, 8, D), lambda b, i: (b, jnp.minimum((i + 1) * r8, last8), 0)),
                  pl.BlockSpec((8, 3 * D), const),
                  pl.BlockSpec((None, D, N_MAIN + 128), lambda b, i: (layer, 0, 0)),
                  pl.BlockSpec((tm, 128), lambda b, i: (i, 0)),
                  pl.BlockSpec((tm, 128), lambda b, i: (i, 0)),
                  pl.BlockSpec((1, 512), const),
                  pl.BlockSpec((1, 512), const),
                  pl.BlockSpec((512, 512), const),
                  pl.BlockSpec((3, 1536), const),
                  pl.BlockSpec((1, 1536), const)],
        out_specs=[pl.BlockSpec((None, tm, N_MAIN), lambda b, i: (b, i, 0)),
                   pl.BlockSpec((None, tm, 128), lambda b, i: (b, i, 0)),
                   pl.BlockSpec((None, ATT_H, None, VT_ROWS, tm),
                                lambda b, i: (b, 0, i // per_tile, 0, i % per_tile))],
        out_shape=[jax.ShapeDtypeStruct((bsz, t, N_MAIN), BF16),
                   jax.ShapeDtypeStruct((bsz, t, 128), F32),
                   jax.ShapeDtypeStruct((bsz, ATT_H, t // ATT_TK, VT_ROWS, ATT_TK), BF16)],
        compiler_params=pltpu.CompilerParams(
            dimension_semantics=("arbitrary", "arbitrary"), vmem_limit_bytes=VMEM_LIMIT),
        name="inproj",
    )(xcat, xcat, xcat, ada, w_all, cos_t, sin_t, gq, gk, gmat, conv_w, conv_b)


def _attn_kernel(q_ref, qn_ref, k_ref, vt_ref, gate_ref, lam_ref, subln_ref, o_ref,
                 q2_ref, m_ref, acc_ref, s_ref, mt_ref, *, tq, tk, nkv, lam_init):
    qi = pl.program_id(2)
    lane = lax.broadcasted_iota(jnp.int32, (tq, 128), 1)
    m_ref[...] = jnp.full((1, 2 * tq), -jnp.inf, F32)
    acc_ref[...] = jnp.zeros((VT_ROWS, 2 * tq), F32)

    def stack_maps(src_ref):
        q = src_ref[...]
        zero = jnp.zeros_like(q)
        q2_ref[0:tq, :] = jnp.where(lane < QK, q, zero)
        q2_ref[tq:2 * tq, :] = jnp.where(lane >= QK, q, zero)

    def hand_off():
        stack_maps(qn_ref)
        scores(0, 2, False)

    def scores(j, slot, ctx_only):
        st = _dot_nt(k_ref[pl.ds(j * tk, tk), :], q2_ref[...])
        if ctx_only:
            krow = j * tk + lax.broadcasted_iota(jnp.int32, (tk, 1), 0)
            st = jnp.where(krow < N_CTX, st, -jnp.inf)
        s_ref[slot] = st
        mt_ref[slot] = jnp.max(st, axis=0, keepdims=True)

    def consume(j, slot):
        m_old = m_ref[...]
        m_new = jnp.maximum(m_old, mt_ref[slot])
        alpha = jnp.exp2(m_old - m_new)
        p = jnp.exp2(s_ref[slot] - m_new)
        acc_ref[...] = alpha * acc_ref[...] + _dot(vt_ref[j], p.astype(BF16))
        m_ref[...] = m_new

    def finalize():
        lam_v = lam_ref[...]
        lam = (jnp.exp(jnp.sum(lam_v[0:1] * lam_v[1:2], axis=-1, keepdims=True))
               - jnp.exp(jnp.sum(lam_v[2:3] * lam_v[3:4], axis=-1, keepdims=True)) + lam_init)
        inv = 1.0 / acc_ref[128:129, :]
        acc = acc_ref[0:128, :]
        o = acc[:, 0:tq] * inv[:, 0:tq] - lam * (acc[:, tq:2 * tq] * inv[:, tq:2 * tq])
        o = o * lax.rsqrt(jnp.mean(o * o, axis=0, keepdims=True) + EPS)
        o = o * subln_ref[...] * (1.0 - lam_init)
        o_ref[...] = (o.T * _silu(gate_ref[...].astype(F32))).astype(BF16)

    @pl.when(qi == 0)
    def _():
        stack_maps(q_ref)
        scores(0, 0, True)
        hand_off()
        consume(0, 0)
        finalize()

    @pl.when(qi != 0)
    def _():
        for j in range(nkv):
            if j + 1 < nkv:
                scores(j + 1, (j + 1) % 2, False)
            else:
                hand_off()
            consume(j, 2 if j == 0 else j % 2)
        finalize()


def _attn(main, vt, lam_vec, subln, lam_init, tq, tk):
    bsz, t, _ = main.shape
    nkv = t // tk
    assert tq == N_CTX and tk >= N_CTX and nkv >= 2
    nq = t // tq
    return pl.pallas_call(
        functools.partial(_attn_kernel, tq=tq, tk=tk, nkv=nkv, lam_init=lam_init),
        grid=(bsz, ATT_H, nq),
        in_specs=[pl.BlockSpec((None, tq, 128), lambda b, h, i: (b, i, h)),
                  pl.BlockSpec((None, tq, 128), lambda b, h, i: (b, jnp.minimum(i + 1, nq - 1), h)),
                  pl.BlockSpec((None, t, 128), lambda b, h, i: (b, 0, C_K // 128 + h)),
                  pl.BlockSpec((None, None, nkv, VT_ROWS, tk), lambda b, h, i: (b, h, 0, 0, 0)),
                  pl.BlockSpec((None, tq, 128), lambda b, h, i: (b, i, 12 + h)),
                  pl.BlockSpec((4, QK), lambda b, h, i: (0, 0)),
                  pl.BlockSpec((128, 1), lambda b, h, i: (0, 0))],
        out_specs=pl.BlockSpec((None, tq, 128), lambda b, h, i: (b, i, h)),
        out_shape=jax.ShapeDtypeStruct((bsz, t, 512), BF16),
        scratch_shapes=[pltpu.VMEM((2 * tq, 128), BF16),
                        pltpu.VMEM((1, 2 * tq), F32),
                        pltpu.VMEM((VT_ROWS, 2 * tq), F32),
                        pltpu.VMEM((3, tk, 2 * tq), F32),
                        pltpu.VMEM((3, 1, 2 * tq), F32)],
        compiler_params=pltpu.CompilerParams(
            dimension_semantics=("arbitrary", "arbitrary", "arbitrary"),
            vmem_limit_bytes=VMEM_LIMIT),
        name="attn",
    )(main, main, main, vt, main, lam_vec, subln)


def _chunk_index(s, nchunks, backward):
    if not backward:
        return s
    nctx = N_CTX // CHUNK
    return jnp.where(s < nctx, nctx - 1 - s, nchunks - 1 + nctx - s)


def _tri_mask(backward):
    r = lax.broadcasted_iota(jnp.int32, (CHUNK, CHUNK), 0)
    c = lax.broadcasted_iota(jnp.int32, (CHUNK, CHUNK), 1)
    return (c >= r) if backward else (c <= r)


def _ssd_chunk(u_ref, dtr_ref, bias_ref, alog_ref, o_ref, s_ref, fin, backward):
    mask = _tri_mask(backward)
    tri = jnp.where(mask, 1.0, 0.0).astype(BF16)
    pre = dtr_ref[...] + bias_ref[...]
    dt = jnp.maximum(pre, 0.0) + jnp.log1p(jnp.exp(-jnp.abs(pre)))
    la = dt * (-LOG2E * jnp.exp(alog_ref[...]))
    cum = _tri_dot(tri, la)
    cum_t = cum.T
    dt_t = dt.T
    edge = 0 if backward else CHUNK - 1
    etot = jnp.exp2(cum[edge:edge + 1, :])
    wdt_t = jnp.exp2(cum_t[:, edge:edge + 1] - cum_t) * dt_t
    first = lax.broadcasted_iota(jnp.int32, (CHUNK, 128), 1) < SSD_P
    first_row = lax.broadcasted_iota(jnp.int32, (1, 128), 1) < SSD_P
    col0 = SSD_H if backward else 0
    yield

    for g in range(SSD_G):
        k = u_ref[:, 1024 + g * SSD_N:1024 + (g + 1) * SSD_N]
        q = u_ref[:, 1280 + g * SSD_N:1280 + (g + 1) * SSD_N]
        scores = _dot_nt(q, k)
        k_t = k.astype(F32).T
        y_inter = _dot(q, s_ref[g].astype(BF16))
        ys = []
        yield
        for pp in range(SSD_HPG // 2):
            h0 = g * SSD_HPG + 2 * pp
            c0 = col0 + h0
            off = h0 * SSD_P
            xs = u_ref[:, off:off + 128]
            zero = jnp.zeros_like(xs)
            vals = jnp.concatenate([jnp.where(first, xs, zero),
                                    jnp.where(first, zero, xs)], axis=0)
            wts, kws, ecs = [], [], []
            for hh in range(2):
                c = c0 + hh
                cum_c = jnp.broadcast_to(cum[:, c:c + 1], (CHUNK, CHUNK))
                dec = jnp.exp2(jnp.where(mask, cum_c - cum_t[c:c + 1, :], -jnp.inf))
                wts.append((scores * dec * dt_t[c:c + 1, :]).astype(BF16))
                kws.append((k_t * wdt_t[c:c + 1, :]).astype(BF16))
                ecs.append(jnp.exp2(cum_c))
            y = (jnp.where(first, ecs[0], ecs[1]) * y_inter[:, pp * 128:(pp + 1) * 128]
                 + _dot(jnp.concatenate(wts, axis=1), vals))
            etot_p = jnp.where(first_row, jnp.broadcast_to(etot[:, c0:c0 + 1], (1, 128)),
                               jnp.broadcast_to(etot[:, c0 + 1:c0 + 2], (1, 128)))
            s_old = s_ref[g, :, pp * 128:(pp + 1) * 128]
            s_ref[g, :, pp * 128:(pp + 1) * 128] = (
                etot_p * s_old + _dot(jnp.concatenate(kws, axis=1), vals))
            if backward:
                z_ref, yf_ref, dskip_ref = fin[0:3]
                y = y + yf_ref[:, off:off + 128] + xs.astype(F32) * dskip_ref[:, off:off + 128]
                y = y * _silu(z_ref[:, off:off + 128].astype(F32))
                ys.append(y)
            else:
                o_ref[:, off:off + 128] = y
            yield
        if backward:
            gain_ref = fin[3]
            lo, hi = g * 512, (g + 1) * 512
            yg = jnp.concatenate(ys, axis=1)
            yg = yg * lax.rsqrt(jnp.mean(yg * yg, axis=-1, keepdims=True) + EPS)
            o_ref[:, lo:hi] = (yg * gain_ref[:, lo:hi]).astype(o_ref.dtype)


def _ret_tables(backward):
    exps = RET_EXP_B if backward else RET_EXP_F
    lg = np.log1p(-np.exp2(-np.asarray(exps, np.float64)))
    i = np.arange(CHUNK, dtype=np.float64)
    if backward:
        cum = (CHUNK - i)[None, :] * lg[:, None]
        tot = cum[:, 0]
        msk = i[None, :] >= i[:, None]
    else:
        cum = (i + 1.0)[None, :] * lg[:, None]
        tot = cum[:, -1]
        msk = i[None, :] <= i[:, None]
    dec = np.where(msk[None], np.exp(cum[:, :, None] - cum[:, None, :]), 0.0)
    inter = np.broadcast_to(np.exp(cum)[:, :, None], (RET_H, CHUNK, 128))
    toend = np.broadcast_to(np.exp(tot[:, None] - cum)[:, :, None], (RET_H, CHUNK, 128))
    tables = np.stack([dec, inter, toend], axis=1).astype(np.float32)
    return tables, [float(np.exp(v)) for v in tot]


def _ret_chunk(qk_ref, v_ref, cos, sin, tab_ref, o_ref, s_ref, fin, backward, etot):
    lane = lax.broadcasted_iota(jnp.int32, (CHUNK, 128), 1)
    lo_half = (lane % QK) < (QK // 2)
    first = lane < QK
    srow = lax.broadcasted_iota(jnp.int32, (128, 256), 0) < QK
    scol = lax.broadcasted_iota(jnp.int32, (128, 256), 1) < 128
    diag = srow == scol
    for pp in range(RET_H // 2):
        h0 = 2 * pp
        q = _rope128(qk_ref[:, pp * 128:(pp + 1) * 128].astype(F32), cos, sin, lo_half)
        k = _rope128(qk_ref[:, 256 + pp * 128:256 + (pp + 1) * 128].astype(F32) * (QK ** -0.5),
                     cos, sin, lo_half)
        qb = q.astype(BF16)
        kb = k.astype(BF16)
        k_t = kb.astype(F32).T.astype(BF16)
        zero = jnp.zeros_like(kb)
        k2 = jnp.concatenate([jnp.where(first, kb, zero),
                              jnp.where(first, zero, kb)], axis=0)
        scores = _dot_nt(qb, k2)
        v2 = v_ref[:, h0 * 128:(h0 + 2) * 128]
        zv = jnp.zeros((CHUNK, 128), BF16)
        vdiag = jnp.concatenate(
            [jnp.concatenate([v2[:, 0:128], zv], axis=1),
             jnp.concatenate([zv, v2[:, 128:256]], axis=1)], axis=0)
        dec = jnp.concatenate([tab_ref[h0, 0], tab_ref[h0 + 1, 0]], axis=1)
        inter = jnp.concatenate([tab_ref[h0, 1], tab_ref[h0 + 1, 1]], axis=1)
        toend = jnp.concatenate([tab_ref[h0, 2], tab_ref[h0 + 1, 2]], axis=1)
        s_pair = s_ref[pp]
        y = _dot((scores * dec).astype(BF16), vdiag) + inter * _dot(qb, s_pair.astype(BF16))
        upd = _dot(k_t, (v2.astype(F32) * toend).astype(BF16))
        s_ref[pp] = (jnp.where(scol, etot[h0], etot[h0 + 1]) * s_pair
                     + jnp.where(diag, upd, 0.0))
        for hh in range(2):
            h = h0 + hh
            yh = y[:, hh * 128:(hh + 1) * 128]
            if backward:
                g_ref, yf_ref, gain_ref = fin
                yh = yh + yf_ref[:, h * 128:(h + 1) * 128]
                yh = yh * lax.rsqrt(jnp.mean(yh * yh, axis=-1, keepdims=True) + EPS) * gain_ref[...]
                yh = yh * _silu(g_ref[:, h * 128:(h + 1) * 128].astype(F32))
            o_ref[:, h * 128:(h + 1) * 128] = yh.astype(o_ref.dtype)
        yield


def _scan_kernel(*refs, backward, etot, nb):
    if backward:
        (u_ref, dtr_ref, bias_ref, alog_ref, qk_ref, v_ref, cos_ref, sin_ref, tab_ref,
         z_ref, ssd_f_ref, dskip_ref, ssd_gain_ref, g_ref, ret_f_ref, ret_gain_ref,
         ssd_o_ref, ret_o_ref, ssd_s_ref, ret_s_ref) = refs
    else:
        (u_ref, dtr_ref, bias_ref, alog_ref, qk_ref, v_ref, cos_ref, sin_ref, tab_ref,
         ssd_o_ref, ret_o_ref, ssd_s_ref, ret_s_ref) = refs

    @pl.when(pl.program_id(0) == 0)
    def _():
        ssd_s_ref[...] = jnp.zeros_like(ssd_s_ref)
        ret_s_ref[...] = jnp.zeros_like(ret_s_ref)

    cos = cos_ref[...]
    sin = sin_ref[...]
    chains = []
    for b in range(nb):
        ssd_fin = (z_ref.at[b], ssd_f_ref.at[b], dskip_ref, ssd_gain_ref) if backward else None
        ret_fin = (g_ref.at[b], ret_f_ref.at[b], ret_gain_ref) if backward else None
        chains.append(_ssd_chunk(u_ref.at[b], dtr_ref.at[b], bias_ref, alog_ref,
                                 ssd_o_ref.at[b], ssd_s_ref.at[b], ssd_fin, backward))
        chains.append(_ret_chunk(qk_ref.at[b], v_ref.at[b], cos, sin, tab_ref,
                                 ret_o_ref.at[b], ret_s_ref.at[b], ret_fin, backward, etot))
    while chains:
        chains = [c for c in chains if next(c, _DONE) is not _DONE]


def _scan(dtr, main, cos_t, sin_t, bias, alog, backward, fwd=None, dskip=None,
          ssd_gain=None, ret_gain=None):
    bsz, t, _ = main.shape
    nchunks = t // CHUNK
    tables, etot = _ret_tables(backward)
    idx = functools.partial(_chunk_index, nchunks=nchunks, backward=backward)
    row = lambda s: (0, idx(s), 0)
    const = lambda s: (0, 0)
    in_specs = [pl.BlockSpec((bsz, CHUNK, 1536), lambda s: (0, idx(s), C_XBC // 1536)),
                pl.BlockSpec((bsz, CHUNK, 128), row),
                pl.BlockSpec((1, 128), const),
                pl.BlockSpec((1, 128), const),
                pl.BlockSpec((bsz, CHUNK, 512), lambda s: (0, idx(s), 9)),
                pl.BlockSpec((bsz, CHUNK, 512), lambda s: (0, idx(s), 10)),
                pl.BlockSpec((CHUNK, 128), lambda s: (idx(s), 0)),
                pl.BlockSpec((CHUNK, 128), lambda s: (idx(s), 0)),
                pl.BlockSpec((RET_H, 3, CHUNK, 128), lambda s: (0, 0, 0, 0))]
    args = [main, dtr, bias, alog, main, main, cos_t, sin_t, jnp.asarray(tables)]
    if backward:
        ssd_f, ret_f = fwd
        in_specs += [pl.BlockSpec((bsz, CHUNK, 1024), lambda s: (0, idx(s), 2)),
                     pl.BlockSpec((bsz, CHUNK, 1024), row),
                     pl.BlockSpec((1, 1024), const),
                     pl.BlockSpec((1, 1024), const),
                     pl.BlockSpec((bsz, CHUNK, 512), lambda s: (0, idx(s), 11)),
                     pl.BlockSpec((bsz, CHUNK, 512), row),
                     pl.BlockSpec((1, 128), const)]
        args += [main, ssd_f, dskip, ssd_gain, main, ret_f, ret_gain]
    out_dtype = BF16 if backward else F32
    return pl.pallas_call(
        functools.partial(_scan_kernel, backward=backward, etot=etot, nb=bsz),
        grid=(nchunks,),
        in_specs=in_specs,
        out_specs=[pl.BlockSpec((bsz, CHUNK, 1024), row),
                   pl.BlockSpec((bsz, CHUNK, 512), row)],
        out_shape=[jax.ShapeDtypeStruct((bsz, t, 1024), out_dtype),
                   jax.ShapeDtypeStruct((bsz, t, 512), out_dtype)],
        scratch_shapes=[pltpu.VMEM((bsz, SSD_G, SSD_N, 512), F32),
                        pltpu.VMEM((bsz, RET_H // 2, 128, 256), F32)],
        compiler_params=pltpu.CompilerParams(
            dimension_semantics=("arbitrary",), vmem_limit_bytes=VMEM_LIMIT),
        name="scan_bwd" if backward else "scan_fwd",
    )(*args)


def _outproj_kernel(x_ref, a_ref, s_ref, r_ref, ada_ref, w_ref, o_ref, *, tm, blk0, ctx_row):
    b = pl.program_id(0)
    i = pl.program_id(1) + blk0
    acc = _dot(a_ref[...], w_ref[0:512, :])
    acc += _dot(s_ref[...], w_ref[512:1536, :])
    acc += _dot(r_ref[...], w_ref[1536:2048, :])
    rows = i * tm + lax.broadcasted_iota(jnp.int32, (tm, 1), 0)
    gate = _row_mod(ada_ref, b, rows, 2 * D, 3 * D, ctx_row)
    o_ref[...] = x_ref[...] + gate * acc


def _outproj(xcat, attn_o, ssd_o, ret_o, ada, w_out, layer, ctx_row, latent_only):
    bsz, t, _ = xcat.shape
    tm = N_CTX if latent_only else 768
    blk0 = 1 if latent_only else 0
    nb = t // tm - blk0
    row = lambda b, i: (b, i + blk0, 0)
    return pl.pallas_call(
        functools.partial(_outproj_kernel, tm=tm, blk0=blk0, ctx_row=ctx_row),
        grid=(bsz, nb),
        in_specs=[pl.BlockSpec((None, tm, D), row),
                  pl.BlockSpec((None, tm, 512), row),
                  pl.BlockSpec((None, tm, 1024), row),
                  pl.BlockSpec((None, tm, 512), row),
                  pl.BlockSpec((8, 3 * D), lambda b, i: (0, 0)),
                  pl.BlockSpec((None, 2 * D, D), lambda b, i: (layer, 0, 0))],
        out_specs=pl.BlockSpec((None, tm, D), lambda b, i: (b, i, 0)),
        out_shape=jax.ShapeDtypeStruct((bsz, nb * tm, D), F32),
        compiler_params=pltpu.CompilerParams(
            dimension_semantics=("arbitrary", "arbitrary"), vmem_limit_bytes=VMEM_LIMIT),
        name="outproj",
    )(xcat, attn_o, ssd_o, ret_o, ada, w_out)


def _rope_tables(seq):
    n_rows = seq // GRID_W
    row = np.repeat(np.arange(n_rows, dtype=np.float32), GRID_W)
    col = np.tile(np.arange(GRID_W, dtype=np.float32), n_rows)
    n_freq = QK // 4
    inv_freq = (np.float32(ROPE_BASE) ** (-np.arange(n_freq, dtype=np.float32) / n_freq)).astype(np.float32)
    ang = np.concatenate([row[:, None] * inv_freq, col[:, None] * inv_freq], axis=-1).astype(np.float32)
    cos = np.concatenate([np.ones((N_CTX, QK // 2)), np.cos(ang.astype(np.float64))], axis=0)
    sin = np.concatenate([np.zeros((N_CTX, QK // 2)), np.sin(ang.astype(np.float64))], axis=0)
    cos_t = np.tile(cos, (1, 4)).astype(np.float32)
    sin_t = np.tile(np.concatenate([-sin, sin], axis=1), (1, 2)).astype(np.float32)
    return jnp.asarray(cos_t), jnp.asarray(sin_t)


_MAIN_COLS = ((0, 2048), (3616, 4640), (2048, 3584), (4640, 6176))
_DT_COLS = (3584, 3616)


def _wprep_kernel(w_ref, o_ref):
    w = w_ref[...].astype(F32)
    pad = jnp.zeros((w.shape[0], 128 - (_DT_COLS[1] - _DT_COLS[0])), F32)
    pieces = [w[:, a:b] for a, b in _MAIN_COLS + (_DT_COLS,)] + [pad]
    o_ref[...] = jnp.concatenate(pieces, axis=1).astype(BF16)


def _wprep(w_in):
    depth, d, n = w_in.shape
    tr = 128
    return pl.pallas_call(
        _wprep_kernel,
        grid=(depth, d // tr),
        in_specs=[pl.BlockSpec((None, tr, n), lambda l, r: (l, r, 0))],
        out_specs=pl.BlockSpec((None, tr, N_MAIN + 128), lambda l, r: (l, r, 0)),
        out_shape=jax.ShapeDtypeStruct((depth, d, N_MAIN + 128), BF16),
        name="wprep",
    )(w_in)


def _pad_lanes(v, n=128):
    v = v.reshape(1, -1)
    return jnp.pad(v, ((0, 0), (0, n - v.shape[1])))


def kernel(x, c, ctx, c_ctx, w_ada, b_ada, w_in, w_out, attn_q_norm, attn_k_norm,
           lambda_q1, lambda_k1, lambda_q2, lambda_k2, attn_subln,
           ssd_conv_w, ssd_conv_b, ssd_dt_bias, ssd_a_log, ssd_d, ssd_norm, ret_norm):
    bsz, seq, _ = x.shape
    depth = w_ada.shape[0]
    assert bsz + 1 <= 8 and ctx.shape[1] == N_CTX
    ctx_row = bsz
    xcat = jnp.concatenate([ctx, x], axis=1)
    cvec = jnp.zeros((8, D), F32).at[0:bsz].set(c).at[ctx_row].set(c_ctx)
    cos_t, sin_t = _rope_tables(seq)
    gsel = np.arange(512) // QK
    gmat = jnp.asarray((gsel[:, None] == gsel[None, :]).astype(np.float32), BF16)

    w_perm = _wprep(w_in.astype(BF16))
    w_out_bf = w_out.astype(BF16)

    for layer in range(depth):
        last = layer == depth - 1
        lam_init = 0.8 - 0.6 * math.exp(-0.3 * layer)

        ada = _ada(cvec, w_ada, layer, b_ada[layer].reshape(1, -1))
        main, dtr, vt = _inproj(xcat, ada, w_perm, layer, cos_t, sin_t,
                                jnp.tile(attn_q_norm[layer], 8).reshape(1, 512),
                                jnp.tile(attn_k_norm[layer], 8).reshape(1, 512), gmat,
                                ssd_conv_w[layer], ssd_conv_b[layer].reshape(1, -1), ctx_row)

        lam_vec = jnp.stack([lambda_q1[layer], lambda_k1[layer], lambda_q2[layer], lambda_k2[layer]])
        attn_o = _attn(main, vt, lam_vec, attn_subln[layer].reshape(128, 1), lam_init,
                       tq=N_CTX, tk=ATT_TK)

        bias = _pad_lanes(ssd_dt_bias[layer])
        alog = _pad_lanes(ssd_a_log[layer])
        fwd = _scan(dtr, main, cos_t, sin_t, bias, alog, backward=False)
        ssd_o, ret_o = _scan(dtr, main, cos_t, sin_t, bias, alog, backward=True, fwd=fwd,
                             dskip=jnp.repeat(ssd_d[layer], SSD_P).reshape(1, 1024),
                             ssd_gain=ssd_norm[layer].reshape(1, 1024),
                             ret_gain=ret_norm[layer].reshape(1, 128))

        xcat = _outproj(xcat, attn_o, ssd_o, ret_o, ada, w_out_bf, layer, ctx_row,
                        latent_only=last)
    return xcat
```

```python
import functools
import math

import numpy as np
import jax
import jax.numpy as jnp
from jax import lax
from jax.experimental import pallas as pl
from jax.experimental.pallas import tpu as pltpu

F32 = jnp.float32
BF16 = jnp.bfloat16

D = 1024
N_CTX = 256
GRID_W = 64
EPS = 1e-6
ROPE_BASE = 10000.0
CHUNK = 128
QK = 64
ATT_H = 4
SSD_H = 16
SSD_G = 2
SSD_HPG = 8
SSD_P = 64
SSD_N = 128
RET_H = 4
RET_EXP_F = (5.0, 6.0, 7.0, 8.0)
RET_EXP_B = (5.5, 6.5, 7.5, 8.5)
N_MAIN = 6144
LOG2E = math.log2(math.e)
VMEM_LIMIT = 56 * 1024 * 1024
ATT_TK = 768
VT_ROWS = 144
_DONE = object()


def _silu(v):
    return v * (0.5 + 0.5 * jnp.tanh(0.5 * v))


def _split3(v):
    hi = v.astype(BF16)
    r1 = v - hi.astype(F32)
    mid = r1.astype(BF16)
    lo = (r1 - mid.astype(F32)).astype(BF16)
    return hi, mid, lo


def _dot(a, b):
    return jnp.dot(a, b, preferred_element_type=F32)


def _dot_nt(a, b):
    return lax.dot_general(a, b, (((1,), (1,)), ((), ())), preferred_element_type=F32)


def _tri_dot(tri, v):
    hi, mid, lo = _split3(v)
    return _dot(tri, hi) + _dot(tri, mid) + _dot(tri, lo)


def _ada_kernel(c_ref, w_ref, b_ref, o_ref):
    o_ref[...] = _dot(_silu(c_ref[...]), w_ref[...]) + b_ref[...]


def _ada(cvec, w, layer, b):
    n = w.shape[2]
    tn = 768
    return pl.pallas_call(
        _ada_kernel,
        grid=(n // tn,),
        in_specs=[pl.BlockSpec((8, D), lambda j: (0, 0)),
                  pl.BlockSpec((None, D, tn), lambda j: (layer, 0, j)),
                  pl.BlockSpec((1, tn), lambda j: (0, j))],
        out_specs=pl.BlockSpec((8, tn), lambda j: (0, j)),
        out_shape=jax.ShapeDtypeStruct((8, n), F32),
        name="ada",
    )(cvec, w, b)


def _row_mod(ada_ref, b, rows, lo, hi, ctx_row):
    vb = ada_ref[pl.ds(b, 1), lo:hi]
    vc = ada_ref[ctx_row:ctx_row + 1, lo:hi]
    return jnp.where(rows < N_CTX, vc, vb)


def _rope128(x, cos, sin, lo_half):
    partner = jnp.where(lo_half, pltpu.roll(x, 96, 1), pltpu.roll(x, 32, 1))
    return x * cos + partner * sin


C_Q, C_K, C_V, C_XBC = 0, 512, 1024, 3072
PLAIN_COLS = ((1536, 3072), (4608, 6144))


def _inproj_kernel(x_ref, xp_ref, xn_ref, ada_ref, w_ref, cos_ref, sin_ref,
                   gq_ref, gk_ref, gmat_ref, cw_ref, cb_ref,
                   main_ref, dt_ref, vt_ref, *, tm, t, ctx_row):
    b = pl.program_id(0)
    i = pl.program_id(1)

    def modulated(x, rows):
        xn = x * lax.rsqrt(jnp.mean(x * x, axis=-1, keepdims=True) + EPS)
        shift = _row_mod(ada_ref, b, rows, 0, D, ctx_row)
        scale = _row_mod(ada_ref, b, rows, D, 2 * D, ctx_row)
        return xn * (1.0 + scale) + shift

    loc = lax.broadcasted_iota(jnp.int32, (tm, 1), 0)
    row = i * tm + loc
    h = modulated(x_ref[...], row).astype(BF16)

    halo = lax.broadcasted_iota(jnp.int32, (8, 1), 0)
    h_halo = jnp.concatenate([modulated(xp_ref[...], i * tm - 8 + halo),
                              modulated(xn_ref[...], (i + 1) * tm + halo)], axis=0)
    h_ext = jnp.concatenate([h, h_halo.astype(BF16)], axis=0)
    seq_start = (row == 0) | (row == N_CTX)
    seq_end = (row == t - 1) | (row == N_CTX - 1)

    cos = cos_ref[...]
    sin = sin_ref[...]
    lane = lax.broadcasted_iota(jnp.int32, (tm, 128), 1)
    lo_half = (lane % QK) < (QK // 2)

    def proj(c0, width=512):
        return _dot(h, w_ref[:, c0:c0 + width])

    def conv_mm(c0):
        res = _dot(h_ext, w_ref[:, c0:c0 + 512])
        return res[0:tm], res[tm + 7:tm + 8], res[tm + 8:tm + 9]

    def conv_epilogue(c0, res):
        xbc, prev, nxt = res
        xm1 = jnp.where(loc == 0, prev, pltpu.roll(xbc, 1, 0))
        xm1 = jnp.where(seq_start, 0.0, xm1)
        xp1 = jnp.where(loc == tm - 1, nxt, pltpu.roll(xbc, tm - 1, 0))
        xp1 = jnp.where(seq_end, 0.0, xp1)
        cc = c0 - C_XBC
        cw = cw_ref[:, cc:cc + 512]
        y = cw[0:1] * xm1 + cw[1:2] * xbc + cw[2:3] * xp1 + cb_ref[:, cc:cc + 512]
        main_ref[:, c0:c0 + 512] = _silu(y).astype(BF16)

    def qk_epilogue(c0, gain_ref, out_scale, y):
        ss = _dot((y * y).astype(BF16), gmat_ref[...])
        y = y * lax.rsqrt(ss * (1.0 / QK) + EPS) * gain_ref[...]
        for g in range(4):
            yg = _rope128(y[:, g * 128:(g + 1) * 128], cos, sin, lo_half)
            main_ref[:, c0 + g * 128:c0 + (g + 1) * 128] = (yg * out_scale).astype(BF16)

    def v_epilogue(v):
        main_ref[:, C_V:C_V + 512] = v.astype(BF16)
        ones_rows = (lax.broadcasted_iota(jnp.int32, (VT_ROWS - 128, tm), 0) == 0).astype(BF16)
        for hd in range(ATT_H):
            vt_ref[hd, 0:128, :] = v[:, hd * 128:(hd + 1) * 128].T.astype(BF16)
            vt_ref[hd, 128:VT_ROWS, :] = ones_rows

    def plain_epilogue(c0, y):
        main_ref[:, c0:c0 + 512] = y.astype(BF16)

    def dt_epilogue(y):
        dt_ref[...] = y

    heavy = [(functools.partial(conv_mm, c0), functools.partial(conv_epilogue, c0))
             for c0 in range(C_XBC, C_XBC + 1536, 512)]
    heavy += [(functools.partial(proj, C_Q),
               functools.partial(qk_epilogue, C_Q, gq_ref, (QK ** -0.5) * LOG2E)),
              (functools.partial(proj, C_K), functools.partial(qk_epilogue, C_K, gk_ref, 1.0)),
              (functools.partial(proj, C_V), v_epilogue)]
    light = [(functools.partial(proj, c0), functools.partial(plain_epilogue, c0))
             for lo, hi in PLAIN_COLS for c0 in range(lo, hi, 512)]
    light.append((functools.partial(proj, N_MAIN, 128), dt_epilogue))
    stages = []
    while heavy or light:
        if heavy:
            stages.append(heavy.pop(0))
        if light:
            stages.append(light.pop(0))
    pending = None
    for mm, epilogue in stages:
        res = mm()
        if pending is not None:
            pending()
        pending = functools.partial(epilogue, res)
    pending()


def _inproj(xcat, ada, w_all, layer, cos_t, sin_t, gq, gk, gmat, conv_w, conv_b, ctx_row):
    bsz, t, _ = xcat.shape
    tm = 768
    assert ATT_TK % tm == 0
    per_tile = ATT_TK // tm
    r8 = tm // 8
    last8 = t // 8 - 1
    const = lambda b, i: (0, 0)
    return pl.pallas_call(
        functools.partial(_inproj_kernel, tm=tm, t=t, ctx_row=ctx_row),
        grid=(bsz, t // tm),
        in_specs=[pl.BlockSpec((None, tm, D), lambda b, i: (b, i, 0)),
                  pl.BlockSpec((None, 8, D), lambda b, i: (b, jnp.maximum(i * r8 - 1, 0), 0)),
                  pl.BlockSpec((None, 8, D), lambda b, i: (b, jnp.minimum((i + 1) * r8, last8), 0)),
                  pl.BlockSpec((8, 3 * D), const),
                  pl.BlockSpec((None, D, N_MAIN + 128), lambda b, i: (layer, 0, 0),
                               pipeline_mode=pl.Buffered(1)),
                  pl.BlockSpec((tm, 128), lambda b, i: (i, 0)),
                  pl.BlockSpec((tm, 128), lambda b, i: (i, 0)),
                  pl.BlockSpec((1, 512), const),
                  pl.BlockSpec((1, 512), const),
                  pl.BlockSpec((512, 512), const),
                  pl.BlockSpec((3, 1536), const),
                  pl.BlockSpec((1, 1536), const)],
        out_specs=[pl.BlockSpec((None, tm, N_MAIN), lambda b, i: (b, i, 0)),
                   pl.BlockSpec((None, tm, 128), lambda b, i: (b, i, 0)),
                   pl.BlockSpec((None, ATT_H, None, VT_ROWS, tm),
                                lambda b, i: (b, 0, i // per_tile, 0, i % per_tile))],
        out_shape=[jax.ShapeDtypeStruct((bsz, t, N_MAIN), BF16),
                   jax.ShapeDtypeStruct((bsz, t, 128), F32),
                   jax.ShapeDtypeStruct((bsz, ATT_H, t // ATT_TK, VT_ROWS, ATT_TK), BF16)],
        compiler_params=pltpu.CompilerParams(
            dimension_semantics=("arbitrary", "arbitrary"), vmem_limit_bytes=VMEM_LIMIT),
        name="inproj",
    )(xcat, xcat, xcat, ada, w_all, cos_t, sin_t, gq, gk, gmat, conv_w, conv_b)


def _attn_kernel(q_ref, qn_ref, k_ref, vt_ref, gate_ref, lam_ref, subln_ref, o_ref,
                 q2_ref, m_ref, acc_ref, s_ref, mt_ref, *, tq, tk, nkv, lam_init):
    qi = pl.program_id(2)
    lane = lax.broadcasted_iota(jnp.int32, (tq, 128), 1)
    m_ref[...] = jnp.full((1, 2 * tq), -jnp.inf, F32)
    acc_ref[...] = jnp.zeros((VT_ROWS, 2 * tq), F32)

    def stack_maps(src_ref):
        q = src_ref[...]
        zero = jnp.zeros_like(q)
        q2_ref[0:tq, :] = jnp.where(lane < QK, q, zero)
        q2_ref[tq:2 * tq, :] = jnp.where(lane >= QK, q, zero)

    def hand_off():
        stack_maps(qn_ref)
        scores(0, 2, False)

    def scores(j, slot, ctx_only):
        st = _dot_nt(k_ref[pl.ds(j * tk, tk), :], q2_ref[...])
        if ctx_only:
            krow = j * tk + lax.broadcasted_iota(jnp.int32, (tk, 1), 0)
            st = jnp.where(krow < N_CTX, st, -jnp.inf)
        s_ref[slot] = st
        mt_ref[slot] = jnp.max(st, axis=0, keepdims=True)

    def consume(j, slot):
        m_old = m_ref[...]
        m_new = jnp.maximum(m_old, mt_ref[slot])
        alpha = jnp.exp2(m_old - m_new)
        p = jnp.exp2(s_ref[slot] - m_new)
        acc_ref[...] = alpha * acc_ref[...] + _dot(vt_ref[j], p.astype(BF16))
        m_ref[...] = m_new

    def finalize():
        lam_v = lam_ref[...]
        lam = (jnp.exp(jnp.sum(lam_v[0:1] * lam_v[1:2], axis=-1, keepdims=True))
               - jnp.exp(jnp.sum(lam_v[2:3] * lam_v[3:4], axis=-1, keepdims=True)) + lam_init)
        inv = 1.0 / acc_ref[128:129, :]
        acc = acc_ref[0:128, :]
        o = acc[:, 0:tq] * inv[:, 0:tq] - lam * (acc[:, tq:2 * tq] * inv[:, tq:2 * tq])
        o = o * lax.rsqrt(jnp.mean(o * o, axis=0, keepdims=True) + EPS)
        o = o * subln_ref[...] * (1.0 - lam_init)
        o_ref[...] = (o.T * _silu(gate_ref[...].astype(F32))).astype(BF16)

    @pl.when(qi == 0)
    def _():
        stack_maps(q_ref)
        scores(0, 0, True)
        hand_off()
        consume(0, 0)
        finalize()

    @pl.when(qi != 0)
    def _():
        for j in range(nkv):
            if j + 1 < nkv:
                scores(j + 1, (j + 1) % 2, False)
            else:
                hand_off()
            consume(j, 2 if j == 0 else j % 2)
        finalize()


def _attn(main, vt, lam_vec, subln, lam_init, tq, tk):
    bsz, t, _ = main.shape
    nkv = t // tk
    assert tq == N_CTX and tk >= N_CTX and nkv >= 2
    nq = t // tq
    return pl.pallas_call(
        functools.partial(_attn_kernel, tq=tq, tk=tk, nkv=nkv, lam_init=lam_init),
        grid=(bsz, ATT_H, nq),
        in_specs=[pl.BlockSpec((None, tq, 128), lambda b, h, i: (b, i, h)),
                  pl.BlockSpec((None, tq, 128), lambda b, h, i: (b, jnp.minimum(i + 1, nq - 1), h)),
                  pl.BlockSpec((None, t, 128), lambda b, h, i: (b, 0, C_K // 128 + h)),
                  pl.BlockSpec((None, None, nkv, VT_ROWS, tk), lambda b, h, i: (b, h, 0, 0, 0)),
                  pl.BlockSpec((None, tq, 128), lambda b, h, i: (b, i, 12 + h)),
                  pl.BlockSpec((4, QK), lambda b, h, i: (0, 0)),
                  pl.BlockSpec((128, 1), lambda b, h, i: (0, 0))],
        out_specs=pl.BlockSpec((None, tq, 128), lambda b, h, i: (b, i, h)),
        out_shape=jax.ShapeDtypeStruct((bsz, t, 512), BF16),
        scratch_shapes=[pltpu.VMEM((2 * tq, 128), BF16),
                        pltpu.VMEM((1, 2 * tq), F32),
                        pltpu.VMEM((VT_ROWS, 2 * tq), F32),
                        pltpu.VMEM((3, tk, 2 * tq), F32),
                        pltpu.VMEM((3, 1, 2 * tq), F32)],
        compiler_params=pltpu.CompilerParams(
            dimension_semantics=("arbitrary", "arbitrary", "arbitrary"),
            vmem_limit_bytes=VMEM_LIMIT),
        name="attn",
    )(main, main, main, vt, main, lam_vec, subln)


def _chunk_index(s, nchunks, backward):
    if not backward:
        return s
    nctx = N_CTX // CHUNK
    return jnp.where(s < nctx, nctx - 1 - s, nchunks - 1 + nctx - s)


def _tri_mask(backward):
    r = lax.broadcasted_iota(jnp.int32, (CHUNK, CHUNK), 0)
    c = lax.broadcasted_iota(jnp.int32, (CHUNK, CHUNK), 1)
    return (c >= r) if backward else (c <= r)


def _ssd_chunk(u_ref, dtr_ref, bias_ref, alog_ref, o_ref, s_ref, fin, backward):
    mask = _tri_mask(backward)
    tri = jnp.where(mask, 1.0, 0.0).astype(BF16)
    pre = dtr_ref[...] + bias_ref[...]
    dt = jnp.maximum(pre, 0.0) + jnp.log1p(jnp.exp(-jnp.abs(pre)))
    la = dt * (-LOG2E * jnp.exp(alog_ref[...]))
    cum = _tri_dot(tri, la)
    cum_t = cum.T
    dt_t = dt.T
    edge = 0 if backward else CHUNK - 1
    etot = jnp.exp2(cum[edge:edge + 1, :])
    wdt_t = jnp.exp2(cum_t[:, edge:edge + 1] - cum_t) * dt_t
    first = lax.broadcasted_iota(jnp.int32, (CHUNK, 128), 1) < SSD_P
    first_row = lax.broadcasted_iota(jnp.int32, (1, 128), 1) < SSD_P
    col0 = SSD_H if backward else 0
    yield

    for g in range(SSD_G):
        k = u_ref[:, 1024 + g * SSD_N:1024 + (g + 1) * SSD_N]
        q = u_ref[:, 1280 + g * SSD_N:1280 + (g + 1) * SSD_N]
        scores = _dot_nt(q, k)
        k_t = k.astype(F32).T
        y_inter = _dot(q, s_ref[g].astype(BF16))
        ys = []
        yield
        for pp in range(SSD_HPG // 2):
            h0 = g * SSD_HPG + 2 * pp
            c0 = col0 + h0
            off = h0 * SSD_P
            xs = u_ref[:, off:off + 128]
            zero = jnp.zeros_like(xs)
            vals = jnp.concatenate([jnp.where(first, xs, zero),
                                    jnp.where(first, zero, xs)], axis=0)
            wts, kws, ecs = [], [], []
            for hh in range(2):
                c = c0 + hh
                cum_c = jnp.broadcast_to(cum[:, c:c + 1], (CHUNK, CHUNK))
                dec = jnp.exp2(jnp.where(mask, cum_c - cum_t[c:c + 1, :], -jnp.inf))
                wts.append((scores * dec * dt_t[c:c + 1, :]).astype(BF16))
                kws.append((k_t * wdt_t[c:c + 1, :]).astype(BF16))
                ecs.append(jnp.exp2(cum_c))
            y = (jnp.where(first, ecs[0], ecs[1]) * y_inter[:, pp * 128:(pp + 1) * 128]
                 + _dot(jnp.concatenate(wts, axis=1), vals))
            etot_p = jnp.where(first_row, jnp.broadcast_to(etot[:, c0:c0 + 1], (1, 128)),
                               jnp.broadcast_to(etot[:, c0 + 1:c0 + 2], (1, 128)))
            s_old = s_ref[g, :, pp * 128:(pp + 1) * 128]
            s_ref[g, :, pp * 128:(pp + 1) * 128] = (
                etot_p * s_old + _dot(jnp.concatenate(kws, axis=1), vals))
            if backward:
                z_ref, yf_ref, dskip_ref = fin[0:3]
                y = y + yf_ref[:, off:off + 128] + xs.astype(F32) * dskip_ref[:, off:off + 128]
                y = y * _silu(z_ref[:, off:off + 128].astype(F32))
                ys.append(y)
            else:
                o_ref[:, off:off + 128] = y
            yield
        if backward:
            gain_ref = fin[3]
            lo, hi = g * 512, (g + 1) * 512
            yg = jnp.concatenate(ys, axis=1)
            yg = yg * lax.rsqrt(jnp.mean(yg * yg, axis=-1, keepdims=True) + EPS)
            o_ref[:, lo:hi] = (yg * gain_ref[:, lo:hi]).astype(o_ref.dtype)


def _ret_tables(backward):
    exps = RET_EXP_B if backward else RET_EXP_F
    lg = np.log1p(-np.exp2(-np.asarray(exps, np.float64)))
    i = np.arange(CHUNK, dtype=np.float64)
    if backward:
        cum = (CHUNK - i)[None, :] * lg[:, None]
        tot = cum[:, 0]
        msk = i[None, :] >= i[:, None]
    else:
        cum = (i + 1.0)[None, :] * lg[:, None]
        tot = cum[:, -1]
        msk = i[None, :] <= i[:, None]
    dec = np.where(msk[None], np.exp(cum[:, :, None] - cum[:, None, :]), 0.0)
    inter = np.broadcast_to(np.exp(cum)[:, :, None], (RET_H, CHUNK, 128))
    toend = np.broadcast_to(np.exp(tot[:, None] - cum)[:, :, None], (RET_H, CHUNK, 128))
    tables = np.stack([dec, inter, toend], axis=1).astype(np.float32)
    return tables, [float(np.exp(v)) for v in tot]


def _ret_chunk(qk_ref, v_ref, cos, sin, tab_ref, o_ref, s_ref, fin, backward, etot):
    lane = lax.broadcasted_iota(jnp.int32, (CHUNK, 128), 1)
    lo_half = (lane % QK) < (QK // 2)
    first = lane < QK
    srow = lax.broadcasted_iota(jnp.int32, (128, 256), 0) < QK
    scol = lax.broadcasted_iota(jnp.int32, (128, 256), 1) < 128
    diag = srow == scol
    for pp in range(RET_H // 2):
        h0 = 2 * pp
        q = _rope128(qk_ref[:, pp * 128:(pp + 1) * 128].astype(F32), cos, sin, lo_half)
        k = _rope128(qk_ref[:, 256 + pp * 128:256 + (pp + 1) * 128].astype(F32) * (QK ** -0.5),
                     cos, sin, lo_half)
        qb = q.astype(BF16)
        kb = k.astype(BF16)
        k_t = kb.astype(F32).T.astype(BF16)
        zero = jnp.zeros_like(kb)
        k2 = jnp.concatenate([jnp.where(first, kb, zero),
                              jnp.where(first, zero, kb)], axis=0)
        scores = _dot_nt(qb, k2)
        v2 = v_ref[:, h0 * 128:(h0 + 2) * 128]
        zv = jnp.zeros((CHUNK, 128), BF16)
        vdiag = jnp.concatenate(
            [jnp.concatenate([v2[:, 0:128], zv], axis=1),
             jnp.concatenate([zv, v2[:, 128:256]], axis=1)], axis=0)
        dec = jnp.concatenate([tab_ref[h0, 0], tab_ref[h0 + 1, 0]], axis=1)
        inter = jnp.concatenate([tab_ref[h0, 1], tab_ref[h0 + 1, 1]], axis=1)
        toend = jnp.concatenate([tab_ref[h0, 2], tab_ref[h0 + 1, 2]], axis=1)
        s_pair = s_ref[pp]
        y = _dot((scores * dec).astype(BF16), vdiag) + inter * _dot(qb, s_pair.astype(BF16))
        upd = _dot(k_t, (v2.astype(F32) * toend).astype(BF16))
        s_ref[pp] = (jnp.where(scol, etot[h0], etot[h0 + 1]) * s_pair
                     + jnp.where(diag, upd, 0.0))
        for hh in range(2):
            h = h0 + hh
            yh = y[:, hh * 128:(hh + 1) * 128]
            if backward:
                g_ref, yf_ref, gain_ref = fin
                yh = yh + yf_ref[:, h * 128:(h + 1) * 128]
                yh = yh * lax.rsqrt(jnp.mean(yh * yh, axis=-1, keepdims=True) + EPS) * gain_ref[...]
                yh = yh * _silu(g_ref[:, h * 128:(h + 1) * 128].astype(F32))
            o_ref[:, h * 128:(h + 1) * 128] = yh.astype(o_ref.dtype)
        yield


def _scan_kernel(*refs, backward, etot, nb):
    if backward:
        (u_ref, dtr_ref, bias_ref, alog_ref, qk_ref, v_ref, cos_ref, sin_ref, tab_ref,
         z_ref, ssd_f_ref, dskip_ref, ssd_gain_ref, g_ref, ret_f_ref, ret_gain_ref,
         ssd_o_ref, ret_o_ref, ssd_s_ref, ret_s_ref) = refs
    else:
        (u_ref, dtr_ref, bias_ref, alog_ref, qk_ref, v_ref, cos_ref, sin_ref, tab_ref,
         ssd_o_ref, ret_o_ref, ssd_s_ref, ret_s_ref) = refs

    @pl.when(pl.program_id(0) == 0)
    def _():
        ssd_s_ref[...] = jnp.zeros_like(ssd_s_ref)
        ret_s_ref[...] = jnp.zeros_like(ret_s_ref)

    cos = cos_ref[...]
    sin = sin_ref[...]
    chains = []
    for b in range(nb):
        ssd_fin = (z_ref.at[b], ssd_f_ref.at[b], dskip_ref, ssd_gain_ref) if backward else None
        ret_fin = (g_ref.at[b], ret_f_ref.at[b], ret_gain_ref) if backward else None
        chains.append(_ssd_chunk(u_ref.at[b], dtr_ref.at[b], bias_ref, alog_ref,
                                 ssd_o_ref.at[b], ssd_s_ref.at[b], ssd_fin, backward))
        chains.append(_ret_chunk(qk_ref.at[b], v_ref.at[b], cos, sin, tab_ref,
                                 ret_o_ref.at[b], ret_s_ref.at[b], ret_fin, backward, etot))
    while chains:
        chains = [c for c in chains if next(c, _DONE) is not _DONE]


def _scan(dtr, main, cos_t, sin_t, bias, alog, backward, fwd=None, dskip=None,
          ssd_gain=None, ret_gain=None):
    bsz, t, _ = main.shape
    nchunks = t // CHUNK
    tables, etot = _ret_tables(backward)
    idx = functools.partial(_chunk_index, nchunks=nchunks, backward=backward)
    row = lambda s: (0, idx(s), 0)
    const = lambda s: (0, 0)
    in_specs = [pl.BlockSpec((bsz, CHUNK, 1536), lambda s: (0, idx(s), C_XBC // 1536)),
                pl.BlockSpec((bsz, CHUNK, 128), row),
                pl.BlockSpec((1, 128), const),
                pl.BlockSpec((1, 128), const),
                pl.BlockSpec((bsz, CHUNK, 512), lambda s: (0, idx(s), 9)),
                pl.BlockSpec((bsz, CHUNK, 512), lambda s: (0, idx(s), 10)),
                pl.BlockSpec((CHUNK, 128), lambda s: (idx(s), 0)),
                pl.BlockSpec((CHUNK, 128), lambda s: (idx(s), 0)),
                pl.BlockSpec((RET_H, 3, CHUNK, 128), lambda s: (0, 0, 0, 0))]
    args = [main, dtr, bias, alog, main, main, cos_t, sin_t, jnp.asarray(tables)]
    if backward:
        ssd_f, ret_f = fwd
        in_specs += [pl.BlockSpec((bsz, CHUNK, 1024), lambda s: (0, idx(s), 2)),
                     pl.BlockSpec((bsz, CHUNK, 1024), row),
                     pl.BlockSpec((1, 1024), const),
                     pl.BlockSpec((1, 1024), const),
                     pl.BlockSpec((bsz, CHUNK, 512), lambda s: (0, idx(s), 11)),
                     pl.BlockSpec((bsz, CHUNK, 512), row),
                     pl.BlockSpec((1, 128), const)]
        args += [main, ssd_f, dskip, ssd_gain, main, ret_f, ret_gain]
    out_dtype = BF16 if backward else F32
    return pl.pallas_call(
        functools.partial(_scan_kernel, backward=backward, etot=etot, nb=bsz),
        grid=(nchunks,),
        in_specs=in_specs,
        out_specs=[pl.BlockSpec((bsz, CHUNK, 1024), row),
                   pl.BlockSpec((bsz, CHUNK, 512), row)],
        out_shape=[jax.ShapeDtypeStruct((bsz, t, 1024), out_dtype),
                   jax.ShapeDtypeStruct((bsz, t, 512), out_dtype)],
        scratch_shapes=[pltpu.VMEM((bsz, SSD_G, SSD_N, 512), F32),
                        pltpu.VMEM((bsz, RET_H // 2, 128, 256), F32)],
        compiler_params=pltpu.CompilerParams(
            dimension_semantics=("arbitrary",), vmem_limit_bytes=VMEM_LIMIT),
        name="scan_bwd" if backward else "scan_fwd",
    )(*args)


def _outproj_kernel(x_ref, a_ref, s_ref, r_ref, ada_ref, w_ref, o_ref, *, tm, blk0, ctx_row):
    b = pl.program_id(0)
    i = pl.program_id(1) + blk0
    acc = _dot(a_ref[...], w_ref[0:512, :])
    acc += _dot(s_ref[...], w_ref[512:1536, :])
    acc += _dot(r_ref[...], w_ref[1536:2048, :])
    rows = i * tm + lax.broadcasted_iota(jnp.int32, (tm, 1), 0)
    gate = _row_mod(ada_ref, b, rows, 2 * D, 3 * D, ctx_row)
    o_ref[...] = x_ref[...] + gate * acc


def _outproj(xcat, attn_o, ssd_o, ret_o, ada, w_out, layer, ctx_row, latent_only):
    bsz, t, _ = xcat.shape
    tm = N_CTX if latent_only else 768
    blk0 = 1 if latent_only else 0
    nb = t // tm - blk0
    row = lambda b, i: (b, i + blk0, 0)
    return pl.pallas_call(
        functools.partial(_outproj_kernel, tm=tm, blk0=blk0, ctx_row=ctx_row),
        grid=(bsz, nb),
        in_specs=[pl.BlockSpec((None, tm, D), row),
                  pl.BlockSpec((None, tm, 512), row),
                  pl.BlockSpec((None, tm, 1024), row),
                  pl.BlockSpec((None, tm, 512), row),
                  pl.BlockSpec((8, 3 * D), lambda b, i: (0, 0)),
                  pl.BlockSpec((None, 2 * D, D), lambda b, i: (layer, 0, 0))],
        out_specs=pl.BlockSpec((None, tm, D), lambda b, i: (b, i, 0)),
        out_shape=jax.ShapeDtypeStruct((bsz, nb * tm, D), F32),
        compiler_params=pltpu.CompilerParams(
            dimension_semantics=("arbitrary", "arbitrary"), vmem_limit_bytes=VMEM_LIMIT),
        name="outproj",
    )(xcat, attn_o, ssd_o, ret_o, ada, w_out)


def _rope_tables(seq):
    n_rows = seq // GRID_W
    row = np.repeat(np.arange(n_rows, dtype=np.float32), GRID_W)
    col = np.tile(np.arange(GRID_W, dtype=np.float32), n_rows)
    n_freq = QK // 4
    inv_freq = (np.float32(ROPE_BASE) ** (-np.arange(n_freq, dtype=np.float32) / n_freq)).astype(np.float32)
    ang = np.concatenate([row[:, None] * inv_freq, col[:, None] * inv_freq], axis=-1).astype(np.float32)
    cos = np.concatenate([np.ones((N_CTX, QK // 2)), np.cos(ang.astype(np.float64))], axis=0)
    sin = np.concatenate([np.zeros((N_CTX, QK // 2)), np.sin(ang.astype(np.float64))], axis=0)
    cos_t = np.tile(cos, (1, 4)).astype(np.float32)
    sin_t = np.tile(np.concatenate([-sin, sin], axis=1), (1, 2)).astype(np.float32)
    return jnp.asarray(cos_t), jnp.asarray(sin_t)


_MAIN_COLS = ((0, 2048), (3616, 4640), (2048, 3584), (4640, 6176))
_DT_COLS = (3584, 3616)


def _wprep_kernel(w_ref, o_ref):
    w = w_ref[...].astype(F32)
    pad = jnp.zeros((w.shape[0], 128 - (_DT_COLS[1] - _DT_COLS[0])), F32)
    pieces = [w[:, a:b] for a, b in _MAIN_COLS + (_DT_COLS,)] + [pad]
    o_ref[...] = jnp.concatenate(pieces, axis=1).astype(BF16)


def _wprep(w_in):
    depth, d, n = w_in.shape
    tr = 128
    return pl.pallas_call(
        _wprep_kernel,
        grid=(depth, d // tr),
        in_specs=[pl.BlockSpec((None, tr, n), lambda l, r: (l, r, 0))],
        out_specs=pl.BlockSpec((None, tr, N_MAIN + 128), lambda l, r: (l, r, 0)),
        out_shape=jax.ShapeDtypeStruct((depth, d, N_MAIN + 128), BF16),
        name="wprep",
    )(w_in)


def _pad_lanes(v, n=128):
    v = v.reshape(1, -1)
    return jnp.pad(v, ((0, 0), (0, n - v.shape[1])))


def kernel(x, c, ctx, c_ctx, w_ada, b_ada, w_in, w_out, attn_q_norm, attn_k_norm,
           lambda_q1, lambda_k1, lambda_q2, lambda_k2, attn_subln,
           ssd_conv_w, ssd_conv_b, ssd_dt_bias, ssd_a_log, ssd_d, ssd_norm, ret_norm):
    bsz, seq, _ = x.shape
    depth = w_ada.shape[0]
    assert bsz + 1 <= 8 and ctx.shape[1] == N_CTX
    ctx_row = bsz
    xcat = jnp.concatenate([ctx, x], axis=1)
    cvec = jnp.zeros((8, D), F32).at[0:bsz].set(c).at[ctx_row].set(c_ctx)
    cos_t, sin_t = _rope_tables(seq)
    gsel = np.arange(512) // QK
    gmat = jnp.asarray((gsel[:, None] == gsel[None, :]).astype(np.float32), BF16)

    w_perm = _wprep(w_in.astype(BF16))
    w_out_bf = w_out.astype(BF16)

    for layer in range(depth):
        last = layer == depth - 1
        lam_init = 0.8 - 0.6 * math.exp(-0.3 * layer)

        ada = _ada(cvec, w_ada, layer, b_ada[layer].reshape(1, -1))
        main, dtr, vt = _inproj(xcat, ada, w_perm, layer, cos_t, sin_t,
                                jnp.tile(attn_q_norm[layer], 8).reshape(1, 512),
                                jnp.tile(attn_k_norm[layer], 8).reshape(1, 512), gmat,
                                ssd_conv_w[layer], ssd_conv_b[layer].reshape(1, -1), ctx_row)

        lam_vec = jnp.stack([lambda_q1[layer], lambda_k1[layer], lambda_q2[layer], lambda_k2[layer]])
        attn_o = _attn(main, vt, lam_vec, attn_subln[layer].reshape(128, 1), lam_init,
                       tq=N_CTX, tk=ATT_TK)

        bias = _pad_lanes(ssd_dt_bias[layer])
        alog = _pad_lanes(ssd_a_log[layer])
        fwd = _scan(dtr, main, cos_t, sin_t, bias, alog, backward=False)
        ssd_o, ret_o = _scan(dtr, main, cos_t, sin_t, bias, alog, backward=True, fwd=fwd,
                             dskip=jnp.repeat(ssd_d[layer], SSD_P).reshape(1, 1024),
                             ssd_gain=ssd_norm[layer].reshape(1, 1024),
                             ret_gain=ret_norm[layer].reshape(1, 128))

        xcat = _outproj(xcat, attn_o, ssd_o, ret_o, ada, w_out_bf, layer, ctx_row,
                        latent_only=last)
    return xcat
```
